```python
import math
import jax, jax.numpy as jnp
from jax import lax
import numpy as np

D_MODEL = 1024
BATCH = 16
SEQ = 256
DEPTH = 2
DEC_BATCH = 2
DEC_SEQ = 4096
PAST_LEN = 512

GRID_W = 64
MIX_W = D_MODEL
N_HEADS = 4
QK_DIM = 64
V_DIM = 2 * QK_DIM
ATTN_W = N_HEADS * V_DIM
F_GROUPS = 4
F_CH = (MIX_W - ATTN_W) // F_GROUPS
FOURIER_W = F_GROUPS * F_CH
QK_W = N_HEADS * 2 * QK_DIM
IN_W = 2 * QK_W + ATTN_W + FOURIER_W
ROPE_THETA = 10000.0
Q_BLOCK = 128
D_FF = 2816
N_EXPERTS = 8
TOP_K = 2
D_FF_E = 1024
N_DENSE = (DEPTH + 1) // 2
N_MOE = DEPTH // 2
N_MOD = 6
EPS = 1e-6

kernel_name = "hybrid_diffattn_fnet_prefix_dit_step"


def rms_norm(x, g):
    xf = x.astype(jnp.float32)
    y = xf * lax.rsqrt(jnp.mean(xf * xf, axis=-1, keepdims=True) + EPS)
    return (y * g.astype(jnp.float32)).astype(x.dtype)


def axial_rope_tables(T):
    rows = T // GRID_W
    row = jnp.repeat(jnp.arange(rows), GRID_W)
    col = jnp.tile(jnp.arange(GRID_W), rows)
    half = QK_DIM // 2
    inv = 1.0 / (ROPE_THETA ** (jnp.arange(0, half, 2, dtype=jnp.float32) / half))

    def tab(pos):
        ang = pos.astype(jnp.float32)[:, None] * inv[None, :]
        return jnp.concatenate([ang, ang], axis=-1)

    ang = jnp.concatenate([tab(row), tab(col)], axis=-1)
    return jnp.cos(ang), jnp.sin(ang)


def apply_axial_rope(x, cos, sin):
    qd = QK_DIM // 4
    xs = x.reshape(x.shape[:-1] + (2, 2, qd))
    rot = jnp.stack([-xs[..., 1, :], xs[..., 0, :]], axis=-2).reshape(x.shape)
    c = cos[None, :, None, None, :]
    s = sin[None, :, None, None, :]
    return (x.astype(jnp.float32) * c + rot.astype(jnp.float32) * s).astype(x.dtype)


def diff_attention(q, k, v, lam, g_subln, lam_init):
    B, Tq = q.shape[0], q.shape[1]
    nb = Tq // Q_BLOCK
    qb = jnp.moveaxis(q.reshape((B, nb, Q_BLOCK) + q.shape[2:]), 1, 0)
    scale = QK_DIM ** -0.5

    def block(qblk):
        s = jnp.einsum('bqhmd,bkhmd->bmhqk', qblk, k, preferred_element_type=jnp.float32) * scale
        p = jax.nn.softmax(s, axis=-1)
        a = p[:, 0] - lam * p[:, 1]
        return jnp.einsum('bhqk,bkhv->bqhv', a.astype(v.dtype), v)

    o = lax.map(block, qb)
    o = jnp.moveaxis(o, 0, 1).reshape(B, Tq, N_HEADS, V_DIM)
    o = rms_norm(o, g_subln) * (1.0 - lam_init)
    return o.reshape(B, Tq, ATTN_W)


def fourier_mix(f, w_fnet):
    B, T = f.shape[0], f.shape[1]
    fg = f.astype(jnp.float32).reshape(B, T, F_GROUPS, F_CH)
    mixed = jnp.fft.fft2(fg, axes=(1, 3), norm="ortho").real.astype(f.dtype)
    return jnp.einsum('btgc,gce->btge', mixed, w_fnet).reshape(B, T, FOURIER_W)


def swiglu(x, w_gate, w_up, w_down):
    return (jax.nn.silu(x @ w_gate) * (x @ w_up)) @ w_down


def moe_ffn(x, w_router, e_gate, e_up, e_down):
    logits = jnp.einsum('btd,de->bte', x, w_router, preferred_element_type=jnp.float32)
    top_v, top_i = lax.top_k(logits, TOP_K)
    w = jax.nn.softmax(top_v, axis=-1)
    gates = jnp.sum(jax.nn.one_hot(top_i, N_EXPERTS, dtype=jnp.float32) * w[..., None], axis=-2)
    out = jnp.zeros(x.shape, jnp.float32)
    for e in range(N_EXPERTS):
        out = out + gates[..., e:e + 1] * swiglu(x, e_gate[e], e_up[e], e_down[e]).astype(jnp.float32)
    return out.astype(x.dtype)


def run_layer(x, cond, l, P, rope, ctx_k, ctx_v):
    B, T, _ = x.shape
    lam_init = 0.8 - 0.6 * math.exp(-0.3 * l)
    mod = jnp.einsum('bd,de->be', jax.nn.silu(cond.astype(jnp.float32)),
                     P['w_ada'][l].astype(jnp.float32)) + P['b_ada'][l].astype(jnp.float32)
    mod = mod.astype(x.dtype)[:, None, :]
    sh_a, sc_a, gt_a, sh_f, sc_f, gt_f = jnp.split(mod, N_MOD, axis=-1)

    hn = rms_norm(x, P['g_attn_pre'][l]) * (1 + sc_a) + sh_a
    proj = hn @ P['w_in'][l]
    q = proj[..., :QK_W].reshape(B, T, N_HEADS, 2, QK_DIM)
    k = proj[..., QK_W:2 * QK_W].reshape(B, T, N_HEADS, 2, QK_DIM)
    v = proj[..., 2 * QK_W:2 * QK_W + ATTN_W].reshape(B, T, N_HEADS, V_DIM)
    f = proj[..., 2 * QK_W + ATTN_W:]
    lp = P['lam_params'][l].astype(jnp.float32)
    lam = jnp.exp(jnp.sum(lp[0] * lp[1])) - jnp.exp(jnp.sum(lp[2] * lp[3])) + lam_init
    if rope is None:
        keys, vals = k, v
        qr = q
    else:
        cos, sin = rope
        qr = apply_axial_rope(q, cos, sin)
        keys = jnp.concatenate([ctx_k.astype(k.dtype), apply_axial_rope(k, cos, sin)], axis=1)
        vals = jnp.concatenate([ctx_v.astype(v.dtype), v], axis=1)
    attn_o = diff_attention(qr, keys, vals, lam, P['g_subln'][l], lam_init)
    four_o = fourier_mix(f, P['w_fnet'][l])
    mixed = jnp.concatenate([attn_o, four_o], axis=-1) @ P['w_out'][l]
    x = x + gt_a * rms_norm(mixed, P['g_attn_post'][l])

    hn = rms_norm(x, P['g_ffn_pre'][l]) * (1 + sc_f) + sh_f
    i = l // 2
    if l % 2 == 0:
        ff = swiglu(hn, P['w_gate'][i], P['w_up'][i], P['w_down'][i])
    else:
        ff = moe_ffn(hn, P['w_router'][i], P['e_gate'][i], P['e_up'][i], P['e_down'][i])
    x = x + gt_f * rms_norm(ff, P['g_ffn_post'][l])
    return x, k, v


def setup_inputs(seed: int = 0) -> dict:
    key = jax.random.key(seed)
    ks = jax.random.split(key, 24)
    D = D_MODEL

    def nrm(k, shape, s):
        return jax.random.normal(k, shape, jnp.float32) * s

    return {
        "x_prompt": nrm(ks[0], (BATCH, SEQ, D), 1.0),
        "x_sample": nrm(ks[1], (DEC_BATCH, DEC_SEQ, D), 1.0),
        "cache_k": nrm(ks[2], (DEC_BATCH, DEPTH, PAST_LEN, N_HEADS, 2, QK_DIM), 1.0),
        "cache_v": nrm(ks[3], (DEC_BATCH, DEPTH, PAST_LEN, N_HEADS, V_DIM), 1.0),
        "c": nrm(ks[4], (DEC_BATCH, D), 1.0),
        "c_ctx": nrm(ks[5], (D,), 1.0),
        "g_attn_pre": 1.0 + nrm(ks[6], (DEPTH, D), 0.05),
        "g_attn_post": 1.0 + nrm(ks[7], (DEPTH, D), 0.05),
        "g_ffn_pre": 1.0 + nrm(ks[8], (DEPTH, D), 0.05),
        "g_ffn_post": 1.0 + nrm(ks[9], (DEPTH, D), 0.05),
        "w_ada": nrm(ks[10], (DEPTH, D, N_MOD * D), 0.5 * D ** -0.5),
        "b_ada": nrm(ks[11], (DEPTH, N_MOD * D), 0.01),
        "w_in": nrm(ks[12], (DEPTH, D, IN_W), D ** -0.5),
        "lam_params": nrm(ks[13], (DEPTH, 4, QK_DIM), 0.1),
        "g_subln": 1.0 + nrm(ks[14], (DEPTH, V_DIM), 0.05),
        "w_fnet": nrm(ks[15], (DEPTH, F_GROUPS, F_CH, F_CH), F_CH ** -0.5),
        "w_out": nrm(ks[16], (DEPTH, MIX_W, D), MIX_W ** -0.5),
        "w_gate": nrm(ks[17], (N_DENSE, D, D_FF), D ** -0.5),
        "w_up": nrm(ks[18], (N_DENSE, D, D_FF), D ** -0.5),
        "w_down": nrm(ks[19], (N_DENSE, D_FF, D), D_FF ** -0.5),
        "w_router": nrm(ks[20], (N_MOE, D, N_EXPERTS), D ** -0.5),
        "e_gate": nrm(ks[21], (N_MOE, N_EXPERTS, D, D_FF_E), D ** -0.5),
        "e_up": nrm(ks[22], (N_MOE, N_EXPERTS, D, D_FF_E), D ** -0.5),
        "e_down": nrm(ks[23], (N_MOE, N_EXPERTS, D_FF_E, D), D_FF_E ** -0.5),
    }


def reference(x_prompt, x_sample, cache_k, cache_v, c, c_ctx, g_attn_pre, g_attn_post, g_ffn_pre,
              g_ffn_post, w_ada, b_ada, w_in, lam_params, g_subln, w_fnet, w_out, w_gate, w_up, w_down,
              w_router, e_gate, e_up, e_down):
    P = {
        'g_attn_pre': g_attn_pre, 'g_attn_post': g_attn_post, 'g_ffn_pre': g_ffn_pre,
        'g_ffn_post': g_ffn_post, 'w_ada': w_ada, 'b_ada': b_ada, 'w_in': w_in,
        'lam_params': lam_params, 'g_subln': g_subln, 'w_fnet': w_fnet, 'w_out': w_out,
        'w_gate': w_gate, 'w_up': w_up, 'w_down': w_down, 'w_router': w_router,
        'e_gate': e_gate, 'e_up': e_up, 'e_down': e_down,
    }
    h = x_prompt
    ks_list = []
    vs_list = []
    for l in range(DEPTH):
        h, k_l, v_l = run_layer(h, c_ctx[None, :], l, P, None, None, None)
        ks_list.append(k_l)
        vs_list.append(v_l)
    y_prompt = h
    new_cache_k = jnp.stack(ks_list, axis=1)
    new_cache_v = jnp.stack(vs_list, axis=1)

    rope = axial_rope_tables(x_sample.shape[1])
    h = x_sample
    for l in range(DEPTH):
        h, _, _ = run_layer(h, c, l, P, rope, cache_k[:, l], cache_v[:, l])
    y_sample = h
    return (y_prompt, y_sample, new_cache_k, new_cache_v)
```

```python
import functools
import math

import numpy as np
import jax
import jax.numpy as jnp
from jax import lax
from jax.experimental import pallas as pl
from jax.experimental.pallas import tpu as pltpu

F32 = jnp.float32
BF16 = jnp.bfloat16

D = 1024
N_CTX_B = 16
T_CTX = 256
N_LAT_B = 2
T_LAT = 4096
T_PAST = 512
DEPTH = 2
GRID_W = 64
N_HEADS = 4
QK_DIM = 64
HEAD_W = 2 * QK_DIM
ATTN_W = N_HEADS * HEAD_W
F_GROUPS = 4
F_CH = 128
FOUR_W = F_GROUPS * F_CH
IN_W = 3 * ATTN_W + FOUR_W
QKV_W = 3 * ATTN_W
D_FF = 2816
N_EXPERTS = 8
D_FF_E = 1024
N_MOD = 6
EPS = 1e-6
ROPE_THETA = 10000.0
N_GROUPS_PAD = 8
E_PAD = 128

TM = 512
TQ = 256
TK = 512
FFT_R = 64
FFT_S = 8
VMEM_LIMIT = 56 * 1024 * 1024


def _cparams(sem):
    return pltpu.CompilerParams(dimension_semantics=sem, vmem_limit_bytes=VMEM_LIMIT)


def _rms(x, g):
    ms = jnp.mean(x * x, axis=-1, keepdims=True)
    return x * lax.rsqrt(ms + EPS) * g


def _silu(x):
    return x * (1.0 / (1.0 + jnp.exp(-x)))


def _ada_body(c_ref, w_ref, b_ref, o_ref):
    s = _silu(c_ref[...])
    o_ref[...] = jnp.dot(s, w_ref[...], preferred_element_type=F32,
                         precision=lax.Precision.HIGHEST) + b_ref[...]


def _ada(cond, w_ada, b_ada):
    tn = 1536
    return pl.pallas_call(
        _ada_body,
        grid=(DEPTH, N_MOD * D // tn),
        in_specs=[pl.BlockSpec((N_GROUPS_PAD, D), lambda l, j: (0, 0)),
                  pl.BlockSpec((None, D, tn), lambda l, j: (l, 0, j)),
                  pl.BlockSpec((None, 1, tn), lambda l, j: (l, 0, j))],
        out_specs=pl.BlockSpec((None, N_GROUPS_PAD, tn), lambda l, j: (l, 0, j)),
        out_shape=jax.ShapeDtypeStruct((DEPTH, N_GROUPS_PAD, N_MOD * D), F32),
        compiler_params=_cparams(("arbitrary", "arbitrary")),
        name="ada_mod",
    )(cond, w_ada, b_ada.reshape(DEPTH, 1, N_MOD * D))


def _wfold_body(cc_ref, sc_ref, w_ref, o_ref):
    w = w_ref[...]
    o_ref[0:F_CH, :] = jnp.dot(cc_ref[...], w, preferred_element_type=F32,
                               precision=lax.Precision.HIGHEST)
    o_ref[F_CH:2 * F_CH, :] = jnp.dot(sc_ref[...], w, preferred_element_type=F32,
                                      precision=lax.Precision.HIGHEST)


def _wfold(w_fnet):
    ang = 2.0 * np.pi * np.outer(np.arange(F_CH), np.arange(F_CH)) / F_CH
    cc = jnp.asarray(np.cos(ang), F32)
    sc = jnp.asarray(np.sin(ang), F32)
    cspec = pl.BlockSpec((F_CH, F_CH), lambda l, g: (0, 0))
    return pl.pallas_call(
        _wfold_body,
        grid=(DEPTH, F_GROUPS),
        in_specs=[cspec, cspec, pl.BlockSpec((None, None, F_CH, F_CH), lambda l, g: (l, g, 0, 0))],
        out_specs=pl.BlockSpec((None, None, 2 * F_CH, F_CH), lambda l, g: (l, g, 0, 0)),
        out_shape=jax.ShapeDtypeStruct((DEPTH, F_GROUPS, 2 * F_CH, F_CH), F32),
        compiler_params=_cparams(("arbitrary", "arbitrary")),
        name="fnet_fold",
    )(cc, sc, w_fnet)


def _group_of(i, group0, tiles_per_group):
    if tiles_per_group is None:
        return group0
    return group0 + lax.div(i, jnp.int32(tiles_per_group))


def _inproj_body(x_ref, sh_ref, sc_ref, g_ref, w_ref, *rest, rope, group0, tiles_per_group):
    if rope:
        cos_ref, sin_ref, qkv_ref, f_ref = rest
    else:
        qkv_ref, f_ref, k32_ref, v32_ref = rest
    grp = _group_of(pl.program_id(0), group0, tiles_per_group)
    sh = sh_ref[pl.ds(grp, 1), :]
    sc = sc_ref[pl.ds(grp, 1), :]
    hn = _rms(x_ref[...], g_ref[...]) * (1.0 + sc) + sh
    proj = jnp.dot(hn.astype(BF16), w_ref[...], preferred_element_type=F32)
    qk = proj[:, :2 * ATTN_W]
    v = proj[:, 2 * ATTN_W:QKV_W]
    if rope:
        cos = jnp.tile(cos_ref[...], (1, 2 * ATTN_W // HEAD_W))
        sin = jnp.tile(sin_ref[...], (1, 2 * ATTN_W // HEAD_W))
        lane = lax.broadcasted_iota(jnp.int32, qk.shape, 1)
        low = (lane % 32) < 16
        rot = jnp.where(low, pltpu.roll(qk, 2 * ATTN_W - 16, 1), pltpu.roll(qk, 16, 1))
        qk = qk * cos + rot * sin
    else:
        k32_ref[...] = qk[:, ATTN_W:]
        v32_ref[...] = v
    qkv_ref[:, 0:ATTN_W] = (qk[:, :ATTN_W] * (QK_DIM ** -0.5)).astype(BF16)
    qkv_ref[:, ATTN_W:2 * ATTN_W] = qk[:, ATTN_W:].astype(BF16)
    qkv_ref[:, 2 * ATTN_W:] = v.astype(BF16)
    f_ref[...] = proj[:, QKV_W:]


def _inproj(x, mod, layer, g_pre, w_in_bf, rope_tabs, group0, tiles_per_group):
    rows = x.shape[0]
    rope = rope_tabs is not None
    row = lambda i: (i, 0)
    const = lambda i: (0, 0)
    in_specs = [pl.BlockSpec((TM, D), row),
                pl.BlockSpec((None, N_GROUPS_PAD, D), lambda i: (layer, 0, 0)),
                pl.BlockSpec((None, N_GROUPS_PAD, D), lambda i: (layer, 0, 1)),
                pl.BlockSpec((1, D), const),
                pl.BlockSpec((D, IN_W), const)]
    args = [x, mod, mod, g_pre.reshape(1, D), w_in_bf]
    out_specs = [pl.BlockSpec((TM, QKV_W), row), pl.BlockSpec((TM, FOUR_W), row)]
    out_shape = [jax.ShapeDtypeStruct((rows, QKV_W), BF16), jax.ShapeDtypeStruct((rows, FOUR_W), F32)]
    if rope:
        tiles_per_seq = T_LAT // TM
        tab = pl.BlockSpec((TM, HEAD_W), lambda i: (i % tiles_per_seq, 0))
        in_specs += [tab, tab]
        args += list(rope_tabs)
    else:
        out_specs += [pl.BlockSpec((TM, ATTN_W), row)] * 2
        out_shape += [jax.ShapeDtypeStruct((rows, ATTN_W), F32)] * 2
    return pl.pallas_call(
        functools.partial(_inproj_body, rope=rope, group0=group0, tiles_per_group=tiles_per_group),
        grid=(rows // TM,),
        in_specs=in_specs, out_specs=out_specs, out_shape=out_shape,
        compiler_params=_cparams(("arbitrary",)),
        name="inproj_lat" if rope else "inproj_ctx",
    )(*args)


def _rope_tables():
    half = QK_DIM // 2
    inv = 1.0 / (ROPE_THETA ** (np.arange(0, half, 2, dtype=np.float64) / half))
    pos = np.arange(T_LAT)
    def tab(p):
        ang = p[:, None].astype(np.float64) * inv[None, :]
        return np.concatenate([ang, ang], axis=-1)
    ang = np.concatenate([tab(pos // GRID_W), tab(pos % GRID_W)], axis=-1)
    sign = np.where((np.arange(QK_DIM) % 32) < 16, -1.0, 1.0)
    cos = np.tile(np.cos(ang), (1, 2))
    sin = np.tile(np.sin(ang) * sign[None, :], (1, 2))
    return jnp.asarray(cos, F32), jnp.asarray(sin, F32)


def _attn_body(lam_ref, gs_ref, q_ref, *rest, tq, chunks, lam_init):
    o_ref = rest[-1]
    kv_refs = rest[:-1]
    lp = lam_ref[...]
    lam = (jnp.exp(jnp.sum(lp[0:1] * lp[1:2], keepdims=True))
           - jnp.exp(jnp.sum(lp[2:3] * lp[3:4], keepdims=True)) + lam_init)
    q = q_ref[...]
    lane = lax.broadcasted_iota(jnp.int32, q.shape, 1)
    zero = jnp.zeros_like(q)
    qq = jnp.concatenate([jnp.where(lane < QK_DIM, q, zero), jnp.where(lane >= QK_DIM, q, zero)], axis=0)

    def step(kb, vb, carry):
        m, l, acc = carry
        s = lax.dot_general(qq, kb, (((1,), (1,)), ((), ())), preferred_element_type=F32)
        m_new = jnp.maximum(m, jnp.max(s, axis=-1, keepdims=True))
        alpha = jnp.exp(m - m_new)
        p = jnp.exp(s - m_new)
        l = alpha * l + jnp.sum(p, axis=-1, keepdims=True)
        acc = alpha * acc + jnp.dot(p.astype(BF16), vb, preferred_element_type=F32)
        return m_new, l, acc

    carry = (jnp.full((2 * tq, 1), -jnp.inf, F32), jnp.zeros((2 * tq, 1), F32),
             jnp.zeros((2 * tq, HEAD_W), F32))
    for idx, (n_chunks, tk) in enumerate(chunks):
        k_ref, v_ref = kv_refs[2 * idx], kv_refs[2 * idx + 1]
        if n_chunks == 1:
            carry = step(k_ref[...], v_ref[...], carry)
        else:
            def body(c, carry, k_ref=k_ref, v_ref=v_ref, tk=tk):
                start = pl.multiple_of(c * tk, tk)
                return step(k_ref[pl.ds(start, tk), :], v_ref[pl.ds(start, tk), :], carry)
            carry = lax.fori_loop(0, n_chunks, body, carry)
    _, l, acc = carry
    o = acc / l
    a = o[:tq] - lam * o[tq:]
    o_ref[...] = (_rms(a, gs_ref[...]) * (1.0 - lam_init)).astype(o_ref.dtype)


def _attn_ctx(qkv, lam_params, g_subln, layer, lam_init):
    qkv3 = qkv.reshape(N_CTX_B, T_CTX, QKV_W)
    blk = lambda off: pl.BlockSpec((None, T_CTX, HEAD_W), lambda b, h: (b, 0, off + h))
    return pl.pallas_call(
        functools.partial(_attn_body, tq=T_CTX, chunks=((1, T_CTX),), lam_init=lam_init),
        grid=(N_CTX_B, N_HEADS),
        in_specs=[pl.BlockSpec((None, 4, QK_DIM), lambda b, h: (layer, 0, 0)),
                  pl.BlockSpec((None, 1, HEAD_W), lambda b, h: (layer, 0, 0)),
                  blk(0), blk(N_HEADS), blk(2 * N_HEADS)],
        out_specs=pl.BlockSpec((None, T_CTX, HEAD_W), lambda b, h: (b, 0, h)),
        out_shape=jax.ShapeDtypeStruct((N_CTX_B, T_CTX, ATTN_W), BF16),
        compiler_params=_cparams(("arbitrary", "arbitrary")),
        name="attn_ctx",
    )(lam_params, g_subln.reshape(DEPTH, 1, HEAD_W), qkv3, qkv3, qkv3)


def _attn_lat(qkv, ck, cv, lam_params, g_subln, layer, lam_init):
    qkv3 = qkv.reshape(N_LAT_B, T_LAT, QKV_W)
    full = lambda off: pl.BlockSpec((None, T_LAT, HEAD_W), lambda b, h, i: (b, 0, off + h))
    past = pl.BlockSpec((None, T_PAST, HEAD_W), lambda b, h, i: (b, 0, h))
    return pl.pallas_call(
        functools.partial(_attn_body, tq=TQ, chunks=((1, T_PAST), (T_LAT // TK, TK)), lam_init=lam_init),
        grid=(N_LAT_B, N_HEADS, T_LAT // TQ),
        in_specs=[pl.BlockSpec((None, 4, QK_DIM), lambda b, h, i: (layer, 0, 0)),
                  pl.BlockSpec((None, 1, HEAD_W), lambda b, h, i: (layer, 0, 0)),
                  pl.BlockSpec((None, TQ, HEAD_W), lambda b, h, i: (b, i, h)),
                  past, past, full(N_HEADS), full(2 * N_HEADS)],
        out_specs=pl.BlockSpec((None, TQ, HEAD_W), lambda b, h, i: (b, i, h)),
        out_shape=jax.ShapeDtypeStruct((N_LAT_B, T_LAT, ATTN_W), BF16),
        compiler_params=_cparams(("arbitrary", "arbitrary", "arbitrary")),
        name="attn_lat",
    )(lam_params, g_subln.reshape(DEPTH, 1, HEAD_W), qkv3, ck, cv, qkv3, qkv3)


def _fold_groups(ur, ui, wf_ref, scale):
    outs = []
    for g in range(F_GROUPS):
        sl = slice(g * F_CH, (g + 1) * F_CH)
        lhs = jnp.concatenate([ur[:, sl], ui[:, sl]], axis=1).astype(BF16)
        outs.append(jnp.dot(lhs, wf_ref[g], preferred_element_type=F32))
    return jnp.concatenate(outs, axis=1) * scale


def _four_ctx_body(f_ref, dft_ref, wf_ref, o_ref):
    u = jnp.dot(dft_ref[...], f_ref[...].astype(BF16), preferred_element_type=F32)
    o_ref[...] = _fold_groups(u[:T_CTX], u[T_CTX:], wf_ref, 1.0 / math.sqrt(T_CTX * F_CH)).astype(o_ref.dtype)


def _four_ctx(f, wfold_bf, layer):
    ang = 2.0 * np.pi * np.outer(np.arange(T_CTX), np.arange(T_CTX)) / T_CTX
    dft = jnp.asarray(np.concatenate([np.cos(ang), -np.sin(ang)], axis=0), F32).astype(BF16)
    return pl.pallas_call(
        _four_ctx_body,
        grid=(N_CTX_B,),
        in_specs=[pl.BlockSpec((None, T_CTX, FOUR_W), lambda b: (b, 0, 0)),
                  pl.BlockSpec((2 * T_CTX, T_CTX), lambda b: (0, 0)),
                  pl.BlockSpec((None, F_GROUPS, 2 * F_CH, F_CH), lambda b: (layer, 0, 0, 0))],
        out_specs=pl.BlockSpec((None, T_CTX, FOUR_W), lambda b: (b, 0, 0)),
        out_shape=jax.ShapeDtypeStruct((N_CTX_B, T_CTX, FOUR_W), BF16),
        compiler_params=_cparams(("arbitrary",)),
        name="fourier_ctx",
    )(f.reshape(N_CTX_B, T_CTX, FOUR_W), dft, wfold_bf)


def _fft_a_body(x_ref, m_ref, twc_ref, tws_ref, o_ref):
    nb = FFT_R * FFT_S
    x = x_ref[...].reshape(nb, FOUR_W).astype(BF16)
    g = jnp.dot(m_ref[...], x, preferred_element_type=F32)
    gr, gi = g[:nb], g[nb:]
    twc = jnp.tile(twc_ref[...], (1, FOUR_W // 128))
    tws = jnp.tile(tws_ref[...], (1, FOUR_W // 128))
    o_ref[:, 0:FOUR_W] = gr * twc + gi * tws
    o_ref[:, FOUR_W:] = gi * twc - gr * tws


def _fft_b_body(h_ref, mc_ref, ms_ref, wf_ref, o_ref):
    nb = FFT_R * FFT_S
    h = h_ref[...].reshape(nb, 2 * FOUR_W).astype(BF16)
    p = jnp.dot(mc_ref[...], h, preferred_element_type=F32)
    q = jnp.dot(ms_ref[...], h, preferred_element_type=F32)
    ur = p[:, :FOUR_W] + q[:, FOUR_W:]
    ui = p[:, FOUR_W:] - q[:, :FOUR_W]
    out = _fold_groups(ur, ui, wf_ref, 1.0 / math.sqrt(T_LAT * F_CH))
    o_ref[...] = out.reshape(FFT_R, FFT_S, FOUR_W)


def _fft_consts():
    r, s = FFT_R, FFT_S
    nb = r * s
    ang = 2.0 * np.pi * np.outer(np.arange(r), np.arange(r)) / r
    c, sn = np.cos(ang), np.sin(ang)
    eye = np.eye(s)
    ma = np.concatenate([np.einsum('pb,ts->tpbs', c, eye).reshape(nb, nb),
                         np.einsum('pb,ts->tpbs', -sn, eye).reshape(nb, nb)], axis=0)
    mbc = np.einsum('pa,ts->ptas', c, eye).reshape(nb, nb)
    mbs = np.einsum('pa,ts->ptas', sn, eye).reshape(nb, nb)
    tw = 2.0 * np.pi * np.outer(np.arange(r), np.arange(r)).reshape(-1) / (r * r)
    twc = np.broadcast_to(np.cos(tw)[:, None], (r * r, 128))
    tws = np.broadcast_to(np.sin(tw)[:, None], (r * r, 128))
    bf = lambda a: jnp.asarray(a, F32).astype(BF16)
    return bf(ma), bf(mbc), bf(mbs), jnp.asarray(twc, F32), jnp.asarray(tws, F32)


def _four_lat(f, wfold_bf, layer, consts):
    ma, mbc, mbs, twc, tws = consts
    r, s = FFT_R, FFT_S
    nb = r * s
    f4 = f.reshape(N_LAT_B, r, r, FOUR_W)
    h = pl.pallas_call(
        _fft_a_body,
        grid=(N_LAT_B, r // s),
        in_specs=[pl.BlockSpec((None, r, s, FOUR_W), lambda b, j: (b, 0, j, 0)),
                  pl.BlockSpec((2 * nb, nb), lambda b, j: (0, 0)),
                  pl.BlockSpec((nb, 128), lambda b, j: (j, 0)),
                  pl.BlockSpec((nb, 128), lambda b, j: (j, 0))],
        out_specs=pl.BlockSpec((None, nb, 2 * FOUR_W), lambda b, j: (b, j, 0)),
        out_shape=jax.ShapeDtypeStruct((N_LAT_B, T_LAT, 2 * FOUR_W), F32),
        compiler_params=_cparams(("arbitrary", "arbitrary")),
        name="fft_stage_a",
    )(f4, ma, twc, tws)
    h4 = h.reshape(N_LAT_B, r, r, 2 * FOUR_W)
    out = pl.pallas_call(
        _fft_b_body,
        grid=(N_LAT_B, r // s),
        in_specs=[pl.BlockSpec((None, r, s, 2 * FOUR_W), lambda b, j: (b, 0, j, 0)),
                  pl.BlockSpec((nb, nb), lambda b, j: (0, 0)),
                  pl.BlockSpec((nb, nb), lambda b, j: (0, 0)),
                  pl.BlockSpec((None, F_GROUPS, 2 * F_CH, F_CH), lambda b, j: (layer, 0, 0, 0))],
        out_specs=pl.BlockSpec((None, r, s, FOUR_W), lambda b, j: (b, 0, j, 0)),
        out_shape=jax.ShapeDtypeStruct((N_LAT_B, r, r, FOUR_W), F32),
        compiler_params=_cparams(("arbitrary", "arbitrary")),
        name="fft_stage_b",
    )(h4, mbc, mbs, wfold_bf)
    return out.reshape(N_LAT_B * T_LAT, FOUR_W)


def _outproj_body(a_ref, f_ref, x_ref, gt_ref, sh_ref, sc_ref, gpost_ref, gpre_ref, wo_ref, *rest,
                  moe, group0, tiles_per_group):
    if moe:
        wr_ref, x1_ref, hn_ref, gates_ref = rest
    else:
        x1_ref, hn_ref = rest
    grp = _group_of(pl.program_id(0), group0, tiles_per_group)
    mixed = (jnp.dot(a_ref[...], wo_ref[0:ATTN_W, :], preferred_element_type=F32)
             + jnp.dot(f_ref[...].astype(BF16), wo_ref[ATTN_W:, :], preferred_element_type=F32))
    x1 = x_ref[...] + gt_ref[pl.ds(grp, 1), :] * _rms(mixed, gpost_ref[...])
    x1_ref[...] = x1
    hn = _rms(x1, gpre_ref[...]) * (1.0 + sc_ref[pl.ds(grp, 1), :]) + sh_ref[pl.ds(grp, 1), :]
    hn_ref[...] = hn.astype(BF16)
    if moe:
        logits = jnp.dot(hn, wr_ref[...], preferred_element_type=F32, precision=lax.Precision.HIGHEST)
        lane = lax.broadcasted_iota(jnp.int32, logits.shape, 1).astype(F32)
        neg = jnp.float32(-jnp.inf)
        logits = jnp.where(lane < N_EXPERTS, logits, neg)
        m1 = jnp.max(logits, axis=-1, keepdims=True)
        i1 = jnp.min(jnp.where(logits == m1, lane, float(E_PAD)), axis=-1, keepdims=True)
        rest_l = jnp.where(lane == i1, neg, logits)
        m2 = jnp.max(rest_l, axis=-1, keepdims=True)
        i2 = jnp.min(jnp.where(rest_l == m2, lane, float(E_PAD)), axis=-1, keepdims=True)
        e2 = jnp.exp(m2 - m1)
        w1 = 1.0 / (1.0 + e2)
        w2 = e2 / (1.0 + e2)
        gates_ref[...] = jnp.where(lane == i1, w1, 0.0) + jnp.where(lane == i2, w2, 0.0)


def _outproj(attn_o, four_o, x, mod, layer, g_post, g_ffn_pre, w_out_bf, w_router_pad, group0, tiles_per_group):
    rows = x.shape[0]
    moe = w_router_pad is not None
    row = lambda i: (i, 0)
    const = lambda i: (0, 0)
    modspec = lambda k: pl.BlockSpec((None, N_GROUPS_PAD, D), lambda i: (layer, 0, k))
    in_specs = [pl.BlockSpec((TM, ATTN_W), row), pl.BlockSpec((TM, FOUR_W), row), pl.BlockSpec((TM, D), row),
                modspec(2), modspec(3), modspec(4),
                pl.BlockSpec((1, D), const), pl.BlockSpec((1, D), const), pl.BlockSpec((D, D), const)]
    args = [attn_o, four_o, x, mod, mod, mod, g_post.reshape(1, D), g_ffn_pre.reshape(1, D), w_out_bf]
    out_specs = [pl.BlockSpec((TM, D), row), pl.BlockSpec((TM, D), row)]
    out_shape = [jax.ShapeDtypeStruct((rows, D), F32), jax.ShapeDtypeStruct((rows, D), BF16)]
    if moe:
        in_specs.append(pl.BlockSpec((D, E_PAD), const))
        args.append(w_router_pad)
        out_specs.append(pl.BlockSpec((TM, E_PAD), row))
        out_shape.append(jax.ShapeDtypeStruct((rows, E_PAD), F32))
    return pl.pallas_call(
        functools.partial(_outproj_body, moe=moe, group0=group0, tiles_per_group=tiles_per_group),
        grid=(rows // TM,),
        in_specs=in_specs, out_specs=out_specs, out_shape=out_shape,
        compiler_params=_cparams(("arbitrary",)),
        name="outproj_moe" if moe else "outproj",
    )(*args)


def _swiglu(hn, wg, wu, wd):
    g = jnp.dot(hn, wg, preferred_element_type=F32)
    u = jnp.dot(hn, wu, preferred_element_type=F32)
    return jnp.dot((_silu(g) * u).astype(BF16), wd, preferred_element_type=F32)


def _ffn_dense_body(hn_ref, x1_ref, gt_ref, gpost_ref, wg_ref, wu_ref, wd_ref, o_ref, *, group0, tiles_per_group):
    grp = _group_of(pl.program_id(0), group0, tiles_per_group)
    ff = _swiglu(hn_ref[...], wg_ref[...], wu_ref[...], wd_ref[...])
    o_ref[...] = x1_ref[...] + gt_ref[pl.ds(grp, 1), :] * _rms(ff, gpost_ref[...])


def _ffn_dense(hn, x1, mod, layer, g_post, wg, wu, wd, group0, tiles_per_group):
    rows = x1.shape[0]
    row = lambda i: (i, 0)
    const = lambda i: (0, 0)
    resident = lambda shape: pl.BlockSpec(shape, const, pipeline_mode=pl.Buffered(1))
    return pl.pallas_call(
        functools.partial(_ffn_dense_body, group0=group0, tiles_per_group=tiles_per_group),
        grid=(rows // TM,),
        in_specs=[pl.BlockSpec((TM, D), row), pl.BlockSpec((TM, D), row),
                  pl.BlockSpec((None, N_GROUPS_PAD, D), lambda i: (layer, 0, 5)),
                  pl.BlockSpec((1, D), const),
                  resident((D, D_FF)), resident((D, D_FF)), resident((D_FF, D))],
        out_specs=pl.BlockSpec((TM, D), row),
        out_shape=jax.ShapeDtypeStruct((rows, D), F32),
        compiler_params=_cparams(("arbitrary",)),
        name="ffn_dense",
    )(hn, x1, mod, g_post.reshape(1, D), wg, wu, wd)


def _ffn_moe_body(hn_ref, x1_ref, gates_ref, gt_ref, gpost_ref, wg_ref, wu_ref, wd_ref, o_ref, acc_ref,
                  *, group0, tiles_per_group):
    e = pl.program_id(1)
    grp = _group_of(pl.program_id(0), group0, tiles_per_group)
    gates = gates_ref[...]
    lane = lax.broadcasted_iota(jnp.int32, gates.shape, 1)
    ge = jnp.sum(jnp.where(lane == e, gates, 0.0), axis=-1, keepdims=True)
    contrib = ge * _swiglu(hn_ref[...], wg_ref[...], wu_ref[...], wd_ref[...])

    @pl.when(e == 0)
    def _():
        acc_ref[...] = contrib

    @pl.when(e > 0)
    def _():
        acc_ref[...] += contrib

    @pl.when(e == N_EXPERTS - 1)
    def _():
        o_ref[...] = x1_ref[...] + gt_ref[pl.ds(grp, 1), :] * _rms(acc_ref[...], gpost_ref[...])


def _ffn_moe(hn, x1, gates, mod, layer, g_post, eg, eu, ed, group0, tiles_per_group):
    rows = x1.shape[0]
    row = lambda i, e: (i, 0)
    const = lambda i, e: (0, 0)
    wspec = lambda shape: pl.BlockSpec((None,) + shape, lambda i, e: (e, 0, 0))
    return pl.pallas_call(
        functools.partial(_ffn_moe_body, group0=group0, tiles_per_group=tiles_per_group),
        grid=(rows // TM, N_EXPERTS),
        in_specs=[pl.BlockSpec((TM, D), row), pl.BlockSpec((TM, D), row), pl.BlockSpec((TM, E_PAD), row),
                  pl.BlockSpec((None, N_GROUPS_PAD, D), lambda i, e: (layer, 0, 5)),
                  pl.BlockSpec((1, D), const),
                  wspec((D, D_FF_E)), wspec((D, D_FF_E)), wspec((D_FF_E, D))],
        out_specs=pl.BlockSpec((TM, D), row),
        out_shape=jax.ShapeDtypeStruct((rows, D), F32),
        scratch_shapes=[pltpu.VMEM((TM, D), F32)],
        compiler_params=_cparams(("arbitrary", "arbitrary")),
        name="ffn_moe",
    )(hn, x1, gates, mod, g_post.reshape(1, D), eg, eu, ed)


def kernel(x_prompt, x_sample, cache_k, cache_v, c, c_ctx, g_attn_pre, g_attn_post, g_ffn_pre, g_ffn_post,
           w_ada, b_ada, w_in, lam_params, g_subln, w_fnet, w_out, w_gate, w_up, w_down, w_router,
           e_gate, e_up, e_down):
    assert x_prompt.shape == (N_CTX_B, T_CTX, D) and x_sample.shape == (N_LAT_B, T_LAT, D)
    assert cache_k.shape == (N_LAT_B, DEPTH, T_PAST, N_HEADS, 2, QK_DIM)

    cond = jnp.zeros((N_GROUPS_PAD, D), F32).at[0].set(c_ctx).at[1:1 + N_LAT_B].set(c)
    mod = _ada(cond, w_ada, b_ada)
    wfold_bf = _wfold(w_fnet).astype(BF16)
    rope_tabs = _rope_tables()
    fft_consts = _fft_consts()

    w_in_bf = w_in.astype(BF16)
    w_out_bf = w_out.astype(BF16)
    w_router_pad = jnp.zeros((DEPTH // 2, D, E_PAD), F32).at[:, :, :N_EXPERTS].set(w_router)
    ck_bf = cache_k.reshape(N_LAT_B, DEPTH, T_PAST, ATTN_W).astype(BF16)
    cv_bf = cache_v.reshape(N_LAT_B, DEPTH, T_PAST, ATTN_W).astype(BF16)

    lat_tiles = T_LAT // TM
    streams = [dict(x=x_prompt.reshape(N_CTX_B * T_CTX, D), group0=0, tiles=None, lat=False),
               dict(x=x_sample.reshape(N_LAT_B * T_LAT, D), group0=1, tiles=lat_tiles, lat=True)]
    new_k, new_v = [], []
    for l in range(DEPTH):
        lam_init = 0.8 - 0.6 * math.exp(-0.3 * l)
        i = l // 2
        for s in streams:
            g0, tiles = s["group0"], s["tiles"]
            if s["lat"]:
                qkv, f = _inproj(s["x"], mod, l, g_attn_pre[l], w_in_bf[l], rope_tabs, g0, tiles)
                attn_o = _attn_lat(qkv, ck_bf[:, l], cv_bf[:, l], lam_params, g_subln, l, lam_init)
                attn_o = attn_o.reshape(N_LAT_B * T_LAT, ATTN_W)
                four_o = _four_lat(f, wfold_bf, l, fft_consts)
            else:
                qkv, f, k32, v32 = _inproj(s["x"], mod, l, g_attn_pre[l], w_in_bf[l], None, g0, tiles)
                new_k.append(k32)
                new_v.append(v32)
                attn_o = _attn_ctx(qkv, lam_params, g_subln, l, lam_init).reshape(N_CTX_B * T_CTX, ATTN_W)
                four_o = _four_ctx(f, wfold_bf, l).reshape(N_CTX_B * T_CTX, FOUR_W)
            if l % 2 == 0:
                x1, hn = _outproj(attn_o, four_o, s["x"], mod, l, g_attn_post[l], g_ffn_pre[l], w_out_bf[l],
                                  None, g0, tiles)
                s["x"] = _ffn_dense(hn, x1, mod, l, g_ffn_post[l], w_gate[i].astype(BF16), w_up[i].astype(BF16),
                                    w_down[i].astype(BF16), g0, tiles)
            else:
                x1, hn, gates = _outproj(attn_o, four_o, s["x"], mod, l, g_attn_post[l], g_ffn_pre[l],
                                         w_out_bf[l], w_router_pad[i], g0, tiles)
                s["x"] = _ffn_moe(hn, x1, gates, mod, l, g_ffn_post[l], e_gate[i].astype(BF16),
                                  e_up[i].astype(BF16), e_down[i].astype(BF16), g0, tiles)

    y_prompt = streams[0]["x"].reshape(N_CTX_B, T_CTX, D)
    y_sample = streams[1]["x"].reshape(N_LAT_B, T_LAT, D)
    new_cache_k = jnp.stack([k.reshape(N_CTX_B, T_CTX, N_HEADS, 2, QK_DIM) for k in new_k], axis=1)
    new_cache_v = jnp.stack([v.reshape(N_CTX_B, T_CTX, N_HEADS, HEAD_W) for v in new_v], axis=1)
    return (y_prompt, y_sample, new_cache_k, new_cache_v)
```

```python
import functools
import math

import numpy as np
import jax
import jax.numpy as jnp
from jax import lax
from jax.experimental import pallas as pl
from jax.experimental.pallas import tpu as pltpu

F32 = jnp.float32
BF16 = jnp.bfloat16

D = 1024
N_CTX_B = 16
T_CTX = 256
N_LAT_B = 2
T_LAT = 4096
T_PAST = 512
DEPTH = 2
GRID_W = 64
N_HEADS = 4
QK_DIM = 64
HEAD_W = 2 * QK_DIM
ATTN_W = N_HEADS * HEAD_W
F_GROUPS = 4
F_CH = 128
FOUR_W = F_GROUPS * F_CH
IN_W = 3 * ATTN_W + FOUR_W
QKV_W = 3 * ATTN_W
D_FF = 2816
N_EXPERTS = 8
D_FF_E = 1024
N_MOD = 6
EPS = 1e-6
ROPE_THETA = 10000.0
N_GROUPS_PAD = 8
E_PAD = 128

TM = 512
TQ = 512
TK = 1536
ATTN_UNROLL = 3
FFT_R = 64
FFT_S = 8
VMEM_LIMIT = 56 * 1024 * 1024


def _cparams(sem):
    return pltpu.CompilerParams(dimension_semantics=sem, vmem_limit_bytes=VMEM_LIMIT)


def _rms(x, g):
    ms = jnp.mean(x * x, axis=-1, keepdims=True)
    return x * lax.rsqrt(ms + EPS) * g


def _silu(x):
    return x * (1.0 / (1.0 + jnp.exp(-x)))


def _ada_body(c_ref, w_ref, b_ref, o_ref):
    s = _silu(c_ref[...])
    o_ref[...] = jnp.dot(s, w_ref[...], preferred_element_type=F32,
                         precision=lax.Precision.HIGHEST) + b_ref[...]


def _ada(cond, w_ada, b_ada):
    tn = 1536
    return pl.pallas_call(
        _ada_body,
        grid=(DEPTH, N_MOD * D // tn),
        in_specs=[pl.BlockSpec((N_GROUPS_PAD, D), lambda l, j: (0, 0)),
                  pl.BlockSpec((None, D, tn), lambda l, j: (l, 0, j)),
                  pl.BlockSpec((None, 1, tn), lambda l, j: (l, 0, j))],
        out_specs=pl.BlockSpec((None, N_GROUPS_PAD, tn), lambda l, j: (l, 0, j)),
        out_shape=jax.ShapeDtypeStruct((DEPTH, N_GROUPS_PAD, N_MOD * D), F32),
        compiler_params=_cparams(("arbitrary", "arbitrary")),
        name="ada_mod",
    )(cond, w_ada, b_ada.reshape(DEPTH, 1, N_MOD * D))


def _wfold_body(cc_ref, sc_ref, w_ref, o_ref):
    w = w_ref[...]
    o_ref[0:F_CH, :] = jnp.dot(cc_ref[...], w, preferred_element_type=F32,
                               precision=lax.Precision.HIGHEST)
    o_ref[F_CH:2 * F_CH, :] = jnp.dot(sc_ref[...], w, preferred_element_type=F32,
                                      precision=lax.Precision.HIGHEST)


def _wfold(w_fnet):
    ang = 2.0 * np.pi * np.outer(np.arange(F_CH), np.arange(F_CH)) / F_CH
    cc = jnp.asarray(np.cos(ang), F32)
    sc = jnp.asarray(np.sin(ang), F32)
    cspec = pl.BlockSpec((F_CH, F_CH), lambda l, g: (0, 0))
    return pl.pallas_call(
        _wfold_body,
        grid=(DEPTH, F_GROUPS),
        in_specs=[cspec, cspec, pl.BlockSpec((None, None, F_CH, F_CH), lambda l, g: (l, g, 0, 0))],
        out_specs=pl.BlockSpec((None, None, 2 * F_CH, F_CH), lambda l, g: (l, g, 0, 0)),
        out_shape=jax.ShapeDtypeStruct((DEPTH, F_GROUPS, 2 * F_CH, F_CH), F32),
        compiler_params=_cparams(("arbitrary", "arbitrary")),
        name="fnet_fold",
    )(cc, sc, w_fnet)


def _group_of(i, group0, tiles_per_group):
    if tiles_per_group is None:
        return group0
    return group0 + lax.div(i, jnp.int32(tiles_per_group))


def _inproj_body(x_ref, sh_ref, sc_ref, g_ref, w_ref, *rest, rope, group0, tiles_per_group):
    if rope:
        cos_ref, sin_ref, qkv_ref, f_ref = rest
    else:
        qkv_ref, f_ref, k32_ref, v32_ref = rest
    grp = _group_of(pl.program_id(0), group0, tiles_per_group)
    sh = sh_ref[pl.ds(grp, 1), :]
    sc = sc_ref[pl.ds(grp, 1), :]
    hn = _rms(x_ref[...], g_ref[...]) * (1.0 + sc) + sh
    proj = jnp.dot(hn.astype(BF16), w_ref[...], preferred_element_type=F32)
    qk = proj[:, :2 * ATTN_W]
    v = proj[:, 2 * ATTN_W:QKV_W]
    if rope:
        cos = jnp.tile(cos_ref[...], (1, 2 * ATTN_W // HEAD_W))
        sin = jnp.tile(sin_ref[...], (1, 2 * ATTN_W // HEAD_W))
        lane = lax.broadcasted_iota(jnp.int32, qk.shape, 1)
        low = (lane % 32) < 16
        rot = jnp.where(low, pltpu.roll(qk, 2 * ATTN_W - 16, 1), pltpu.roll(qk, 16, 1))
        qk = qk * cos + rot * sin
    else:
        k32_ref[...] = qk[:, ATTN_W:]
        v32_ref[...] = v
    qkv_ref[:, 0:ATTN_W] = (qk[:, :ATTN_W] * (QK_DIM ** -0.5 * math.log2(math.e))).astype(BF16)
    qkv_ref[:, ATTN_W:2 * ATTN_W] = qk[:, ATTN_W:].astype(BF16)
    qkv_ref[:, 2 * ATTN_W:] = v.astype(BF16)
    f_ref[...] = proj[:, QKV_W:]


def _inproj(x, mod, layer, g_pre, w_in_bf, rope_tabs, group0, tiles_per_group):
    rows = x.shape[0]
    rope = rope_tabs is not None
    row = lambda i: (i, 0)
    const = lambda i: (0, 0)
    in_specs = [pl.BlockSpec((TM, D), row),
                pl.BlockSpec((None, N_GROUPS_PAD, D), lambda i: (layer, 0, 0)),
                pl.BlockSpec((None, N_GROUPS_PAD, D), lambda i: (layer, 0, 1)),
                pl.BlockSpec((1, D), const),
                pl.BlockSpec((D, IN_W), const)]
    args = [x, mod, mod, g_pre.reshape(1, D), w_in_bf]
    out_specs = [pl.BlockSpec((TM, QKV_W), row), pl.BlockSpec((TM, FOUR_W), row)]
    out_shape = [jax.ShapeDtypeStruct((rows, QKV_W), BF16), jax.ShapeDtypeStruct((rows, FOUR_W), F32)]
    if rope:
        tiles_per_seq = T_LAT // TM
        tab = pl.BlockSpec((TM, HEAD_W), lambda i: (i % tiles_per_seq, 0))
        in_specs += [tab, tab]
        args += list(rope_tabs)
    else:
        out_specs += [pl.BlockSpec((TM, ATTN_W), row)] * 2
        out_shape += [jax.ShapeDtypeStruct((rows, ATTN_W), F32)] * 2
    return pl.pallas_call(
        functools.partial(_inproj_body, rope=rope, group0=group0, tiles_per_group=tiles_per_group),
        grid=(rows // TM,),
        in_specs=in_specs, out_specs=out_specs, out_shape=out_shape,
        compiler_params=_cparams(("arbitrary",)),
        name="inproj_lat" if rope else "inproj_ctx",
    )(*args)


def _rope_tables():
    half = QK_DIM // 2
    inv = 1.0 / (ROPE_THETA ** (np.arange(0, half, 2, dtype=np.float64) / half))
    pos = np.arange(T_LAT)
    def tab(p):
        ang = p[:, None].astype(np.float64) * inv[None, :]
        return np.concatenate([ang, ang], axis=-1)
    ang = np.concatenate([tab(pos // GRID_W), tab(pos % GRID_W)], axis=-1)
    sign = np.where((np.arange(QK_DIM) % 32) < 16, -1.0, 1.0)
    cos = np.tile(np.cos(ang), (1, 2))
    sin = np.tile(np.sin(ang) * sign[None, :], (1, 2))
    return jnp.asarray(cos, F32), jnp.asarray(sin, F32)


def _lam(lam_ref, lam_init):
    lp = lam_ref[...]
    return (jnp.exp(jnp.sum(lp[0:1] * lp[1:2], keepdims=True))
            - jnp.exp(jnp.sum(lp[2:3] * lp[3:4], keepdims=True)) + lam_init)


def _stack_maps(q):
    lane = lax.broadcasted_iota(jnp.int32, q.shape, 1)
    zero = jnp.zeros_like(q)
    return jnp.concatenate([jnp.where(lane < QK_DIM, q, zero), jnp.where(lane >= QK_DIM, q, zero)], axis=0)


def _softmax_step(qq, kb, vb, carry):
    m, l, acc = carry
    s = lax.dot_general(qq, kb, (((1,), (1,)), ((), ())), preferred_element_type=F32)
    m_new = jnp.maximum(m, jnp.max(s, axis=-1, keepdims=True))
    alpha = jnp.exp2(m - m_new)
    p = jnp.exp2(s - m_new)
    l = alpha * l + jnp.sum(p, axis=-1, keepdims=True)
    acc = alpha * acc + jnp.dot(p.astype(BF16), vb, preferred_element_type=F32)
    return m_new, l, acc


def _softmax_init(rows):
    return (jnp.full((rows, 1), -jnp.inf, F32), jnp.zeros((rows, 1), F32), jnp.zeros((rows, HEAD_W), F32))


def _diff_out(carry, tq, lam, gs, lam_init):
    _, l, acc = carry
    o = acc / l
    a = o[:tq] - lam * o[tq:]
    return _rms(a, gs) * (1.0 - lam_init)


def _attn_ctx_body(lam_ref, gs_ref, q_ref, k_ref, v_ref, o_ref, *, lam_init):
    lam = _lam(lam_ref, lam_init)
    for h in range(N_HEADS):
        sl = slice(h * HEAD_W, (h + 1) * HEAD_W)
        carry = _softmax_step(_stack_maps(q_ref[:, sl]), k_ref[:, sl], v_ref[:, sl], _softmax_init(2 * T_CTX))
        o_ref[:, sl] = _diff_out(carry, T_CTX, lam, gs_ref[...], lam_init).astype(o_ref.dtype)


def _attn_ctx(qkv, lam_params, g_subln, layer, lam_init):
    qkv3 = qkv.reshape(N_CTX_B, T_CTX, QKV_W)
    blk = lambda part: pl.BlockSpec((None, T_CTX, ATTN_W), lambda b: (b, 0, part))
    return pl.pallas_call(
        functools.partial(_attn_ctx_body, lam_init=lam_init),
        grid=(N_CTX_B,),
        in_specs=[pl.BlockSpec((None, 4, QK_DIM), lambda b: (layer, 0, 0)),
                  pl.BlockSpec((None, 1, HEAD_W), lambda b: (layer, 0, 0)),
                  blk(0), blk(1), blk(2)],
        out_specs=pl.BlockSpec((None, T_CTX, ATTN_W), lambda b: (b, 0, 0)),
        out_shape=jax.ShapeDtypeStruct((N_CTX_B, T_CTX, ATTN_W), BF16),
        compiler_params=_cparams(("arbitrary",)),
        name="attn_ctx",
    )(lam_params, g_subln.reshape(DEPTH, 1, HEAD_W), qkv3, qkv3, qkv3)


def _attn_lat_body(lam_ref, gs_ref, q_ref, kc_ref, vc_ref, kl_ref, vl_ref, o_ref, kcat, vcat, *, lam_init):
    @pl.when(pl.program_id(2) == 0)
    def _():
        kcat[0:T_PAST, :] = kc_ref[...]
        kcat[T_PAST:, :] = kl_ref[...]
        vcat[0:T_PAST, :] = vc_ref[...]
        vcat[T_PAST:, :] = vl_ref[...]

    qq = _stack_maps(q_ref[...])

    def body(c, carry):
        start = pl.multiple_of(c * TK, TK)
        return _softmax_step(qq, kcat[pl.ds(start, TK), :], vcat[pl.ds(start, TK), :], carry)

    carry = lax.fori_loop(0, (T_PAST + T_LAT) // TK, body, _softmax_init(2 * TQ), unroll=ATTN_UNROLL)
    o_ref[...] = _diff_out(carry, TQ, _lam(lam_ref, lam_init), gs_ref[...], lam_init).astype(o_ref.dtype)


def _attn_lat(qkv, ck, cv, lam_params, g_subln, layer, lam_init):
    qkv3 = qkv.reshape(N_LAT_B, T_LAT, QKV_W)
    full = lambda off: pl.BlockSpec((None, T_LAT, HEAD_W), lambda b, h, i: (b, 0, off + h))
    past = pl.BlockSpec((None, T_PAST, HEAD_W), lambda b, h, i: (b, 0, h))
    return pl.pallas_call(
        functools.partial(_attn_lat_body, lam_init=lam_init),
        grid=(N_LAT_B, N_HEADS, T_LAT // TQ),
        in_specs=[pl.BlockSpec((None, 4, QK_DIM), lambda b, h, i: (layer, 0, 0)),
                  pl.BlockSpec((None, 1, HEAD_W), lambda b, h, i: (layer, 0, 0)),
                  pl.BlockSpec((None, TQ, HEAD_W), lambda b, h, i: (b, i, h)),
                  past, past, full(N_HEADS), full(2 * N_HEADS)],
        out_specs=pl.BlockSpec((None, TQ, HEAD_W), lambda b, h, i: (b, i, h)),
        out_shape=jax.ShapeDtypeStruct((N_LAT_B, T_LAT, ATTN_W), BF16),
        scratch_shapes=[pltpu.VMEM((T_PAST + T_LAT, HEAD_W), BF16)] * 2,
        compiler_params=_cparams(("arbitrary", "arbitrary", "arbitrary")),
        name="attn_lat",
    )(lam_params, g_subln.reshape(DEPTH, 1, HEAD_W), qkv3, ck, cv, qkv3, qkv3)


def _fold_groups(ur, ui, wf_ref, scale):
    outs = []
    for g in range(F_GROUPS):
        sl = slice(g * F_CH, (g + 1) * F_CH)
        lhs = jnp.concatenate([ur[:, sl], ui[:, sl]], axis=1).astype(BF16)
        outs.append(jnp.dot(lhs, wf_ref[g], preferred_element_type=F32))
    return jnp.concatenate(outs, axis=1) * scale


def _four_ctx_body(f_ref, dft_ref, wf_ref, o_ref):
    u = jnp.dot(dft_ref[...], f_ref[...].astype(BF16), preferred_element_type=F32)
    o_ref[...] = _fold_groups(u[:T_CTX], u[T_CTX:], wf_ref, 1.0 / math.sqrt(T_CTX * F_CH)).astype(o_ref.dtype)


def _four_ctx(f, wfold_bf, layer):
    ang = 2.0 * np.pi * np.outer(np.arange(T_CTX), np.arange(T_CTX)) / T_CTX
    dft = jnp.asarray(np.concatenate([np.cos(ang), -np.sin(ang)], axis=0), F32).astype(BF16)
    return pl.pallas_call(
        _four_ctx_body,
        grid=(N_CTX_B,),
        in_specs=[pl.BlockSpec((None, T_CTX, FOUR_W), lambda b: (b, 0, 0)),
                  pl.BlockSpec((2 * T_CTX, T_CTX), lambda b: (0, 0)),
                  pl.BlockSpec((None, F_GROUPS, 2 * F_CH, F_CH), lambda b: (layer, 0, 0, 0))],
        out_specs=pl.BlockSpec((None, T_CTX, FOUR_W), lambda b: (b, 0, 0)),
        out_shape=jax.ShapeDtypeStruct((N_CTX_B, T_CTX, FOUR_W), BF16),
        compiler_params=_cparams(("arbitrary",)),
        name="fourier_ctx",
    )(f.reshape(N_CTX_B, T_CTX, FOUR_W), dft, wfold_bf)


def _fft_body(x_ref, ma_ref, twc_ref, tws_ref, mc_ref, ms_ref, wf_ref, o_ref, h_ref):
    nb = FFT_R * FFT_S
    nj = FFT_R // FFT_S
    j = pl.program_id(1)

    @pl.when(j < nj)
    def _():
        x = x_ref[...].reshape(nb, FOUR_W).astype(BF16)
        g = jnp.dot(ma_ref[...], x, preferred_element_type=F32)
        gr, gi = g[:nb], g[nb:]
        twc = jnp.tile(twc_ref[...], (1, FOUR_W // 128))
        tws = jnp.tile(tws_ref[...], (1, FOUR_W // 128))
        lo = pl.ds(pl.multiple_of(j * FFT_S, FFT_S), FFT_S)
        h_ref[lo, :, 0:FOUR_W] = (gr * twc + gi * tws).reshape(FFT_S, FFT_R, FOUR_W)
        h_ref[lo, :, FOUR_W:] = (gi * twc - gr * tws).reshape(FFT_S, FFT_R, FOUR_W)

    @pl.when(j >= nj)
    def _():
        hi = pl.ds(pl.multiple_of((j - nj) * FFT_S, FFT_S), FFT_S)
        h = h_ref[:, hi, :].reshape(nb, 2 * FOUR_W).astype(BF16)
        p = jnp.dot(mc_ref[...], h, preferred_element_type=F32)
        q = jnp.dot(ms_ref[...], h, preferred_element_type=F32)
        ur = p[:, :FOUR_W] + q[:, FOUR_W:]
        ui = p[:, FOUR_W:] - q[:, :FOUR_W]
        out = _fold_groups(ur, ui, wf_ref, 1.0 / math.sqrt(T_LAT * F_CH))
        o_ref[...] = out.reshape(FFT_R, FFT_S, FOUR_W)


def _fft_consts():
    r, s = FFT_R, FFT_S
    nb = r * s
    ang = 2.0 * np.pi * np.outer(np.arange(r), np.arange(r)) / r
    c, sn = np.cos(ang), np.sin(ang)
    eye = np.eye(s)
    ma = np.concatenate([np.einsum('pb,ts->tpbs', c, eye).reshape(nb, nb),
                         np.einsum('pb,ts->tpbs', -sn, eye).reshape(nb, nb)], axis=0)
    mbc = np.einsum('pa,ts->ptas', c, eye).reshape(nb, nb)
    mbs = np.einsum('pa,ts->ptas', sn, eye).reshape(nb, nb)
    tw = 2.0 * np.pi * np.outer(np.arange(r), np.arange(r)).reshape(-1) / (r * r)
    twc = np.broadcast_to(np.cos(tw)[:, None], (r * r, 128))
    tws = np.broadcast_to(np.sin(tw)[:, None], (r * r, 128))
    bf = lambda a: jnp.asarray(a, F32).astype(BF16)
    return bf(ma), bf(mbc), bf(mbs), jnp.asarray(twc, F32), jnp.asarray(tws, F32)


def _four_lat(f, wfold_bf, layer, consts):
    ma, mbc, mbs, twc, tws = consts
    r, s = FFT_R, FFT_S
    nb = r * s
    nj = r // s
    f4 = f.reshape(N_LAT_B, r, r, FOUR_W)
    a_step = lambda j: jnp.minimum(j, nj - 1)
    b_step = lambda j: jnp.maximum(j - nj, 0)
    const = lambda b, j: (0, 0)
    out = pl.pallas_call(
        _fft_body,
        grid=(N_LAT_B, 2 * nj),
        in_specs=[pl.BlockSpec((None, r, s, FOUR_W), lambda b, j: (b, 0, a_step(j), 0)),
                  pl.BlockSpec((2 * nb, nb), const),
                  pl.BlockSpec((nb, 128), lambda b, j: (a_step(j), 0)),
                  pl.BlockSpec((nb, 128), lambda b, j: (a_step(j), 0)),
                  pl.BlockSpec((nb, nb), const),
                  pl.BlockSpec((nb, nb), const),
                  pl.BlockSpec((None, F_GROUPS, 2 * F_CH, F_CH), lambda b, j: (layer, 0, 0, 0))],
        out_specs=pl.BlockSpec((None, r, s, FOUR_W), lambda b, j: (b, 0, b_step(j), 0)),
        out_shape=jax.ShapeDtypeStruct((N_LAT_B, r, r, FOUR_W), F32),
        scratch_shapes=[pltpu.VMEM((r, r, 2 * FOUR_W), F32)],
        compiler_params=_cparams(("arbitrary", "arbitrary")),
        name="fft_lat",
    )(f4, ma, twc, tws, mbc, mbs, wfold_bf)
    return out.reshape(N_LAT_B * T_LAT, FOUR_W)


def _outproj_body(a_ref, f_ref, x_ref, gt_ref, sh_ref, sc_ref, gpost_ref, gpre_ref, wo_ref, *rest,
                  moe, group0, tiles_per_group):
    if moe:
        wr_ref, x1_ref, hn_ref, gates_ref = rest
    else:
        x1_ref, hn_ref = rest
    grp = _group_of(pl.program_id(0), group0, tiles_per_group)
    mixed = (jnp.dot(a_ref[...], wo_ref[0:ATTN_W, :], preferred_element_type=F32)
             + jnp.dot(f_ref[...].astype(BF16), wo_ref[ATTN_W:, :], preferred_element_type=F32))
    x1 = x_ref[...] + gt_ref[pl.ds(grp, 1), :] * _rms(mixed, gpost_ref[...])
    x1_ref[...] = x1
    hn = _rms(x1, gpre_ref[...]) * (1.0 + sc_ref[pl.ds(grp, 1), :]) + sh_ref[pl.ds(grp, 1), :]
    hn_ref[...] = hn.astype(BF16)
    if moe:
        wr = wr_ref[...]
        lane = lax.broadcasted_iota(jnp.int32, (hn.shape[0], E_PAD), 1).astype(F32)
        neg = jnp.float32(-jnp.inf)
        logits = jnp.full((hn.shape[0], E_PAD), neg, F32)
        for e in range(N_EXPERTS):
            logits = jnp.where(lane == float(e), jnp.sum(hn * wr[e:e + 1, :], axis=-1, keepdims=True), logits)
        m1 = jnp.max(logits, axis=-1, keepdims=True)
        i1 = jnp.min(jnp.where(logits == m1, lane, float(E_PAD)), axis=-1, keepdims=True)
        rest_l = jnp.where(lane == i1, neg, logits)
        m2 = jnp.max(rest_l, axis=-1, keepdims=True)
        i2 = jnp.min(jnp.where(rest_l == m2, lane, float(E_PAD)), axis=-1, keepdims=True)
        e2 = jnp.exp(m2 - m1)
        w1 = 1.0 / (1.0 + e2)
        w2 = e2 / (1.0 + e2)
        gates_ref[...] = jnp.where(lane == i1, w1, 0.0) + jnp.where(lane == i2, w2, 0.0)


def _outproj(attn_o, four_o, x, mod, layer, g_post, g_ffn_pre, w_out_bf, w_router_t, group0, tiles_per_group):
    rows = x.shape[0]
    moe = w_router_t is not None
    row = lambda i: (i, 0)
    const = lambda i: (0, 0)
    modspec = lambda k: pl.BlockSpec((None, N_GROUPS_PAD, D), lambda i: (layer, 0, k))
    in_specs = [pl.BlockSpec((TM, ATTN_W), row), pl.BlockSpec((TM, FOUR_W), row), pl.BlockSpec((TM, D), row),
                modspec(2), modspec(3), modspec(4),
                pl.BlockSpec((1, D), const), pl.BlockSpec((1, D), const), pl.BlockSpec((D, D), const)]
    args = [attn_o, four_o, x, mod, mod, mod, g_post.reshape(1, D), g_ffn_pre.reshape(1, D), w_out_bf]
    out_specs = [pl.BlockSpec((TM, D), row), pl.BlockSpec((TM, D), row)]
    out_shape = [jax.ShapeDtypeStruct((rows, D), F32), jax.ShapeDtypeStruct((rows, D), BF16)]
    if moe:
        in_specs.append(pl.BlockSpec((N_EXPERTS, D), const))
        args.append(w_router_t)
        out_specs.append(pl.BlockSpec((TM, E_PAD), row))
        out_shape.append(jax.ShapeDtypeStruct((rows, E_PAD), F32))
    return pl.pallas_call(
        functools.partial(_outproj_body, moe=moe, group0=group0, tiles_per_group=tiles_per_group),
        grid=(rows // TM,),
        in_specs=in_specs, out_specs=out_specs, out_shape=out_shape,
        compiler_params=_cparams(("arbitrary",)),
        name="outproj_moe" if moe else "outproj",
    )(*args)


def _swiglu(hn, wg, wu, wd):
    g = jnp.dot(hn, wg, preferred_element_type=F32)
    u = jnp.dot(hn, wu, preferred_element_type=F32)
    return jnp.dot((_silu(g) * u).astype(BF16), wd, preferred_element_type=F32)


def _ffn_dense_body(hn_ref, x1_ref, gt_ref, gpost_ref, wg_ref, wu_ref, wd_ref, o_ref, *, group0, tiles_per_group):
    grp = _group_of(pl.program_id(0), group0, tiles_per_group)
    ff = _swiglu(hn_ref[...], wg_ref[...], wu_ref[...], wd_ref[...])
    o_ref[...] = x1_ref[...] + gt_ref[pl.ds(grp, 1), :] * _rms(ff, gpost_ref[...])


def _ffn_dense(hn, x1, mod, layer, g_post, wg, wu, wd, group0, tiles_per_group):
    rows = x1.shape[0]
    row = lambda i: (i, 0)
    const = lambda i: (0, 0)
    resident = lambda shape: pl.BlockSpec(shape, const, pipeline_mode=pl.Buffered(1))
    return pl.pallas_call(
        functools.partial(_ffn_dense_body, group0=group0, tiles_per_group=tiles_per_group),
        grid=(rows // TM,),
        in_specs=[pl.BlockSpec((TM, D), row), pl.BlockSpec((TM, D), row),
                  pl.BlockSpec((None, N_GROUPS_PAD, D), lambda i: (layer, 0, 5)),
                  pl.BlockSpec((1, D), const),
                  resident((D, D_FF)), resident((D, D_FF)), resident((D_FF, D))],
        out_specs=pl.BlockSpec((TM, D), row),
        out_shape=jax.ShapeDtypeStruct((rows, D), F32),
        compiler_params=_cparams(("arbitrary",)),
        name="ffn_dense",
    )(hn, x1, mod, g_post.reshape(1, D), wg, wu, wd)


def _ffn_moe_body(hn_ref, x1_ref, gates_ref, gt_ref, gpost_ref, wg_ref, wu_ref, wd_ref, o_ref, acc_ref,
                  *, group0, tiles_per_group):
    e = pl.program_id(1)
    grp = _group_of(pl.program_id(0), group0, tiles_per_group)
    gates = gates_ref[...]
    lane = lax.broadcasted_iota(jnp.int32, gates.shape, 1)
    ge = jnp.sum(jnp.where(lane == e, gates, 0.0), axis=-1, keepdims=True)
    contrib = ge * _swiglu(hn_ref[...], wg_ref[...], wu_ref[...], wd_ref[...])

    @pl.when(e == 0)
    def _():
        acc_ref[...] = contrib

    @pl.when(e > 0)
    def _():
        acc_ref[...] += contrib

    @pl.when(e == N_EXPERTS - 1)
    def _():
        o_ref[...] = x1_ref[...] + gt_ref[pl.ds(grp, 1), :] * _rms(acc_ref[...], gpost_ref[...])


def _ffn_moe(hn, x1, gates, mod, layer, g_post, eg, eu, ed, group0, tiles_per_group):
    rows = x1.shape[0]
    row = lambda i, e: (i, 0)
    const = lambda i, e: (0, 0)
    wspec = lambda shape: pl.BlockSpec((None,) + shape, lambda i, e: (e, 0, 0))
    return pl.pallas_call(
        functools.partial(_ffn_moe_body, group0=group0, tiles_per_group=tiles_per_group),
        grid=(rows // TM, N_EXPERTS),
        in_specs=[pl.BlockSpec((TM, D), row), pl.BlockSpec((TM, D), row), pl.BlockSpec((TM, E_PAD), row),
                  pl.BlockSpec((None, N_GROUPS_PAD, D), lambda i, e: (layer, 0, 5)),
                  pl.BlockSpec((1, D), const),
                  wspec((D, D_FF_E)), wspec((D, D_FF_E)), wspec((D_FF_E, D))],
        out_specs=pl.BlockSpec((TM, D), row),
        out_shape=jax.ShapeDtypeStruct((rows, D), F32),
        scratch_shapes=[pltpu.VMEM((TM, D), F32)],
        compiler_params=_cparams(("arbitrary", "arbitrary")),
        name="ffn_moe",
    )(hn, x1, gates, mod, g_post.reshape(1, D), eg, eu, ed)


def kernel(x_prompt, x_sample, cache_k, cache_v, c, c_ctx, g_attn_pre, g_attn_post, g_ffn_pre, g_ffn_post,
           w_ada, b_ada, w_in, lam_params, g_subln, w_fnet, w_out, w_gate, w_up, w_down, w_router,
           e_gate, e_up, e_down):
    assert x_prompt.shape == (N_CTX_B, T_CTX, D) and x_sample.shape == (N_LAT_B, T_LAT, D)
    assert cache_k.shape == (N_LAT_B, DEPTH, T_PAST, N_HEADS, 2, QK_DIM)

    cond = jnp.zeros((N_GROUPS_PAD, D), F32).at[0].set(c_ctx).at[1:1 + N_LAT_B].set(c)
    mod = _ada(cond, w_ada, b_ada)
    wfold_bf = _wfold(w_fnet).astype(BF16)
    rope_tabs = _rope_tables()
    fft_consts = _fft_consts()

    w_in_bf = w_in.astype(BF16)
    w_out_bf = w_out.astype(BF16)
    w_router_t = jnp.swapaxes(w_router, 1, 2)
    ck_bf = cache_k.reshape(N_LAT_B, DEPTH, T_PAST, ATTN_W).astype(BF16)
    cv_bf = cache_v.reshape(N_LAT_B, DEPTH, T_PAST, ATTN_W).astype(BF16)

    lat_tiles = T_LAT // TM
    streams = [dict(x=x_prompt.reshape(N_CTX_B * T_CTX, D), group0=0, tiles=None, lat=False),
               dict(x=x_sample.reshape(N_LAT_B * T_LAT, D), group0=1, tiles=lat_tiles, lat=True)]
    new_k, new_v = [], []
    for l in range(DEPTH):
        lam_init = 0.8 - 0.6 * math.exp(-0.3 * l)
        i = l // 2
        for s in streams:
            g0, tiles = s["group0"], s["tiles"]
            if s["lat"]:
                qkv, f = _inproj(s["x"], mod, l, g_attn_pre[l], w_in_bf[l], rope_tabs, g0, tiles)
                attn_o = _attn_lat(qkv, ck_bf[:, l], cv_bf[:, l], lam_params, g_subln, l, lam_init)
                attn_o = attn_o.reshape(N_LAT_B * T_LAT, ATTN_W)
                four_o = _four_lat(f, wfold_bf, l, fft_consts)
            else:
                qkv, f, k32, v32 = _inproj(s["x"], mod, l, g_attn_pre[l], w_in_bf[l], None, g0, tiles)
                new_k.append(k32)
                new_v.append(v32)
                attn_o = _attn_ctx(qkv, lam_params, g_subln, l, lam_init).reshape(N_CTX_B * T_CTX, ATTN_W)
                four_o = _four_ctx(f, wfold_bf, l).reshape(N_CTX_B * T_CTX, FOUR_W)
            if l % 2 == 0:
                x1, hn = _outproj(attn_o, four_o, s["x"], mod, l, g_attn_post[l], g_ffn_pre[l], w_out_bf[l],
                                  None, g0, tiles)
                s["x"] = _ffn_dense(hn, x1, mod, l, g_ffn_post[l], w_gate[i].astype(BF16), w_up[i].astype(BF16),
                                    w_down[i].astype(BF16), g0, tiles)
            else:
                x1, hn, gates = _outproj(attn_o, four_o, s["x"], mod, l, g_attn_post[l], g_ffn_pre[l],
                                         w_out_bf[l], w_router_t[i], g0, tiles)
                s["x"] = _ffn_moe(hn, x1, gates, mod, l, g_ffn_post[l], e_gate[i].astype(BF16),
                                  e_up[i].astype(BF16), e_down[i].astype(BF16), g0, tiles)

    y_prompt = streams[0]["x"].reshape(N_CTX_B, T_CTX, D)
    y_sample = streams[1]["x"].reshape(N_LAT_B, T_LAT, D)
    new_cache_k = jnp.stack([k.reshape(N_CTX_B, T_CTX, N_HEADS, 2, QK_DIM) for k in new_k], axis=1)
    new_cache_v = jnp.stack([v.reshape(N_CTX_B, T_CTX, N_HEADS, HEAD_W) for v in new_v], axis=1)
    return (y_prompt, y_sample, new_cache_k, new_cache_v)
```

```python
import functools
import math

import numpy as np
import jax
import jax.numpy as jnp
from jax import lax
from jax.experimental import pallas as pl
from jax.experimental.pallas import tpu as pltpu

F32 = jnp.float32
BF16 = jnp.bfloat16

D = 1024
N_CTX_B = 16
T_CTX = 256
N_LAT_B = 2
T_LAT = 4096
T_PAST = 512
DEPTH = 2
GRID_W = 64
N_HEADS = 4
QK_DIM = 64
HEAD_W = 2 * QK_DIM
ATTN_W = N_HEADS * HEAD_W
F_GROUPS = 4
F_CH = 128
FOUR_W = F_GROUPS * F_CH
IN_W = 3 * ATTN_W + FOUR_W
QKV_W = 3 * ATTN_W
D_FF = 2816
N_EXPERTS = 8
D_FF_E = 1024
N_MOD = 6
EPS = 1e-6
ROPE_THETA = 10000.0
N_GROUPS_PAD = 8
E_PAD = 128

TM = 512
TOP_K = 2
TMOE = 512
N_TOKENS = N_CTX_B * T_CTX + N_LAT_B * T_LAT
NP = TOP_K * N_TOKENS + N_EXPERTS * TMOE
N_TILES = NP // TMOE
TQ = 512
TK = 1536
ATTN_UNROLL = 3
FFT_R = 64
FFT_S = 8
VMEM_LIMIT = 56 * 1024 * 1024


def _cparams(sem):
    return pltpu.CompilerParams(dimension_semantics=sem, vmem_limit_bytes=VMEM_LIMIT)


def _rms(x, g):
    ms = jnp.mean(x * x, axis=-1, keepdims=True)
    return x * lax.rsqrt(ms + EPS) * g


def _silu(x):
    return x * (1.0 / (1.0 + jnp.exp(-x)))


def _ada_body(c_ref, w_ref, b_ref, o_ref):
    s = _silu(c_ref[...])
    o_ref[...] = jnp.dot(s, w_ref[...], preferred_element_type=F32,
                         precision=lax.Precision.HIGHEST) + b_ref[...]


def _ada(cond, w_ada, b_ada):
    tn = 1536
    return pl.pallas_call(
        _ada_body,
        grid=(DEPTH, N_MOD * D // tn),
        in_specs=[pl.BlockSpec((N_GROUPS_PAD, D), lambda l, j: (0, 0)),
                  pl.BlockSpec((None, D, tn), lambda l, j: (l, 0, j)),
                  pl.BlockSpec((None, 1, tn), lambda l, j: (l, 0, j))],
        out_specs=pl.BlockSpec((None, N_GROUPS_PAD, tn), lambda l, j: (l, 0, j)),
        out_shape=jax.ShapeDtypeStruct((DEPTH, N_GROUPS_PAD, N_MOD * D), F32),
        compiler_params=_cparams(("arbitrary", "arbitrary")),
        name="ada_mod",
    )(cond, w_ada, b_ada.reshape(DEPTH, 1, N_MOD * D))


def _wfold_body(cc_ref, sc_ref, w_ref, o_ref):
    w = w_ref[...]
    o_ref[0:F_CH, :] = jnp.dot(cc_ref[...], w, preferred_element_type=F32,
                               precision=lax.Precision.HIGHEST)
    o_ref[F_CH:2 * F_CH, :] = jnp.dot(sc_ref[...], w, preferred_element_type=F32,
                                      precision=lax.Precision.HIGHEST)


def _wfold(w_fnet):
    ang = 2.0 * np.pi * np.outer(np.arange(F_CH), np.arange(F_CH)) / F_CH
    cc = jnp.asarray(np.cos(ang), F32)
    sc = jnp.asarray(np.sin(ang), F32)
    cspec = pl.BlockSpec((F_CH, F_CH), lambda l, g: (0, 0))
    return pl.pallas_call(
        _wfold_body,
        grid=(DEPTH, F_GROUPS),
        in_specs=[cspec, cspec, pl.BlockSpec((None, None, F_CH, F_CH), lambda l, g: (l, g, 0, 0))],
        out_specs=pl.BlockSpec((None, None, 2 * F_CH, F_CH), lambda l, g: (l, g, 0, 0)),
        out_shape=jax.ShapeDtypeStruct((DEPTH, F_GROUPS, 2 * F_CH, F_CH), F32),
        compiler_params=_cparams(("arbitrary", "arbitrary")),
        name="fnet_fold",
    )(cc, sc, w_fnet)


def _group_of(i, group0, tiles_per_group):
    if tiles_per_group is None:
        return group0
    return group0 + lax.div(i, jnp.int32(tiles_per_group))


def _inproj_body(x_ref, sh_ref, sc_ref, g_ref, w_ref, *rest, rope, group0, tiles_per_group):
    if rope:
        cos_ref, sin_ref, qkv_ref, f_ref = rest
    else:
        qkv_ref, f_ref, k32_ref, v32_ref = rest
    grp = _group_of(pl.program_id(0), group0, tiles_per_group)
    sh = sh_ref[pl.ds(grp, 1), :]
    sc = sc_ref[pl.ds(grp, 1), :]
    hn = _rms(x_ref[...], g_ref[...]) * (1.0 + sc) + sh
    proj = jnp.dot(hn.astype(BF16), w_ref[...], preferred_element_type=F32)
    qk = proj[:, :2 * ATTN_W]
    v = proj[:, 2 * ATTN_W:QKV_W]
    if rope:
        cos = jnp.tile(cos_ref[...], (1, 2 * ATTN_W // HEAD_W))
        sin = jnp.tile(sin_ref[...], (1, 2 * ATTN_W // HEAD_W))
        lane = lax.broadcasted_iota(jnp.int32, qk.shape, 1)
        low = (lane % 32) < 16
        rot = jnp.where(low, pltpu.roll(qk, 2 * ATTN_W - 16, 1), pltpu.roll(qk, 16, 1))
        qk = qk * cos + rot * sin
    else:
        k32_ref[...] = qk[:, ATTN_W:]
        v32_ref[...] = v
    qkv_ref[:, 0:ATTN_W] = (qk[:, :ATTN_W] * (QK_DIM ** -0.5 * math.log2(math.e))).astype(BF16)
    qkv_ref[:, ATTN_W:2 * ATTN_W] = qk[:, ATTN_W:].astype(BF16)
    qkv_ref[:, 2 * ATTN_W:] = v.astype(BF16)
    f_ref[...] = proj[:, QKV_W:]


def _inproj(x, mod, layer, g_pre, w_in_bf, rope_tabs, group0, tiles_per_group):
    rows = x.shape[0]
    rope = rope_tabs is not None
    row = lambda i: (i, 0)
    const = lambda i: (0, 0)
    in_specs = [pl.BlockSpec((TM, D), row),
                pl.BlockSpec((None, N_GROUPS_PAD, D), lambda i: (layer, 0, 0)),
                pl.BlockSpec((None, N_GROUPS_PAD, D), lambda i: (layer, 0, 1)),
                pl.BlockSpec((1, D), const),
                pl.BlockSpec((D, IN_W), const)]
    args = [x, mod, mod, g_pre.reshape(1, D), w_in_bf]
    out_specs = [pl.BlockSpec((TM, QKV_W), row), pl.BlockSpec((TM, FOUR_W), row)]
    out_shape = [jax.ShapeDtypeStruct((rows, QKV_W), BF16), jax.ShapeDtypeStruct((rows, FOUR_W), F32)]
    if rope:
        tiles_per_seq = T_LAT // TM
        tab = pl.BlockSpec((TM, HEAD_W), lambda i: (i % tiles_per_seq, 0))
        in_specs += [tab, tab]
        args += list(rope_tabs)
    else:
        out_specs += [pl.BlockSpec((TM, ATTN_W), row)] * 2
        out_shape += [jax.ShapeDtypeStruct((rows, ATTN_W), F32)] * 2
    return pl.pallas_call(
        functools.partial(_inproj_body, rope=rope, group0=group0, tiles_per_group=tiles_per_group),
        grid=(rows // TM,),
        in_specs=in_specs, out_specs=out_specs, out_shape=out_shape,
        compiler_params=_cparams(("arbitrary",)),
        name="inproj_lat" if rope else "inproj_ctx",
    )(*args)


def _rope_tables():
    half = QK_DIM // 2
    inv = 1.0 / (ROPE_THETA ** (np.arange(0, half, 2, dtype=np.float64) / half))
    pos = np.arange(T_LAT)
    def tab(p):
        ang = p[:, None].astype(np.float64) * inv[None, :]
        return np.concatenate([ang, ang], axis=-1)
    ang = np.concatenate([tab(pos // GRID_W), tab(pos % GRID_W)], axis=-1)
    sign = np.where((np.arange(QK_DIM) % 32) < 16, -1.0, 1.0)
    cos = np.tile(np.cos(ang), (1, 2))
    sin = np.tile(np.sin(ang) * sign[None, :], (1, 2))
    return jnp.asarray(cos, F32), jnp.asarray(sin, F32)


def _lam(lam_ref, lam_init):
    lp = lam_ref[...]
    return (jnp.exp(jnp.sum(lp[0:1] * lp[1:2], keepdims=True))
            - jnp.exp(jnp.sum(lp[2:3] * lp[3:4], keepdims=True)) + lam_init)


def _stack_maps(q):
    lane = lax.broadcasted_iota(jnp.int32, q.shape, 1)
    zero = jnp.zeros_like(q)
    return jnp.concatenate([jnp.where(lane < QK_DIM, q, zero), jnp.where(lane >= QK_DIM, q, zero)], axis=0)


def _softmax_step(qq, kb, vb, carry):
    m, l, acc = carry
    s = lax.dot_general(qq, kb, (((1,), (1,)), ((), ())), preferred_element_type=F32)
    m_new = jnp.maximum(m, jnp.max(s, axis=-1, keepdims=True))
    alpha = jnp.exp2(m - m_new)
    p = jnp.exp2(s - m_new)
    l = alpha * l + jnp.sum(p, axis=-1, keepdims=True)
    acc = alpha * acc + jnp.dot(p.astype(BF16), vb, preferred_element_type=F32)
    return m_new, l, acc


def _softmax_init(rows):
    return (jnp.full((rows, 1), -jnp.inf, F32), jnp.zeros((rows, 1), F32), jnp.zeros((rows, HEAD_W), F32))


def _diff_out(carry, tq, lam, gs, lam_init):
    _, l, acc = carry
    o = acc / l
    a = o[:tq] - lam * o[tq:]
    return _rms(a, gs) * (1.0 - lam_init)


def _attn_ctx_body(lam_ref, gs_ref, q_ref, k_ref, v_ref, o_ref, *, lam_init):
    lam = _lam(lam_ref, lam_init)
    for h in range(N_HEADS):
        sl = slice(h * HEAD_W, (h + 1) * HEAD_W)
        carry = _softmax_step(_stack_maps(q_ref[:, sl]), k_ref[:, sl], v_ref[:, sl], _softmax_init(2 * T_CTX))
        o_ref[:, sl] = _diff_out(carry, T_CTX, lam, gs_ref[...], lam_init).astype(o_ref.dtype)


def _attn_ctx(qkv, lam_params, g_subln, layer, lam_init):
    qkv3 = qkv.reshape(N_CTX_B, T_CTX, QKV_W)
    blk = lambda part: pl.BlockSpec((None, T_CTX, ATTN_W), lambda b: (b, 0, part))
    return pl.pallas_call(
        functools.partial(_attn_ctx_body, lam_init=lam_init),
        grid=(N_CTX_B,),
        in_specs=[pl.BlockSpec((None, 4, QK_DIM), lambda b: (layer, 0, 0)),
                  pl.BlockSpec((None, 1, HEAD_W), lambda b: (layer, 0, 0)),
                  blk(0), blk(1), blk(2)],
        out_specs=pl.BlockSpec((None, T_CTX, ATTN_W), lambda b: (b, 0, 0)),
        out_shape=jax.ShapeDtypeStruct((N_CTX_B, T_CTX, ATTN_W), BF16),
        compiler_params=_cparams(("arbitrary",)),
        name="attn_ctx",
    )(lam_params, g_subln.reshape(DEPTH, 1, HEAD_W), qkv3, qkv3, qkv3)


def _attn_lat_body(lam_ref, gs_ref, q_ref, kc_ref, vc_ref, kl_ref, vl_ref, o_ref, kcat, vcat, *, lam_init):
    @pl.when(pl.program_id(2) == 0)
    def _():
        kcat[0:T_PAST, :] = kc_ref[...]
        kcat[T_PAST:, :] = kl_ref[...]
        vcat[0:T_PAST, :] = vc_ref[...]
        vcat[T_PAST:, :] = vl_ref[...]

    qq = _stack_maps(q_ref[...])

    def body(c, carry):
        start = pl.multiple_of(c * TK, TK)
        return _softmax_step(qq, kcat[pl.ds(start, TK), :], vcat[pl.ds(start, TK), :], carry)

    carry = lax.fori_loop(0, (T_PAST + T_LAT) // TK, body, _softmax_init(2 * TQ), unroll=ATTN_UNROLL)
    o_ref[...] = _diff_out(carry, TQ, _lam(lam_ref, lam_init), gs_ref[...], lam_init).astype(o_ref.dtype)


def _attn_lat(qkv, ck, cv, lam_params, g_subln, layer, lam_init):
    qkv3 = qkv.reshape(N_LAT_B, T_LAT, QKV_W)
    full = lambda off: pl.BlockSpec((None, T_LAT, HEAD_W), lambda b, h, i: (b, 0, off + h))
    past = pl.BlockSpec((None, T_PAST, HEAD_W), lambda b, h, i: (b, 0, h))
    return pl.pallas_call(
        functools.partial(_attn_lat_body, lam_init=lam_init),
        grid=(N_LAT_B, N_HEADS, T_LAT // TQ),
        in_specs=[pl.BlockSpec((None, 4, QK_DIM), lambda b, h, i: (layer, 0, 0)),
                  pl.BlockSpec((None, 1, HEAD_W), lambda b, h, i: (layer, 0, 0)),
                  pl.BlockSpec((None, TQ, HEAD_W), lambda b, h, i: (b, i, h)),
                  past, past, full(N_HEADS), full(2 * N_HEADS)],
        out_specs=pl.BlockSpec((None, TQ, HEAD_W), lambda b, h, i: (b, i, h)),
        out_shape=jax.ShapeDtypeStruct((N_LAT_B, T_LAT, ATTN_W), BF16),
        scratch_shapes=[pltpu.VMEM((T_PAST + T_LAT, HEAD_W), BF16)] * 2,
        compiler_params=_cparams(("arbitrary", "arbitrary", "arbitrary")),
        name="attn_lat",
    )(lam_params, g_subln.reshape(DEPTH, 1, HEAD_W), qkv3, ck, cv, qkv3, qkv3)


def _fold_groups(ur, ui, wf_ref, scale):
    outs = []
    for g in range(F_GROUPS):
        sl = slice(g * F_CH, (g + 1) * F_CH)
        lhs = jnp.concatenate([ur[:, sl], ui[:, sl]], axis=1).astype(BF16)
        outs.append(jnp.dot(lhs, wf_ref[g], preferred_element_type=F32))
    return jnp.concatenate(outs, axis=1) * scale


def _four_ctx_body(f_ref, dft_ref, wf_ref, o_ref):
    u = jnp.dot(dft_ref[...], f_ref[...].astype(BF16), preferred_element_type=F32)
    o_ref[...] = _fold_groups(u[:T_CTX], u[T_CTX:], wf_ref, 1.0 / math.sqrt(T_CTX * F_CH)).astype(o_ref.dtype)


def _four_ctx(f, wfold_bf, layer):
    ang = 2.0 * np.pi * np.outer(np.arange(T_CTX), np.arange(T_CTX)) / T_CTX
    dft = jnp.asarray(np.concatenate([np.cos(ang), -np.sin(ang)], axis=0), F32).astype(BF16)
    return pl.pallas_call(
        _four_ctx_body,
        grid=(N_CTX_B,),
        in_specs=[pl.BlockSpec((None, T_CTX, FOUR_W), lambda b: (b, 0, 0)),
                  pl.BlockSpec((2 * T_CTX, T_CTX), lambda b: (0, 0)),
                  pl.BlockSpec((None, F_GROUPS, 2 * F_CH, F_CH), lambda b: (layer, 0, 0, 0))],
        out_specs=pl.BlockSpec((None, T_CTX, FOUR_W), lambda b: (b, 0, 0)),
        out_shape=jax.ShapeDtypeStruct((N_CTX_B, T_CTX, FOUR_W), BF16),
        compiler_params=_cparams(("arbitrary",)),
        name="fourier_ctx",
    )(f.reshape(N_CTX_B, T_CTX, FOUR_W), dft, wfold_bf)


def _fft_body(x_ref, ma_ref, twc_ref, tws_ref, mc_ref, ms_ref, wf_ref, o_ref, h_ref):
    nb = FFT_R * FFT_S
    nj = FFT_R // FFT_S
    j = pl.program_id(1)

    @pl.when(j < nj)
    def _():
        x = x_ref[...].reshape(nb, FOUR_W).astype(BF16)
        g = jnp.dot(ma_ref[...], x, preferred_element_type=F32)
        gr, gi = g[:nb], g[nb:]
        twc = jnp.tile(twc_ref[...], (1, FOUR_W // 128))
        tws = jnp.tile(tws_ref[...], (1, FOUR_W // 128))
        lo = pl.ds(pl.multiple_of(j * FFT_S, FFT_S), FFT_S)
        h_ref[lo, :, 0:FOUR_W] = (gr * twc + gi * tws).reshape(FFT_S, FFT_R, FOUR_W)
        h_ref[lo, :, FOUR_W:] = (gi * twc - gr * tws).reshape(FFT_S, FFT_R, FOUR_W)

    @pl.when(j >= nj)
    def _():
        hi = pl.ds(pl.multiple_of((j - nj) * FFT_S, FFT_S), FFT_S)
        h = h_ref[:, hi, :].reshape(nb, 2 * FOUR_W).astype(BF16)
        p = jnp.dot(mc_ref[...], h, preferred_element_type=F32)
        q = jnp.dot(ms_ref[...], h, preferred_element_type=F32)
        ur = p[:, :FOUR_W] + q[:, FOUR_W:]
        ui = p[:, FOUR_W:] - q[:, :FOUR_W]
        out = _fold_groups(ur, ui, wf_ref, 1.0 / math.sqrt(T_LAT * F_CH))
        o_ref[...] = out.reshape(FFT_R, FFT_S, FOUR_W)


def _fft_consts():
    r, s = FFT_R, FFT_S
    nb = r * s
    ang = 2.0 * np.pi * np.outer(np.arange(r), np.arange(r)) / r
    c, sn = np.cos(ang), np.sin(ang)
    eye = np.eye(s)
    ma = np.concatenate([np.einsum('pb,ts->tpbs', c, eye).reshape(nb, nb),
                         np.einsum('pb,ts->tpbs', -sn, eye).reshape(nb, nb)], axis=0)
    mbc = np.einsum('pa,ts->ptas', c, eye).reshape(nb, nb)
    mbs = np.einsum('pa,ts->ptas', sn, eye).reshape(nb, nb)
    tw = 2.0 * np.pi * np.outer(np.arange(r), np.arange(r)).reshape(-1) / (r * r)
    twc = np.broadcast_to(np.cos(tw)[:, None], (r * r, 128))
    tws = np.broadcast_to(np.sin(tw)[:, None], (r * r, 128))
    bf = lambda a: jnp.asarray(a, F32).astype(BF16)
    return bf(ma), bf(mbc), bf(mbs), jnp.asarray(twc, F32), jnp.asarray(tws, F32)


def _four_lat(f, wfold_bf, layer, consts):
    ma, mbc, mbs, twc, tws = consts
    r, s = FFT_R, FFT_S
    nb = r * s
    nj = r // s
    f4 = f.reshape(N_LAT_B, r, r, FOUR_W)
    a_step = lambda j: jnp.minimum(j, nj - 1)
    b_step = lambda j: jnp.maximum(j - nj, 0)
    const = lambda b, j: (0, 0)
    out = pl.pallas_call(
        _fft_body,
        grid=(N_LAT_B, 2 * nj),
        in_specs=[pl.BlockSpec((None, r, s, FOUR_W), lambda b, j: (b, 0, a_step(j), 0)),
                  pl.BlockSpec((2 * nb, nb), const),
                  pl.BlockSpec((nb, 128), lambda b, j: (a_step(j), 0)),
                  pl.BlockSpec((nb, 128), lambda b, j: (a_step(j), 0)),
                  pl.BlockSpec((nb, nb), const),
                  pl.BlockSpec((nb, nb), const),
                  pl.BlockSpec((None, F_GROUPS, 2 * F_CH, F_CH), lambda b, j: (layer, 0, 0, 0))],
        out_specs=pl.BlockSpec((None, r, s, FOUR_W), lambda b, j: (b, 0, b_step(j), 0)),
        out_shape=jax.ShapeDtypeStruct((N_LAT_B, r, r, FOUR_W), F32),
        scratch_shapes=[pltpu.VMEM((r, r, 2 * FOUR_W), F32)],
        compiler_params=_cparams(("arbitrary", "arbitrary")),
        name="fft_lat",
    )(f4, ma, twc, tws, mbc, mbs, wfold_bf)
    return out.reshape(N_LAT_B * T_LAT, FOUR_W)


def _outproj_body(a_ref, f_ref, x_ref, gt_ref, sh_ref, sc_ref, gpost_ref, gpre_ref, wo_ref, *rest,
                  moe, group0, tiles_per_group):
    if moe:
        wr_ref, x1_ref, hn_ref, gates_ref, sel_ref = rest
    else:
        x1_ref, hn_ref = rest
    grp = _group_of(pl.program_id(0), group0, tiles_per_group)
    mix_in = jnp.concatenate([a_ref[...], f_ref[...].astype(BF16)], axis=1)
    mixed = jnp.dot(mix_in, wo_ref[...], preferred_element_type=F32)
    x1 = x_ref[...] + gt_ref[pl.ds(grp, 1), :] * _rms(mixed, gpost_ref[...])
    x1_ref[...] = x1
    hn = _rms(x1, gpre_ref[...]) * (1.0 + sc_ref[pl.ds(grp, 1), :]) + sh_ref[pl.ds(grp, 1), :]
    hn_ref[...] = hn.astype(hn_ref.dtype)
    if moe:
        wr = wr_ref[...]
        wr_hi = wr.astype(BF16)
        wr_lo = (wr - wr_hi.astype(F32)).astype(BF16)
        hn_hi = hn.astype(BF16)
        hn_lo = (hn - hn_hi.astype(F32)).astype(BF16)
        hh = jnp.dot(hn_hi, jnp.concatenate([wr_hi, wr_lo], axis=1), preferred_element_type=F32)
        logits = hh[:, :E_PAD] + hh[:, E_PAD:] + jnp.dot(hn_lo, wr_hi, preferred_element_type=F32)
        lane = lax.broadcasted_iota(jnp.int32, logits.shape, 1).astype(F32)
        neg = jnp.float32(-jnp.inf)
        logits = jnp.where(lane < float(N_EXPERTS), logits, neg)
        m1 = jnp.max(logits, axis=-1, keepdims=True)
        i1 = jnp.min(jnp.where(logits == m1, lane, float(E_PAD)), axis=-1, keepdims=True)
        rest_l = jnp.where(lane == i1, neg, logits)
        m2 = jnp.max(rest_l, axis=-1, keepdims=True)
        i2 = jnp.min(jnp.where(rest_l == m2, lane, float(E_PAD)), axis=-1, keepdims=True)
        e2 = jnp.exp(m2 - m1)
        w1 = 1.0 / (1.0 + e2)
        w2 = e2 / (1.0 + e2)
        gates_ref[...] = jnp.where(lane == i1, w1, 0.0) + jnp.where(lane == i2, w2, 0.0)
        sel_ref[...] = jnp.where((lane == i1) | (lane == i2), 1.0, 0.0).astype(sel_ref.dtype)


def _outproj(attn_o, four_o, x, mod, layer, g_post, g_ffn_pre, w_out_bf, w_router_pad, group0, tiles_per_group):
    rows = x.shape[0]
    moe = w_router_pad is not None
    row = lambda i: (i, 0)
    const = lambda i: (0, 0)
    modspec = lambda k: pl.BlockSpec((None, N_GROUPS_PAD, D), lambda i: (layer, 0, k))
    in_specs = [pl.BlockSpec((TM, ATTN_W), row), pl.BlockSpec((TM, FOUR_W), row), pl.BlockSpec((TM, D), row),
                modspec(2), modspec(3), modspec(4),
                pl.BlockSpec((1, D), const), pl.BlockSpec((1, D), const), pl.BlockSpec((D, D), const)]
    args = [attn_o, four_o, x, mod, mod, mod, g_post.reshape(1, D), g_ffn_pre.reshape(1, D), w_out_bf]
    out_specs = [pl.BlockSpec((TM, D), row), pl.BlockSpec((TM, D), row)]
    out_shape = [jax.ShapeDtypeStruct((rows, D), F32), jax.ShapeDtypeStruct((rows, D), F32 if moe else BF16)]
    if moe:
        in_specs.append(pl.BlockSpec((D, E_PAD), const))
        args.append(w_router_pad)
        out_specs += [pl.BlockSpec((TM, E_PAD), row)] * 2
        out_shape += [jax.ShapeDtypeStruct((rows, E_PAD), F32), jax.ShapeDtypeStruct((rows, E_PAD), BF16)]
    return pl.pallas_call(
        functools.partial(_outproj_body, moe=moe, group0=group0, tiles_per_group=tiles_per_group),
        grid=(rows // TM,),
        in_specs=in_specs, out_specs=out_specs, out_shape=out_shape,
        compiler_params=_cparams(("arbitrary",)),
        name="outproj_moe" if moe else "outproj",
    )(*args)


def _swiglu(hn, wg, wu, wd):
    g = jnp.dot(hn, wg, preferred_element_type=F32)
    u = jnp.dot(hn, wu, preferred_element_type=F32)
    return jnp.dot((_silu(g) * u).astype(BF16), wd, preferred_element_type=F32)


def _ffn_dense_body(hn_ref, x1_ref, gt_ref, gpost_ref, wg_ref, wu_ref, wd_ref, o_ref, *, group0, tiles_per_group):
    grp = _group_of(pl.program_id(0), group0, tiles_per_group)
    ff = _swiglu(hn_ref[...], wg_ref[...], wu_ref[...], wd_ref[...])
    o_ref[...] = x1_ref[...] + gt_ref[pl.ds(grp, 1), :] * _rms(ff, gpost_ref[...])


def _ffn_dense(hn, x1, mod, layer, g_post, wg, wu, wd, group0, tiles_per_group):
    rows = x1.shape[0]
    row = lambda i: (i, 0)
    const = lambda i: (0, 0)
    resident = lambda shape: pl.BlockSpec(shape, const, pipeline_mode=pl.Buffered(1))
    return pl.pallas_call(
        functools.partial(_ffn_dense_body, group0=group0, tiles_per_group=tiles_per_group),
        grid=(rows // TM,),
        in_specs=[pl.BlockSpec((TM, D), row), pl.BlockSpec((TM, D), row),
                  pl.BlockSpec((None, N_GROUPS_PAD, D), lambda i: (layer, 0, 5)),
                  pl.BlockSpec((1, D), const),
                  resident((D, D_FF)), resident((D, D_FF)), resident((D_FF, D))],
        out_specs=pl.BlockSpec((TM, D), row),
        out_shape=jax.ShapeDtypeStruct((rows, D), F32),
        compiler_params=_cparams(("arbitrary",)),
        name="ffn_dense",
    )(hn, x1, mod, g_post.reshape(1, D), wg, wu, wd)


def _route_body(gates_ref, sel_ref, ltri_ref, slot_ref, meta_ref, cnt_ref, off_ref, run_ref):
    phase, t = pl.program_id(0), pl.program_id(1)
    sel = sel_ref[...]

    @pl.when((phase == 0) & (t == 0))
    def _():
        cnt_ref[...] = jnp.zeros_like(cnt_ref)

    @pl.when(phase == 0)
    def _():
        cnt_ref[...] += jnp.sum(sel.astype(F32), axis=0, keepdims=True)

    @pl.when((phase == 1) & (t == 0))
    def _():
        cnt = cnt_ref[...]
        padded = jnp.ceil(cnt * (1.0 / TMOE)) * TMOE
        lane = lax.broadcasted_iota(jnp.int32, cnt.shape, 1).astype(F32)
        start = lane * TMOE
        off = jnp.zeros_like(cnt)
        tile_e = jnp.zeros_like(cnt)
        tile_n = jnp.zeros_like(cnt)
        running = jnp.zeros((1, 1), F32)
        for e in range(N_EXPERTS):
            pick = lane == float(e)
            cnt_e = jnp.sum(jnp.where(pick, cnt, 0.0), axis=-1, keepdims=True)
            pad_e = jnp.sum(jnp.where(pick, padded, 0.0), axis=-1, keepdims=True)
            off = jnp.where(pick, running, off)
            in_seg = (start >= running) & (start < running + pad_e)
            tile_e = jnp.where(in_seg, float(e), tile_e)
            tile_n = jnp.where(in_seg, jnp.clip(cnt_e - (start - running), 0.0, float(TMOE)), tile_n)
            running = running + pad_e
        tile_e = jnp.where(start >= running, float(N_EXPERTS - 1), tile_e)
        off_ref[...] = off
        run_ref[...] = jnp.zeros_like(run_ref)
        row = lax.broadcasted_iota(jnp.int32, meta_ref.shape, 0)
        meta_ref[...] = jnp.where(row == 0, tile_e, jnp.where(row == 1, tile_n, 0.0))

    @pl.when(phase == 1)
    def _():
        earlier = jnp.dot(ltri_ref[...], sel, preferred_element_type=F32)
        pos = earlier + run_ref[...] + off_ref[...]
        run_ref[...] += jnp.sum(sel.astype(F32), axis=0, keepdims=True)
        chosen = sel > 0
        s_lo = jnp.min(jnp.where(chosen, pos, float(NP)), axis=-1, keepdims=True)
        s_hi = jnp.max(jnp.where(chosen, pos, -1.0), axis=-1, keepdims=True)
        g = gates_ref[...]
        w_lo = jnp.sum(jnp.where(chosen & (pos == s_lo), g, 0.0), axis=-1, keepdims=True)
        w_hi = jnp.sum(jnp.where(chosen & (pos == s_hi), g, 0.0), axis=-1, keepdims=True)
        lane = lax.broadcasted_iota(jnp.int32, pos.shape, 1)
        slot_ref[...] = jnp.where(lane == 0, s_lo, jnp.where(lane == 1, s_hi,
                                  jnp.where(lane == 2, w_lo, jnp.where(lane == 3, w_hi, 0.0))))


def _route(gates, sel):
    rows = gates.shape[0]
    n = rows // TM
    ltri = jnp.asarray(np.tril(np.ones((TM, TM)), -1), F32).astype(BF16)
    tile = lambda p, t: (t, 0)
    return pl.pallas_call(
        _route_body,
        grid=(2, n),
        in_specs=[pl.BlockSpec((TM, E_PAD), tile), pl.BlockSpec((TM, E_PAD), tile),
                  pl.BlockSpec((TM, TM), lambda p, t: (0, 0))],
        out_specs=[pl.BlockSpec((TM, E_PAD), lambda p, t: (t * p, 0)),
                   pl.BlockSpec((N_GROUPS_PAD, E_PAD), lambda p, t: (0, 0))],
        out_shape=[jax.ShapeDtypeStruct((rows, E_PAD), F32), jax.ShapeDtypeStruct((N_GROUPS_PAD, E_PAD), F32)],
        scratch_shapes=[pltpu.VMEM((1, E_PAD), F32)] * 3,
        compiler_params=_cparams(("arbitrary", "arbitrary")),
        name="moe_route",
    )(gates, sel, ltri)


def _row_copies(slot_ref, src_of, dst_of, sem):
    sub = 8

    def copy(g, j, k):
        r = pl.multiple_of(g * sub, sub) + j
        s = slot_ref[0, 2 * r + k]
        return pltpu.make_async_copy(src_of(r, k, s), dst_of(r, k, s), sem)

    def start(g, carry):
        for j in range(sub):
            for k in range(TOP_K):
                copy(g, j, k).start()
        return carry

    def wait(g, carry):
        for j in range(sub):
            for k in range(TOP_K):
                copy(g, j, k).wait()
        return carry

    lax.fori_loop(0, TM // sub, start, 0)
    lax.fori_loop(0, TM // sub, wait, 0)


def _dispatch_body(slot_ref, hn_ref, xs_in_ref, xs_ref, sem):
    del xs_in_ref
    _row_copies(slot_ref,
                lambda r, k, s: hn_ref.at[pl.ds(r, 1), :],
                lambda r, k, s: xs_ref.at[pl.ds(s, 1), :], sem)


def _dispatch(hn, slots, xs):
    rows = hn.shape[0]
    n = rows // TM
    return pl.pallas_call(
        _dispatch_body,
        grid=(n,),
        in_specs=[pl.BlockSpec((None, 1, 2 * TM), lambda i: (i, 0, 0), memory_space=pltpu.SMEM),
                  pl.BlockSpec((TM, D), lambda i: (i, 0)),
                  pl.BlockSpec(memory_space=pl.ANY)],
        out_specs=pl.BlockSpec(memory_space=pl.ANY),
        out_shape=jax.ShapeDtypeStruct((NP, D), F32),
        scratch_shapes=[pltpu.SemaphoreType.DMA],
        input_output_aliases={2: 0},
        compiler_params=_cparams(("arbitrary",)),
        name="moe_dispatch",
    )(slots.reshape(n, 1, 2 * TM), hn, xs)


def _experts_body(te_ref, tn_ref, xs_ref, wg_ref, wu_ref, wd_ref, ys_ref, wg_bf, wu_bf, wd_bf):
    i = pl.program_id(0)
    fresh = (i == 0) | (te_ref[i] != te_ref[jnp.maximum(i - 1, 0)])

    @pl.when(fresh)
    def _():
        wg_bf[...] = wg_ref[...].astype(BF16)
        wu_bf[...] = wu_ref[...].astype(BF16)
        wd_bf[...] = wd_ref[...].astype(BF16)

    n_real = tn_ref[i]

    @pl.when(n_real > 0)
    def _():
        ys_ref[...] = _swiglu(xs_ref[...].astype(BF16), wg_bf[...], wu_bf[...], wd_bf[...])

    @pl.when(n_real == 0)
    def _():
        ys_ref[...] = jnp.zeros_like(ys_ref)


def _experts(xs, tile_e, tile_n, eg, eu, ed):
    tile = lambda i, te, tn: (i, 0)
    wspec = lambda shape: pl.BlockSpec((None,) + shape, lambda i, te, tn: (te[i], 0, 0))
    return pl.pallas_call(
        _experts_body,
        grid_spec=pltpu.PrefetchScalarGridSpec(
            num_scalar_prefetch=2,
            grid=(N_TILES,),
            in_specs=[pl.BlockSpec((TMOE, D), tile), wspec((D, D_FF_E)), wspec((D, D_FF_E)), wspec((D_FF_E, D))],
            out_specs=pl.BlockSpec((TMOE, D), tile),
            scratch_shapes=[pltpu.VMEM((D, D_FF_E), BF16), pltpu.VMEM((D, D_FF_E), BF16),
                            pltpu.VMEM((D_FF_E, D), BF16)]),
        out_shape=jax.ShapeDtypeStruct((NP, D), F32),
        compiler_params=_cparams(("arbitrary",)),
        name="moe_experts",
    )(tile_e, tile_n, xs, eg, eu, ed)


def _combine_body(slot_ref, w_ref, x1_ref, gt_ref, gpost_ref, ys_ref, o_ref, ybuf, sem, *, group0, tiles_per_group):
    grp = _group_of(pl.program_id(0), group0, tiles_per_group)
    _row_copies(slot_ref,
                lambda r, k, s: ys_ref.at[pl.ds(s, 1), :],
                lambda r, k, s: ybuf.at[k, pl.ds(r, 1), :], sem)
    w = w_ref[...]
    lane = lax.broadcasted_iota(jnp.int32, w.shape, 1)
    w_lo = jnp.sum(jnp.where(lane == 2, w, 0.0), axis=-1, keepdims=True)
    w_hi = jnp.sum(jnp.where(lane == 3, w, 0.0), axis=-1, keepdims=True)
    ff = w_lo * ybuf[0] + w_hi * ybuf[1]
    o_ref[...] = x1_ref[...] + gt_ref[pl.ds(grp, 1), :] * _rms(ff, gpost_ref[...])


def _combine(ys, slots, slot_w, x1, mod, layer, g_post, group0, tiles_per_group):
    rows = x1.shape[0]
    n = rows // TM
    row = lambda i: (i, 0)
    return pl.pallas_call(
        functools.partial(_combine_body, group0=group0, tiles_per_group=tiles_per_group),
        grid=(n,),
        in_specs=[pl.BlockSpec((None, 1, 2 * TM), lambda i: (i, 0, 0), memory_space=pltpu.SMEM),
                  pl.BlockSpec((TM, E_PAD), row), pl.BlockSpec((TM, D), row),
                  pl.BlockSpec((None, N_GROUPS_PAD, D), lambda i: (layer, 0, 5)),
                  pl.BlockSpec((1, D), lambda i: (0, 0)),
                  pl.BlockSpec(memory_space=pl.ANY)],
        out_specs=pl.BlockSpec((TM, D), row),
        out_shape=jax.ShapeDtypeStruct((rows, D), F32),
        scratch_shapes=[pltpu.VMEM((2, TM, D), F32), pltpu.SemaphoreType.DMA],
        compiler_params=_cparams(("arbitrary",)),
        name="moe_combine",
    )(slots.reshape(n, 1, 2 * TM), slot_w, x1, mod, g_post.reshape(1, D), ys)


def kernel(x_prompt, x_sample, cache_k, cache_v, c, c_ctx, g_attn_pre, g_attn_post, g_ffn_pre, g_ffn_post,
           w_ada, b_ada, w_in, lam_params, g_subln, w_fnet, w_out, w_gate, w_up, w_down, w_router,
           e_gate, e_up, e_down):
    assert x_prompt.shape == (N_CTX_B, T_CTX, D) and x_sample.shape == (N_LAT_B, T_LAT, D)
    assert cache_k.shape == (N_LAT_B, DEPTH, T_PAST, N_HEADS, 2, QK_DIM)

    cond = jnp.zeros((N_GROUPS_PAD, D), F32).at[0].set(c_ctx).at[1:1 + N_LAT_B].set(c)
    mod = _ada(cond, w_ada, b_ada)
    wfold_bf = _wfold(w_fnet).astype(BF16)
    rope_tabs = _rope_tables()
    fft_consts = _fft_consts()

    w_in_bf = w_in.astype(BF16)
    w_out_bf = w_out.astype(BF16)
    w_router_pad = jnp.zeros((DEPTH // 2, D, E_PAD), F32).at[:, :, :N_EXPERTS].set(w_router)
    ck_bf = cache_k.reshape(N_LAT_B, DEPTH, T_PAST, ATTN_W).astype(BF16)
    cv_bf = cache_v.reshape(N_LAT_B, DEPTH, T_PAST, ATTN_W).astype(BF16)

    lat_tiles = T_LAT // TM
    streams = [dict(x=x_prompt.reshape(N_CTX_B * T_CTX, D), group0=0, tiles=None, lat=False),
               dict(x=x_sample.reshape(N_LAT_B * T_LAT, D), group0=1, tiles=lat_tiles, lat=True)]
    new_k, new_v = [], []
    for l in range(DEPTH):
        lam_init = 0.8 - 0.6 * math.exp(-0.3 * l)
        i = l // 2
        routed = []
        for s in streams:
            g0, tiles = s["group0"], s["tiles"]
            if s["lat"]:
                qkv, f = _inproj(s["x"], mod, l, g_attn_pre[l], w_in_bf[l], rope_tabs, g0, tiles)
                attn_o = _attn_lat(qkv, ck_bf[:, l], cv_bf[:, l], lam_params, g_subln, l, lam_init)
                attn_o = attn_o.reshape(N_LAT_B * T_LAT, ATTN_W)
                four_o = _four_lat(f, wfold_bf, l, fft_consts)
            else:
                qkv, f, k32, v32 = _inproj(s["x"], mod, l, g_attn_pre[l], w_in_bf[l], None, g0, tiles)
                new_k.append(k32)
                new_v.append(v32)
                attn_o = _attn_ctx(qkv, lam_params, g_subln, l, lam_init).reshape(N_CTX_B * T_CTX, ATTN_W)
                four_o = _four_ctx(f, wfold_bf, l).reshape(N_CTX_B * T_CTX, FOUR_W)
            if l % 2 == 0:
                x1, hn = _outproj(attn_o, four_o, s["x"], mod, l, g_attn_post[l], g_ffn_pre[l], w_out_bf[l],
                                  None, g0, tiles)
                s["x"] = _ffn_dense(hn, x1, mod, l, g_ffn_post[l], w_gate[i].astype(BF16), w_up[i].astype(BF16),
                                    w_down[i].astype(BF16), g0, tiles)
            else:
                routed.append(_outproj(attn_o, four_o, s["x"], mod, l, g_attn_post[l], g_ffn_pre[l],
                                       w_out_bf[l], w_router_pad[i], g0, tiles))
        if l % 2 == 1:
            slot_w, meta = _route(jnp.concatenate([r[2] for r in routed]), jnp.concatenate([r[3] for r in routed]))
            slots = slot_w[:, :2].astype(jnp.int32)
            tile_e = meta[0, :N_TILES].astype(jnp.int32)
            tile_n = meta[1, :N_TILES].astype(jnp.int32)
            bounds = np.cumsum([0] + [r[0].shape[0] for r in routed])
            xs = jnp.zeros((NP, D), F32)
            for r, lo, hi in zip(routed, bounds[:-1], bounds[1:]):
                xs = _dispatch(r[1], slots[lo:hi], xs)
            ys = _experts(xs, tile_e, tile_n, e_gate[i], e_up[i], e_down[i])
            for s, r, lo, hi in zip(streams, routed, bounds[:-1], bounds[1:]):
                s["x"] = _combine(ys, slots[lo:hi], slot_w[lo:hi], r[0], mod, l, g_ffn_post[l],
                                  s["group0"], s["tiles"])

    y_prompt = streams[0]["x"].reshape(N_CTX_B, T_CTX, D)
    y_sample = streams[1]["x"].reshape(N_LAT_B, T_LAT, D)
    new_cache_k = jnp.stack([k.reshape(N_CTX_B, T_CTX, N_HEADS, 2, QK_DIM) for k in new_k], axis=1)
    new_cache_v = jnp.stack([v.reshape(N_CTX_B, T_CTX, N_HEADS, HEAD_W) for v in new_v], axis=1)
    return (y_prompt, y_sample, new_cache_k, new_cache_v)
```

```python
import functools
import math

import numpy as np
import jax
import jax.numpy as jnp
from jax import lax
from jax.experimental import pallas as pl
from jax.experimental.pallas import tpu as pltpu

F32 = jnp.float32
BF16 = jnp.bfloat16

D = 1024
N_CTX_B = 16
T_CTX = 256
N_LAT_B = 2
T_LAT = 4096
T_PAST = 512
DEPTH = 2
GRID_W = 64
N_HEADS = 4
QK_DIM = 64
HEAD_W = 2 * QK_DIM
ATTN_W = N_HEADS * HEAD_W
F_GROUPS = 4
F_CH = 128
FOUR_W = F_GROUPS * F_CH
IN_W = 3 * ATTN_W + FOUR_W
QKV_W = 3 * ATTN_W
D_FF = 2816
N_EXPERTS = 8
D_FF_E = 1024
N_MOD = 6
EPS = 1e-6
ROPE_THETA = 10000.0
N_GROUPS_PAD = 8
E_PAD = 128

TM = 512
TM_OUT = 1024
TOP_K = 2
TMOE = 512
N_TOKENS = N_CTX_B * T_CTX + N_LAT_B * T_LAT
NP = TOP_K * N_TOKENS + N_EXPERTS * TMOE
N_TILES = NP // TMOE
TQ = 512
TK = 1536
ATTN_UNROLL = 3
FFT_R = 64
FFT_S = 8
VMEM_LIMIT = 56 * 1024 * 1024


def _cparams(sem):
    return pltpu.CompilerParams(dimension_semantics=sem, vmem_limit_bytes=VMEM_LIMIT)


def _rms(x, g):
    ms = jnp.mean(x * x, axis=-1, keepdims=True)
    return x * lax.rsqrt(ms + EPS) * g


def _silu(x):
    return x * (1.0 / (1.0 + jnp.exp(-x)))


def _ada_body(c_ref, w_ref, b_ref, o_ref):
    s = _silu(c_ref[...])
    o_ref[...] = jnp.dot(s, w_ref[...], preferred_element_type=F32,
                         precision=lax.Precision.HIGHEST) + b_ref[...]


def _ada(cond, w_ada, b_ada):
    tn = 1536
    return pl.pallas_call(
        _ada_body,
        grid=(DEPTH, N_MOD * D // tn),
        in_specs=[pl.BlockSpec((N_GROUPS_PAD, D), lambda l, j: (0, 0)),
                  pl.BlockSpec((None, D, tn), lambda l, j: (l, 0, j)),
                  pl.BlockSpec((None, 1, tn), lambda l, j: (l, 0, j))],
        out_specs=pl.BlockSpec((None, N_GROUPS_PAD, tn), lambda l, j: (l, 0, j)),
        out_shape=jax.ShapeDtypeStruct((DEPTH, N_GROUPS_PAD, N_MOD * D), F32),
        compiler_params=_cparams(("arbitrary", "arbitrary")),
        name="ada_mod",
    )(cond, w_ada, b_ada.reshape(DEPTH, 1, N_MOD * D))


def _wfold_body(cc_ref, sc_ref, w_ref, o_ref):
    w = w_ref[...]
    o_ref[0:F_CH, :] = jnp.dot(cc_ref[...], w, preferred_element_type=F32,
                               precision=lax.Precision.HIGHEST)
    o_ref[F_CH:2 * F_CH, :] = jnp.dot(sc_ref[...], w, preferred_element_type=F32,
                                      precision=lax.Precision.HIGHEST)


def _wfold(w_fnet):
    ang = 2.0 * np.pi * np.outer(np.arange(F_CH), np.arange(F_CH)) / F_CH
    cc = jnp.asarray(np.cos(ang), F32)
    sc = jnp.asarray(np.sin(ang), F32)
    cspec = pl.BlockSpec((F_CH, F_CH), lambda l, g: (0, 0))
    return pl.pallas_call(
        _wfold_body,
        grid=(DEPTH, F_GROUPS),
        in_specs=[cspec, cspec, pl.BlockSpec((None, None, F_CH, F_CH), lambda l, g: (l, g, 0, 0))],
        out_specs=pl.BlockSpec((None, None, 2 * F_CH, F_CH), lambda l, g: (l, g, 0, 0)),
        out_shape=jax.ShapeDtypeStruct((DEPTH, F_GROUPS, 2 * F_CH, F_CH), F32),
        compiler_params=_cparams(("arbitrary", "arbitrary")),
        name="fnet_fold",
    )(cc, sc, w_fnet)


def _group_of(i, group0, tiles_per_group):
    if tiles_per_group is None:
        return group0
    return group0 + lax.div(i, jnp.int32(tiles_per_group))


def _inproj_body(x_ref, sh_ref, sc_ref, g_ref, w_ref, *rest, rope, cache_layer, group0, tiles_per_group):
    if rope:
        cos_ref, sin_ref, qkv_ref, f_ref = rest
    else:
        qkv_ref, f_ref, k32_ref, v32_ref = rest[-4:]
    grp = _group_of(pl.program_id(0), group0, tiles_per_group)
    sh = sh_ref[pl.ds(grp, 1), :]
    sc = sc_ref[pl.ds(grp, 1), :]
    hn = _rms(x_ref[...], g_ref[...]) * (1.0 + sc) + sh
    proj = jnp.dot(hn.astype(BF16), w_ref[...], preferred_element_type=F32)
    qk = proj[:, :2 * ATTN_W]
    v = proj[:, 2 * ATTN_W:QKV_W]
    if rope:
        cos = jnp.tile(cos_ref[...], (1, 2 * ATTN_W // HEAD_W))
        sin = jnp.tile(sin_ref[...], (1, 2 * ATTN_W // HEAD_W))
        lane = lax.broadcasted_iota(jnp.int32, qk.shape, 1)
        low = (lane % 32) < 16
        rot = jnp.where(low, pltpu.roll(qk, 2 * ATTN_W - 16, 1), pltpu.roll(qk, 16, 1))
        qk = qk * cos + rot * sin
    else:
        for ref, val in ((k32_ref, qk[:, ATTN_W:]), (v32_ref, v)):
            val = val.reshape(TM // T_CTX, T_CTX, ATTN_W)
            if cache_layer == 0:
                ref[:, 0] = val
                ref[:, 1:] = jnp.zeros((TM // T_CTX, DEPTH - 1, T_CTX, ATTN_W), F32)
            else:
                ref[...] = val
    qkv_ref[:, 0:ATTN_W] = (qk[:, :ATTN_W] * (QK_DIM ** -0.5 * math.log2(math.e))).astype(BF16)
    qkv_ref[:, ATTN_W:2 * ATTN_W] = qk[:, ATTN_W:].astype(BF16)
    qkv_ref[:, 2 * ATTN_W:] = v.astype(BF16)
    f_ref[...] = proj[:, QKV_W:]


def _inproj(x, mod, layer, g_pre, w_in_bf, rope_tabs, group0, tiles_per_group, cache=None):
    rows = x.shape[0]
    rope = rope_tabs is not None
    aliases = {}
    row = lambda i: (i, 0)
    const = lambda i: (0, 0)
    in_specs = [pl.BlockSpec((TM, D), row),
                pl.BlockSpec((None, N_GROUPS_PAD, D), lambda i: (layer, 0, 0)),
                pl.BlockSpec((None, N_GROUPS_PAD, D), lambda i: (layer, 0, 1)),
                pl.BlockSpec((1, D), const),
                pl.BlockSpec((D, IN_W), const)]
    args = [x, mod, mod, g_pre.reshape(1, D), w_in_bf]
    out_specs = [pl.BlockSpec((TM, QKV_W), row), pl.BlockSpec((TM, FOUR_W), row)]
    out_shape = [jax.ShapeDtypeStruct((rows, QKV_W), BF16), jax.ShapeDtypeStruct((rows, FOUR_W), F32)]
    if rope:
        tiles_per_seq = T_LAT // TM
        tab = pl.BlockSpec((TM, HEAD_W), lambda i: (i % tiles_per_seq, 0))
        in_specs += [tab, tab]
        args += list(rope_tabs)
    else:
        nb = TM // T_CTX
        if cache is None:
            cache_spec = pl.BlockSpec((nb, DEPTH, T_CTX, ATTN_W), lambda i: (i, 0, 0, 0))
        else:
            cache_spec = pl.BlockSpec((nb, None, T_CTX, ATTN_W), lambda i: (i, layer, 0, 0))
            aliases = {len(args): 2, len(args) + 1: 3}
            in_specs += [pl.BlockSpec(memory_space=pl.ANY)] * 2
            args += list(cache)
        out_specs += [cache_spec] * 2
        out_shape += [jax.ShapeDtypeStruct((N_CTX_B, DEPTH, T_CTX, ATTN_W), F32)] * 2
    return pl.pallas_call(
        functools.partial(_inproj_body, rope=rope, cache_layer=None if rope else (0 if cache is None else layer),
                          group0=group0, tiles_per_group=tiles_per_group),
        grid=(rows // TM,),
        in_specs=in_specs, out_specs=out_specs, out_shape=out_shape,
        input_output_aliases=aliases,
        compiler_params=_cparams(("arbitrary",)),
        name="inproj_lat" if rope else "inproj_ctx",
    )(*args)


def _rope_tables():
    half = QK_DIM // 2
    inv = 1.0 / (ROPE_THETA ** (np.arange(0, half, 2, dtype=np.float64) / half))
    pos = np.arange(T_LAT)
    def tab(p):
        ang = p[:, None].astype(np.float64) * inv[None, :]
        return np.concatenate([ang, ang], axis=-1)
    ang = np.concatenate([tab(pos // GRID_W), tab(pos % GRID_W)], axis=-1)
    sign = np.where((np.arange(QK_DIM) % 32) < 16, -1.0, 1.0)
    cos = np.tile(np.cos(ang), (1, 2))
    sin = np.tile(np.sin(ang) * sign[None, :], (1, 2))
    return jnp.asarray(cos, F32), jnp.asarray(sin, F32)


def _lam(lam_ref, lam_init):
    lp = lam_ref[...]
    return (jnp.exp(jnp.sum(lp[0:1] * lp[1:2], keepdims=True))
            - jnp.exp(jnp.sum(lp[2:3] * lp[3:4], keepdims=True)) + lam_init)


def _stack_maps(q):
    lane = lax.broadcasted_iota(jnp.int32, q.shape, 1)
    zero = jnp.zeros_like(q)
    return jnp.concatenate([jnp.where(lane < QK_DIM, q, zero), jnp.where(lane >= QK_DIM, q, zero)], axis=0)


def _softmax_step(qq, kb, vb, carry):
    m, l, acc = carry
    s = lax.dot_general(qq, kb, (((1,), (1,)), ((), ())), preferred_element_type=F32)
    m_new = jnp.maximum(m, jnp.max(s, axis=-1, keepdims=True))
    alpha = jnp.exp2(m - m_new)
    p = jnp.exp2(s - m_new)
    l = alpha * l + jnp.sum(p, axis=-1, keepdims=True)
    acc = alpha * acc + jnp.dot(p.astype(BF16), vb, preferred_element_type=F32)
    return m_new, l, acc


def _softmax_init(rows):
    return (jnp.full((rows, 1), -jnp.inf, F32), jnp.zeros((rows, 1), F32), jnp.zeros((rows, HEAD_W), F32))


def _diff_out(carry, tq, lam, gs, lam_init):
    _, l, acc = carry
    o = acc / l
    a = o[:tq] - lam * o[tq:]
    return _rms(a, gs) * (1.0 - lam_init)


def _attn_ctx_body(lam_ref, gs_ref, q_ref, k_ref, v_ref, o_ref, *, lam_init):
    lam = _lam(lam_ref, lam_init)
    for h in range(N_HEADS):
        sl = slice(h * HEAD_W, (h + 1) * HEAD_W)
        carry = _softmax_step(_stack_maps(q_ref[:, sl]), k_ref[:, sl], v_ref[:, sl], _softmax_init(2 * T_CTX))
        o_ref[:, sl] = _diff_out(carry, T_CTX, lam, gs_ref[...], lam_init).astype(o_ref.dtype)


def _attn_ctx(qkv, lam_params, g_subln, layer, lam_init):
    qkv3 = qkv.reshape(N_CTX_B, T_CTX, QKV_W)
    blk = lambda part: pl.BlockSpec((None, T_CTX, ATTN_W), lambda b: (b, 0, part))
    return pl.pallas_call(
        functools.partial(_attn_ctx_body, lam_init=lam_init),
        grid=(N_CTX_B,),
        in_specs=[pl.BlockSpec((None, 4, QK_DIM), lambda b: (layer, 0, 0)),
                  pl.BlockSpec((None, 1, HEAD_W), lambda b: (layer, 0, 0)),
                  blk(0), blk(1), blk(2)],
        out_specs=pl.BlockSpec((None, T_CTX, ATTN_W), lambda b: (b, 0, 0)),
        out_shape=jax.ShapeDtypeStruct((N_CTX_B, T_CTX, ATTN_W), BF16),
        compiler_params=_cparams(("arbitrary",)),
        name="attn_ctx",
    )(lam_params, g_subln.reshape(DEPTH, 1, HEAD_W), qkv3, qkv3, qkv3)


def _attn_lat_body(lam_ref, gs_ref, q_ref, kc_ref, vc_ref, kl_ref, vl_ref, o_ref, kcat, vcat, *, lam_init):
    @pl.when(pl.program_id(2) == 0)
    def _():
        kcat[0:T_PAST, :] = kc_ref[...]
        kcat[T_PAST:, :] = kl_ref[...]
        vcat[0:T_PAST, :] = vc_ref[...]
        vcat[T_PAST:, :] = vl_ref[...]

    qq = _stack_maps(q_ref[...])

    def body(c, carry):
        start = pl.multiple_of(c * TK, TK)
        return _softmax_step(qq, kcat[pl.ds(start, TK), :], vcat[pl.ds(start, TK), :], carry)

    carry = lax.fori_loop(0, (T_PAST + T_LAT) // TK, body, _softmax_init(2 * TQ), unroll=ATTN_UNROLL)
    o_ref[...] = _diff_out(carry, TQ, _lam(lam_ref, lam_init), gs_ref[...], lam_init).astype(o_ref.dtype)


def _attn_lat(qkv, ck, cv, lam_params, g_subln, layer, lam_init):
    qkv3 = qkv.reshape(N_LAT_B, T_LAT, QKV_W)
    full = lambda off: pl.BlockSpec((None, T_LAT, HEAD_W), lambda b, h, i: (b, 0, off + h))
    past = pl.BlockSpec((None, T_PAST, HEAD_W), lambda b, h, i: (b, 0, h))
    return pl.pallas_call(
        functools.partial(_attn_lat_body, lam_init=lam_init),
        grid=(N_LAT_B, N_HEADS, T_LAT // TQ),
        in_specs=[pl.BlockSpec((None, 4, QK_DIM), lambda b, h, i: (layer, 0, 0)),
                  pl.BlockSpec((None, 1, HEAD_W), lambda b, h, i: (layer, 0, 0)),
                  pl.BlockSpec((None, TQ, HEAD_W), lambda b, h, i: (b, i, h)),
                  past, past, full(N_HEADS), full(2 * N_HEADS)],
        out_specs=pl.BlockSpec((None, TQ, HEAD_W), lambda b, h, i: (b, i, h)),
        out_shape=jax.ShapeDtypeStruct((N_LAT_B, T_LAT, ATTN_W), BF16),
        scratch_shapes=[pltpu.VMEM((T_PAST + T_LAT, HEAD_W), BF16)] * 2,
        compiler_params=_cparams(("arbitrary", "arbitrary", "arbitrary")),
        name="attn_lat",
    )(lam_params, g_subln.reshape(DEPTH, 1, HEAD_W), qkv3, ck, cv, qkv3, qkv3)


def _fold_groups(ur, ui, wf_ref, scale):
    outs = []
    for g in range(F_GROUPS):
        sl = slice(g * F_CH, (g + 1) * F_CH)
        lhs = jnp.concatenate([ur[:, sl], ui[:, sl]], axis=1).astype(BF16)
        outs.append(jnp.dot(lhs, wf_ref[g], preferred_element_type=F32))
    return jnp.concatenate(outs, axis=1) * scale


def _four_ctx_body(f_ref, dft_ref, wf_ref, o_ref):
    u = jnp.dot(dft_ref[...], f_ref[...].astype(BF16), preferred_element_type=F32)
    o_ref[...] = _fold_groups(u[:T_CTX], u[T_CTX:], wf_ref, 1.0 / math.sqrt(T_CTX * F_CH)).astype(o_ref.dtype)


def _four_ctx(f, wfold_bf, layer):
    ang = 2.0 * np.pi * np.outer(np.arange(T_CTX), np.arange(T_CTX)) / T_CTX
    dft = jnp.asarray(np.concatenate([np.cos(ang), -np.sin(ang)], axis=0), F32).astype(BF16)
    return pl.pallas_call(
        _four_ctx_body,
        grid=(N_CTX_B,),
        in_specs=[pl.BlockSpec((None, T_CTX, FOUR_W), lambda b: (b, 0, 0)),
                  pl.BlockSpec((2 * T_CTX, T_CTX), lambda b: (0, 0)),
                  pl.BlockSpec((None, F_GROUPS, 2 * F_CH, F_CH), lambda b: (layer, 0, 0, 0))],
        out_specs=pl.BlockSpec((None, T_CTX, FOUR_W), lambda b: (b, 0, 0)),
        out_shape=jax.ShapeDtypeStruct((N_CTX_B, T_CTX, FOUR_W), BF16),
        compiler_params=_cparams(("arbitrary",)),
        name="fourier_ctx",
    )(f.reshape(N_CTX_B, T_CTX, FOUR_W), dft, wfold_bf)


def _fft_body(x_ref, ma_ref, twc_ref, tws_ref, mc_ref, ms_ref, wf_ref, o_ref, h_ref):
    nb = FFT_R * FFT_S
    nj = FFT_R // FFT_S
    j = pl.program_id(1)

    @pl.when(j < nj)
    def _():
        x = x_ref[...].reshape(nb, FOUR_W).astype(BF16)
        g = jnp.dot(ma_ref[...], x, preferred_element_type=F32)
        gr, gi = g[:nb], g[nb:]
        twc = jnp.tile(twc_ref[...], (1, FOUR_W // 128))
        tws = jnp.tile(tws_ref[...], (1, FOUR_W // 128))
        lo = pl.ds(pl.multiple_of(j * FFT_S, FFT_S), FFT_S)
        h_ref[lo, :, 0:FOUR_W] = (gr * twc + gi * tws).reshape(FFT_S, FFT_R, FOUR_W)
        h_ref[lo, :, FOUR_W:] = (gi * twc - gr * tws).reshape(FFT_S, FFT_R, FOUR_W)

    @pl.when(j >= nj)
    def _():
        hi = pl.ds(pl.multiple_of((j - nj) * FFT_S, FFT_S), FFT_S)
        h = h_ref[:, hi, :].reshape(nb, 2 * FOUR_W).astype(BF16)
        p = jnp.dot(mc_ref[...], h, preferred_element_type=F32)
        q = jnp.dot(ms_ref[...], h, preferred_element_type=F32)
        ur = p[:, :FOUR_W] + q[:, FOUR_W:]
        ui = p[:, FOUR_W:] - q[:, :FOUR_W]
        out = _fold_groups(ur, ui, wf_ref, 1.0 / math.sqrt(T_LAT * F_CH))
        o_ref[...] = out.reshape(FFT_R, FFT_S, FOUR_W)


def _fft_consts():
    r, s = FFT_R, FFT_S
    nb = r * s
    ang = 2.0 * np.pi * np.outer(np.arange(r), np.arange(r)) / r
    c, sn = np.cos(ang), np.sin(ang)
    eye = np.eye(s)
    ma = np.concatenate([np.einsum('pb,ts->tpbs', c, eye).reshape(nb, nb),
                         np.einsum('pb,ts->tpbs', -sn, eye).reshape(nb, nb)], axis=0)
    mbc = np.einsum('pa,ts->ptas', c, eye).reshape(nb, nb)
    mbs = np.einsum('pa,ts->ptas', sn, eye).reshape(nb, nb)
    tw = 2.0 * np.pi * np.outer(np.arange(r), np.arange(r)).reshape(-1) / (r * r)
    twc = np.broadcast_to(np.cos(tw)[:, None], (r * r, 128))
    tws = np.broadcast_to(np.sin(tw)[:, None], (r * r, 128))
    bf = lambda a: jnp.asarray(a, F32).astype(BF16)
    return bf(ma), bf(mbc), bf(mbs), jnp.asarray(twc, F32), jnp.asarray(tws, F32)


def _four_lat(f, wfold_bf, layer, consts):
    ma, mbc, mbs, twc, tws = consts
    r, s = FFT_R, FFT_S
    nb = r * s
    nj = r // s
    f4 = f.reshape(N_LAT_B, r, r, FOUR_W)
    a_step = lambda j: jnp.minimum(j, nj - 1)
    b_step = lambda j: jnp.maximum(j - nj, 0)
    const = lambda b, j: (0, 0)
    out = pl.pallas_call(
        _fft_body,
        grid=(N_LAT_B, 2 * nj),
        in_specs=[pl.BlockSpec((None, r, s, FOUR_W), lambda b, j: (b, 0, a_step(j), 0)),
                  pl.BlockSpec((2 * nb, nb), const),
                  pl.BlockSpec((nb, 128), lambda b, j: (a_step(j), 0)),
                  pl.BlockSpec((nb, 128), lambda b, j: (a_step(j), 0)),
                  pl.BlockSpec((nb, nb), const),
                  pl.BlockSpec((nb, nb), const),
                  pl.BlockSpec((None, F_GROUPS, 2 * F_CH, F_CH), lambda b, j: (layer, 0, 0, 0))],
        out_specs=pl.BlockSpec((None, r, s, FOUR_W), lambda b, j: (b, 0, b_step(j), 0)),
        out_shape=jax.ShapeDtypeStruct((N_LAT_B, r, r, FOUR_W), F32),
        scratch_shapes=[pltpu.VMEM((r, r, 2 * FOUR_W), F32)],
        compiler_params=_cparams(("arbitrary", "arbitrary")),
        name="fft_lat",
    )(f4, ma, twc, tws, mbc, mbs, wfold_bf)
    return out.reshape(N_LAT_B * T_LAT, FOUR_W)


def _outproj_body(a_ref, f_ref, x_ref, gt_ref, sh_ref, sc_ref, gpost_ref, gpre_ref, wo_ref, *rest,
                  moe, group0, tiles_per_group):
    if moe:
        wr_ref, x1_ref, hn_ref, gates_ref, sel_ref = rest
    else:
        x1_ref, hn_ref = rest
    grp = _group_of(pl.program_id(0), group0, tiles_per_group)
    gt, sh, sc = gt_ref[pl.ds(grp, 1), :], sh_ref[pl.ds(grp, 1), :], sc_ref[pl.ds(grp, 1), :]
    for half in range(TM_OUT // TM):
        rows = slice(half * TM, (half + 1) * TM)
        mix_in = jnp.concatenate([a_ref[rows, :], f_ref[rows, :].astype(BF16)], axis=1)
        mixed = jnp.dot(mix_in, wo_ref[...], preferred_element_type=F32)
        x1 = x_ref[rows, :] + gt * _rms(mixed, gpost_ref[...])
        x1_ref[rows, :] = x1
        hn = _rms(x1, gpre_ref[...]) * (1.0 + sc) + sh
        hn_ref[rows, :] = hn.astype(hn_ref.dtype)
        if moe:
            gates, sel = _route_top2(hn, wr_ref[...])
            gates_ref[rows, :] = gates
            sel_ref[rows, :] = sel.astype(sel_ref.dtype)


def _route_top2(hn, wr):
    wr_hi = wr.astype(BF16)
    wr_lo = (wr - wr_hi.astype(F32)).astype(BF16)
    hn_hi = hn.astype(BF16)
    hn_lo = (hn - hn_hi.astype(F32)).astype(BF16)
    hh = jnp.dot(hn_hi, jnp.concatenate([wr_hi, wr_lo], axis=1), preferred_element_type=F32)
    logits = hh[:, :E_PAD] + hh[:, E_PAD:] + jnp.dot(hn_lo, wr_hi, preferred_element_type=F32)
    lane = lax.broadcasted_iota(jnp.int32, logits.shape, 1).astype(F32)
    neg = jnp.float32(-jnp.inf)
    logits = jnp.where(lane < float(N_EXPERTS), logits, neg)
    m1 = jnp.max(logits, axis=-1, keepdims=True)
    i1 = jnp.min(jnp.where(logits == m1, lane, float(E_PAD)), axis=-1, keepdims=True)
    rest_l = jnp.where(lane == i1, neg, logits)
    m2 = jnp.max(rest_l, axis=-1, keepdims=True)
    i2 = jnp.min(jnp.where(rest_l == m2, lane, float(E_PAD)), axis=-1, keepdims=True)
    e2 = jnp.exp(m2 - m1)
    w1 = 1.0 / (1.0 + e2)
    w2 = e2 / (1.0 + e2)
    gates = jnp.where(lane == i1, w1, 0.0) + jnp.where(lane == i2, w2, 0.0)
    sel = jnp.where((lane == i1) | (lane == i2), 1.0, 0.0)
    return gates, sel


def _outproj(attn_o, four_o, x, mod, layer, g_post, g_ffn_pre, w_out_bf, w_router_pad, group0, tiles_per_group):
    rows = x.shape[0]
    moe = w_router_pad is not None
    row = lambda i: (i, 0)
    const = lambda i: (0, 0)
    modspec = lambda k: pl.BlockSpec((None, N_GROUPS_PAD, D), lambda i: (layer, 0, k))
    blocks_per_group = None if tiles_per_group is None else tiles_per_group * TM // TM_OUT
    in_specs = [pl.BlockSpec((TM_OUT, ATTN_W), row), pl.BlockSpec((TM_OUT, FOUR_W), row),
                pl.BlockSpec((TM_OUT, D), row),
                modspec(2), modspec(3), modspec(4),
                pl.BlockSpec((1, D), const), pl.BlockSpec((1, D), const), pl.BlockSpec((D, D), const)]
    args = [attn_o, four_o, x, mod, mod, mod, g_post.reshape(1, D), g_ffn_pre.reshape(1, D), w_out_bf]
    out_specs = [pl.BlockSpec((TM_OUT, D), row), pl.BlockSpec((TM_OUT, D), row)]
    out_shape = [jax.ShapeDtypeStruct((rows, D), F32), jax.ShapeDtypeStruct((rows, D), F32 if moe else BF16)]
    if moe:
        in_specs.append(pl.BlockSpec((D, E_PAD), const))
        args.append(w_router_pad)
        out_specs += [pl.BlockSpec((TM_OUT, E_PAD), row)] * 2
        out_shape += [jax.ShapeDtypeStruct((rows, E_PAD), F32), jax.ShapeDtypeStruct((rows, E_PAD), BF16)]
    return pl.pallas_call(
        functools.partial(_outproj_body, moe=moe, group0=group0, tiles_per_group=blocks_per_group),
        grid=(rows // TM_OUT,),
        in_specs=in_specs, out_specs=out_specs, out_shape=out_shape,
        compiler_params=_cparams(("arbitrary",)),
        name="outproj_moe" if moe else "outproj",
    )(*args)


def _swiglu(hn, wg, wu, wd):
    g = jnp.dot(hn, wg, preferred_element_type=F32)
    u = jnp.dot(hn, wu, preferred_element_type=F32)
    return jnp.dot((_silu(g) * u).astype(BF16), wd, preferred_element_type=F32)


def _ffn_dense_body(hn_ref, x1_ref, gt_ref, gpost_ref, wg_ref, wu_ref, wd_ref, o_ref, *, group0, tiles_per_group):
    grp = _group_of(pl.program_id(0), group0, tiles_per_group)
    ff = _swiglu(hn_ref[...], wg_ref[...], wu_ref[...], wd_ref[...])
    o_ref[...] = x1_ref[...] + gt_ref[pl.ds(grp, 1), :] * _rms(ff, gpost_ref[...])


def _ffn_dense(hn, x1, mod, layer, g_post, wg, wu, wd, group0, tiles_per_group):
    rows = x1.shape[0]
    row = lambda i: (i, 0)
    const = lambda i: (0, 0)
    resident = lambda shape: pl.BlockSpec(shape, const, pipeline_mode=pl.Buffered(1))
    return pl.pallas_call(
        functools.partial(_ffn_dense_body, group0=group0, tiles_per_group=tiles_per_group),
        grid=(rows // TM,),
        in_specs=[pl.BlockSpec((TM, D), row), pl.BlockSpec((TM, D), row),
                  pl.BlockSpec((None, N_GROUPS_PAD, D), lambda i: (layer, 0, 5)),
                  pl.BlockSpec((1, D), const),
                  resident((D, D_FF)), resident((D, D_FF)), resident((D_FF, D))],
        out_specs=pl.BlockSpec((TM, D), row),
        out_shape=jax.ShapeDtypeStruct((rows, D), F32),
        compiler_params=_cparams(("arbitrary",)),
        name="ffn_dense",
    )(hn, x1, mod, g_post.reshape(1, D), wg, wu, wd)


def _route_body(gates_ref, sel_ref, ltri_ref, slot_ref, meta_ref, cnt_ref, off_ref, run_ref):
    phase, t = pl.program_id(0), pl.program_id(1)
    sel = sel_ref[...]

    @pl.when((phase == 0) & (t == 0))
    def _():
        cnt_ref[...] = jnp.zeros_like(cnt_ref)

    @pl.when(phase == 0)
    def _():
        cnt_ref[...] += jnp.sum(sel.astype(F32), axis=0, keepdims=True)

    @pl.when((phase == 1) & (t == 0))
    def _():
        cnt = cnt_ref[...]
        padded = jnp.ceil(cnt * (1.0 / TMOE)) * TMOE
        lane = lax.broadcasted_iota(jnp.int32, cnt.shape, 1).astype(F32)
        start = lane * TMOE
        off = jnp.zeros_like(cnt)
        tile_e = jnp.zeros_like(cnt)
        tile_n = jnp.zeros_like(cnt)
        running = jnp.zeros((1, 1), F32)
        for e in range(N_EXPERTS):
            pick = lane == float(e)
            cnt_e = jnp.sum(jnp.where(pick, cnt, 0.0), axis=-1, keepdims=True)
            pad_e = jnp.sum(jnp.where(pick, padded, 0.0), axis=-1, keepdims=True)
            off = jnp.where(pick, running, off)
            in_seg = (start >= running) & (start < running + pad_e)
            tile_e = jnp.where(in_seg, float(e), tile_e)
            tile_n = jnp.where(in_seg, jnp.clip(cnt_e - (start - running), 0.0, float(TMOE)), tile_n)
            running = running + pad_e
        tile_e = jnp.where(start >= running, float(N_EXPERTS - 1), tile_e)
        off_ref[...] = off
        run_ref[...] = jnp.zeros_like(run_ref)
        row = lax.broadcasted_iota(jnp.int32, meta_ref.shape, 0)
        last_tile = jnp.where(lane < float(N_EXPERTS), off + padded - TMOE, running)
        last_tile = jnp.where((lane < float(N_EXPERTS)) & (padded == 0.0), -1.0, last_tile)
        meta_ref[...] = jnp.where(row == 0, tile_e, jnp.where(row == 1, tile_n,
                                  jnp.where(row == 2, last_tile, 0.0)))

    @pl.when(phase == 1)
    def _():
        earlier = jnp.dot(ltri_ref[...], sel, preferred_element_type=F32)
        pos = earlier + run_ref[...] + off_ref[...]
        run_ref[...] += jnp.sum(sel.astype(F32), axis=0, keepdims=True)
        chosen = sel > 0
        s_lo = jnp.min(jnp.where(chosen, pos, float(NP)), axis=-1, keepdims=True)
        s_hi = jnp.max(jnp.where(chosen, pos, -1.0), axis=-1, keepdims=True)
        g = gates_ref[...]
        w_lo = jnp.sum(jnp.where(chosen & (pos == s_lo), g, 0.0), axis=-1, keepdims=True)
        w_hi = jnp.sum(jnp.where(chosen & (pos == s_hi), g, 0.0), axis=-1, keepdims=True)
        lane = lax.broadcasted_iota(jnp.int32, pos.shape, 1)
        slot_ref[...] = jnp.where(lane == 0, s_lo, jnp.where(lane == 1, s_hi,
                                  jnp.where(lane == 2, w_lo, jnp.where(lane == 3, w_hi, 0.0))))


def _route(gates, sel):
    rows = gates.shape[0]
    n = rows // TM
    ltri = jnp.asarray(np.tril(np.ones((TM, TM)), -1), F32).astype(BF16)
    tile = lambda p, t: (t, 0)
    return pl.pallas_call(
        _route_body,
        grid=(2, n),
        in_specs=[pl.BlockSpec((TM, E_PAD), tile), pl.BlockSpec((TM, E_PAD), tile),
                  pl.BlockSpec((TM, TM), lambda p, t: (0, 0))],
        out_specs=[pl.BlockSpec((TM, E_PAD), lambda p, t: (t * p, 0)),
                   pl.BlockSpec((N_GROUPS_PAD, E_PAD), lambda p, t: (0, 0))],
        out_shape=[jax.ShapeDtypeStruct((rows, E_PAD), F32), jax.ShapeDtypeStruct((N_GROUPS_PAD, E_PAD), F32)],
        scratch_shapes=[pltpu.VMEM((1, E_PAD), F32)] * 3,
        compiler_params=_cparams(("arbitrary", "arbitrary")),
        name="moe_route",
    )(gates, sel, ltri)


def _row_copies(slot_ref, src_of, dst_of, sem, *, start=True, wait=True):
    sub = 8

    def copy(g, j, k):
        r = pl.multiple_of(g * sub, sub) + j
        s = slot_ref[0, 2 * r + k]
        return pltpu.make_async_copy(src_of(r, k, s), dst_of(r, k, s), sem)

    def each(act):
        def trip(g, carry):
            for j in range(sub):
                for k in range(TOP_K):
                    act(copy(g, j, k))
            return carry
        lax.fori_loop(0, TM // sub, trip, 0)

    if start:
        each(lambda c: c.start())
    if wait:
        each(lambda c: c.wait())


def _dispatch_body(slot_ref, pad_ref, *rest, tiles):
    hn_refs, (xs_ref, zeros, sem) = rest[:len(tiles)], rest[len(tiles):]
    i = pl.program_id(0)

    @pl.when(i == 0)
    def _():
        zeros[...] = jnp.zeros_like(zeros)
        fills = []
        for e in range(N_EXPERTS):
            s = pad_ref[0, e]
            fills.append((s >= 0, pl.multiple_of(jnp.maximum(s, 0), TMOE)))
        for j in range(N_EXPERTS):
            s = NP - (j + 1) * TMOE
            fills.append((s >= pad_ref[0, N_EXPERTS], s))
        fill = lambda s: pltpu.make_async_copy(zeros, xs_ref.at[pl.ds(s, TMOE), :], sem)
        for live, s in fills:
            @pl.when(live)
            def _(s=s):
                fill(s).start()
        for live, s in fills:
            @pl.when(live)
            def _(s=s):
                fill(s).wait()

    first = 0
    for hn_ref, n in zip(hn_refs, tiles):
        @pl.when((i >= first) & (i < first + n))
        def _(hn_ref=hn_ref):
            _row_copies(slot_ref,
                        lambda r, k, s: hn_ref.at[pl.ds(r, 1), :],
                        lambda r, k, s: xs_ref.at[pl.ds(s, 1), :], sem)
        first += n


def _dispatch(hns, slots, pad_starts):
    tiles = [h.shape[0] // TM for h in hns]
    firsts = np.cumsum([0] + tiles)
    n = int(firsts[-1])
    stream_spec = lambda lo, cnt: pl.BlockSpec((TM, D), lambda i: (jnp.clip(i - lo, 0, cnt - 1), 0))
    return pl.pallas_call(
        functools.partial(_dispatch_body, tiles=tuple(tiles)),
        grid=(n,),
        in_specs=[pl.BlockSpec((None, 1, 2 * TM), lambda i: (i, 0, 0), memory_space=pltpu.SMEM),
                  pl.BlockSpec(memory_space=pltpu.SMEM)]
                 + [stream_spec(int(lo), cnt) for lo, cnt in zip(firsts[:-1], tiles)],
        out_specs=pl.BlockSpec(memory_space=pl.ANY),
        out_shape=jax.ShapeDtypeStruct((NP, D), F32),
        scratch_shapes=[pltpu.VMEM((TMOE, D), F32), pltpu.SemaphoreType.DMA],
        compiler_params=_cparams(("arbitrary",)),
        name="moe_dispatch",
    )(slots.reshape(n, 1, 2 * TM), pad_starts, *hns)


def _experts_body(te_ref, tn_ref, xs_ref, wg_ref, wu_ref, wd_ref, ys_ref, wg_bf, wu_bf, wd_bf):
    i = pl.program_id(0)
    fresh = (i == 0) | (te_ref[i] != te_ref[jnp.maximum(i - 1, 0)])

    @pl.when(fresh)
    def _():
        wg_bf[...] = wg_ref[...].astype(BF16)
        wu_bf[...] = wu_ref[...].astype(BF16)
        wd_bf[...] = wd_ref[...].astype(BF16)

    n_real = tn_ref[i]

    @pl.when(n_real > 0)
    def _():
        ys_ref[...] = _swiglu(xs_ref[...].astype(BF16), wg_bf[...], wu_bf[...], wd_bf[...])

    @pl.when(n_real == 0)
    def _():
        ys_ref[...] = jnp.zeros_like(ys_ref)


def _experts(xs, tile_e, tile_n, eg, eu, ed):
    tile = lambda i, te, tn: (i, 0)
    wspec = lambda shape: pl.BlockSpec((None,) + shape, lambda i, te, tn: (te[i], 0, 0))
    return pl.pallas_call(
        _experts_body,
        grid_spec=pltpu.PrefetchScalarGridSpec(
            num_scalar_prefetch=2,
            grid=(N_TILES,),
            in_specs=[pl.BlockSpec((TMOE, D), tile), wspec((D, D_FF_E)), wspec((D, D_FF_E)), wspec((D_FF_E, D))],
            out_specs=pl.BlockSpec((TMOE, D), tile),
            scratch_shapes=[pltpu.VMEM((D, D_FF_E), BF16), pltpu.VMEM((D, D_FF_E), BF16),
                            pltpu.VMEM((D_FF_E, D), BF16)]),
        out_shape=jax.ShapeDtypeStruct((NP, D), F32),
        compiler_params=_cparams(("arbitrary",)),
        name="moe_experts",
    )(tile_e, tile_n, xs, eg, eu, ed)


def _combine_body(slot_ref, next_slot_ref, w_ref, x1_ref, gt_ref, gpost_ref, ys_ref, o_ref, ybuf, sems,
                  *, group0, tiles_per_group):
    i = pl.program_id(0)
    grp = _group_of(i, group0, tiles_per_group)
    cur = lax.rem(i, 2)
    src = lambda r, k, s: ys_ref.at[pl.ds(s, 1), :]
    dst = lambda half: (lambda r, k, s: ybuf.at[half, k, pl.ds(r, 1), :])

    @pl.when(i == 0)
    def _():
        _row_copies(slot_ref, src, dst(0), sems.at[0], wait=False)

    @pl.when(i + 1 < pl.num_programs(0))
    def _():
        _row_copies(next_slot_ref, src, dst(1 - cur), sems.at[1 - cur], wait=False)

    _row_copies(slot_ref, src, dst(cur), sems.at[cur], start=False)
    w = w_ref[...]
    lane = lax.broadcasted_iota(jnp.int32, w.shape, 1)
    w_lo = jnp.sum(jnp.where(lane == 2, w, 0.0), axis=-1, keepdims=True)
    w_hi = jnp.sum(jnp.where(lane == 3, w, 0.0), axis=-1, keepdims=True)
    ff = w_lo * ybuf[cur, 0] + w_hi * ybuf[cur, 1]
    o_ref[...] = x1_ref[...] + gt_ref[pl.ds(grp, 1), :] * _rms(ff, gpost_ref[...])


def _combine(ys, slots, slot_w, x1, mod, layer, g_post, group0, tiles_per_group):
    rows = x1.shape[0]
    n = rows // TM
    row = lambda i: (i, 0)
    slots3 = slots.reshape(n, 1, 2 * TM)
    return pl.pallas_call(
        functools.partial(_combine_body, group0=group0, tiles_per_group=tiles_per_group),
        grid=(n,),
        in_specs=[pl.BlockSpec((None, 1, 2 * TM), lambda i: (i, 0, 0), memory_space=pltpu.SMEM),
                  pl.BlockSpec((None, 1, 2 * TM), lambda i: (jnp.minimum(i + 1, n - 1), 0, 0),
                               memory_space=pltpu.SMEM),
                  pl.BlockSpec((TM, E_PAD), row), pl.BlockSpec((TM, D), row),
                  pl.BlockSpec((None, N_GROUPS_PAD, D), lambda i: (layer, 0, 5)),
                  pl.BlockSpec((1, D), lambda i: (0, 0)),
                  pl.BlockSpec(memory_space=pl.ANY)],
        out_specs=pl.BlockSpec((TM, D), row),
        out_shape=jax.ShapeDtypeStruct((rows, D), F32),
        scratch_shapes=[pltpu.VMEM((2, TOP_K, TM, D), F32), pltpu.SemaphoreType.DMA((2,))],
        compiler_params=_cparams(("arbitrary",)),
        name="moe_combine",
    )(slots3, slots3, slot_w, x1, mod, g_post.reshape(1, D), ys)


def kernel(x_prompt, x_sample, cache_k, cache_v, c, c_ctx, g_attn_pre, g_attn_post, g_ffn_pre, g_ffn_post,
           w_ada, b_ada, w_in, lam_params, g_subln, w_fnet, w_out, w_gate, w_up, w_down, w_router,
           e_gate, e_up, e_down):
    assert x_prompt.shape == (N_CTX_B, T_CTX, D) and x_sample.shape == (N_LAT_B, T_LAT, D)
    assert cache_k.shape == (N_LAT_B, DEPTH, T_PAST, N_HEADS, 2, QK_DIM)

    cond = jnp.zeros((N_GROUPS_PAD, D), F32).at[0].set(c_ctx).at[1:1 + N_LAT_B].set(c)
    mod = _ada(cond, w_ada, b_ada)
    wfold_bf = _wfold(w_fnet).astype(BF16)
    rope_tabs = _rope_tables()
    fft_consts = _fft_consts()

    w_in_bf = w_in.astype(BF16)
    w_out_bf = w_out.astype(BF16)
    w_router_pad = jnp.zeros((DEPTH // 2, D, E_PAD), F32).at[:, :, :N_EXPERTS].set(w_router)
    ck_bf = cache_k.reshape(N_LAT_B, DEPTH, T_PAST, ATTN_W).astype(BF16)
    cv_bf = cache_v.reshape(N_LAT_B, DEPTH, T_PAST, ATTN_W).astype(BF16)

    lat_tiles = T_LAT // TM
    streams = [dict(x=x_prompt.reshape(N_CTX_B * T_CTX, D), group0=0, tiles=None, lat=False),
               dict(x=x_sample.reshape(N_LAT_B * T_LAT, D), group0=1, tiles=lat_tiles, lat=True)]
    new_k = new_v = None
    for l in range(DEPTH):
        lam_init = 0.8 - 0.6 * math.exp(-0.3 * l)
        i = l // 2
        routed = []
        for s in streams:
            g0, tiles = s["group0"], s["tiles"]
            if s["lat"]:
                qkv, f = _inproj(s["x"], mod, l, g_attn_pre[l], w_in_bf[l], rope_tabs, g0, tiles)
                attn_o = _attn_lat(qkv, ck_bf[:, l], cv_bf[:, l], lam_params, g_subln, l, lam_init)
                attn_o = attn_o.reshape(N_LAT_B * T_LAT, ATTN_W)
                four_o = _four_lat(f, wfold_bf, l, fft_consts)
            else:
                qkv, f, new_k, new_v = _inproj(s["x"], mod, l, g_attn_pre[l], w_in_bf[l], None, g0, tiles,
                                               cache=None if l == 0 else (new_k, new_v))
                attn_o = _attn_ctx(qkv, lam_params, g_subln, l, lam_init).reshape(N_CTX_B * T_CTX, ATTN_W)
                four_o = _four_ctx(f, wfold_bf, l).reshape(N_CTX_B * T_CTX, FOUR_W)
            if l % 2 == 0:
                x1, hn = _outproj(attn_o, four_o, s["x"], mod, l, g_attn_post[l], g_ffn_pre[l], w_out_bf[l],
                                  None, g0, tiles)
                s["x"] = _ffn_dense(hn, x1, mod, l, g_ffn_post[l], w_gate[i].astype(BF16), w_up[i].astype(BF16),
                                    w_down[i].astype(BF16), g0, tiles)
            else:
                routed.append(_outproj(attn_o, four_o, s["x"], mod, l, g_attn_post[l], g_ffn_pre[l],
                                       w_out_bf[l], w_router_pad[i], g0, tiles))
        if l % 2 == 1:
            slot_w, meta = _route(jnp.concatenate([r[2] for r in routed]), jnp.concatenate([r[3] for r in routed]))
            slots = slot_w[:, :2].astype(jnp.int32)
            tile_e = meta[0, :N_TILES].astype(jnp.int32)
            tile_n = meta[1, :N_TILES].astype(jnp.int32)
            pad_starts = meta[2:3, :N_EXPERTS + 1].astype(jnp.int32)
            bounds = np.cumsum([0] + [r[0].shape[0] for r in routed])
            xs = _dispatch([r[1] for r in routed], slots, pad_starts)
            ys = _experts(xs, tile_e, tile_n, e_gate[i], e_up[i], e_down[i])
            for s, r, lo, hi in zip(streams, routed, bounds[:-1], bounds[1:]):
                s["x"] = _combine(ys, slots[lo:hi], slot_w[lo:hi], r[0], mod, l, g_ffn_post[l],
                                  s["group0"], s["tiles"])

    y_prompt = streams[0]["x"].reshape(N_CTX_B, T_CTX, D)
    y_sample = streams[1]["x"].reshape(N_LAT_B, T_LAT, D)
    new_cache_k = new_k.reshape(N_CTX_B, DEPTH, T_CTX, N_HEADS, 2, QK_DIM)
    new_cache_v = new_v.reshape(N_CTX_B, DEPTH, T_CTX, N_HEADS, HEAD_W)
    return (y_prompt, y_sample, new_cache_k, new_cache_v)
```

```python
import functools
import math

import numpy as np
import jax
import jax.numpy as jnp
from jax import lax
from jax.experimental import pallas as pl
from jax.experimental.pallas import tpu as pltpu

F32 = jnp.float32
BF16 = jnp.bfloat16

D = 1024
N_CTX_B = 16
T_CTX = 256
N_LAT_B = 2
T_LAT = 4096
T_PAST = 512
DEPTH = 2
GRID_W = 64
N_HEADS = 4
QK_DIM = 64
HEAD_W = 2 * QK_DIM
ATTN_W = N_HEADS * HEAD_W
F_GROUPS = 4
F_CH = 128
FOUR_W = F_GROUPS * F_CH
IN_W = 3 * ATTN_W + FOUR_W
QKV_W = 3 * ATTN_W
D_FF = 2816
N_EXPERTS = 8
D_FF_E = 1024
N_MOD = 6
EPS = 1e-6
ROPE_THETA = 10000.0
N_GROUPS_PAD = 8
E_PAD = 128

TM = 512
TM_OUT = 1024
TOP_K = 2
TMOE = 512
N_TOKENS = N_CTX_B * T_CTX + N_LAT_B * T_LAT
NP = TOP_K * N_TOKENS + N_EXPERTS * TMOE
N_TILES = NP // TMOE
TQ = 512
Q_SUB = 2
TK = 1536
ATTN_UNROLL = 3
FFT_R = 64
FFT_S = 8
VMEM_LIMIT = 56 * 1024 * 1024


def _cparams(sem):
    return pltpu.CompilerParams(dimension_semantics=sem, vmem_limit_bytes=VMEM_LIMIT)


def _rms(x, g):
    ms = jnp.mean(x * x, axis=-1, keepdims=True)
    return x * lax.rsqrt(ms + EPS) * g


def _silu(x):
    return x * (1.0 / (1.0 + jnp.exp(-x)))


def _ada_body(c_ref, w_ref, b_ref, o_ref):
    s = _silu(c_ref[...])
    o_ref[...] = jnp.dot(s, w_ref[...], preferred_element_type=F32,
                         precision=lax.Precision.HIGHEST) + b_ref[...]


def _ada(cond, w_ada, b_ada):
    tn = 1536
    return pl.pallas_call(
        _ada_body,
        grid=(DEPTH, N_MOD * D // tn),
        in_specs=[pl.BlockSpec((N_GROUPS_PAD, D), lambda l, j: (0, 0)),
                  pl.BlockSpec((None, D, tn), lambda l, j: (l, 0, j)),
                  pl.BlockSpec((None, 1, tn), lambda l, j: (l, 0, j))],
        out_specs=pl.BlockSpec((None, N_GROUPS_PAD, tn), lambda l, j: (l, 0, j)),
        out_shape=jax.ShapeDtypeStruct((DEPTH, N_GROUPS_PAD, N_MOD * D), F32),
        compiler_params=_cparams(("arbitrary", "arbitrary")),
        name="ada_mod",
    )(cond, w_ada, b_ada.reshape(DEPTH, 1, N_MOD * D))


def _wfold_body(cc_ref, sc_ref, w_ref, o_ref):
    w = w_ref[...]
    o_ref[0:F_CH, :] = jnp.dot(cc_ref[...], w, preferred_element_type=F32,
                               precision=lax.Precision.HIGHEST)
    o_ref[F_CH:2 * F_CH, :] = jnp.dot(sc_ref[...], w, preferred_element_type=F32,
                                      precision=lax.Precision.HIGHEST)


def _wfold(w_fnet):
    ang = 2.0 * np.pi * np.outer(np.arange(F_CH), np.arange(F_CH)) / F_CH
    cc = jnp.asarray(np.cos(ang), F32)
    sc = jnp.asarray(np.sin(ang), F32)
    cspec = pl.BlockSpec((F_CH, F_CH), lambda l, g: (0, 0))
    return pl.pallas_call(
        _wfold_body,
        grid=(DEPTH, F_GROUPS),
        in_specs=[cspec, cspec, pl.BlockSpec((None, None, F_CH, F_CH), lambda l, g: (l, g, 0, 0))],
        out_specs=pl.BlockSpec((None, None, 2 * F_CH, F_CH), lambda l, g: (l, g, 0, 0)),
        out_shape=jax.ShapeDtypeStruct((DEPTH, F_GROUPS, 2 * F_CH, F_CH), F32),
        compiler_params=_cparams(("arbitrary", "arbitrary")),
        name="fnet_fold",
    )(cc, sc, w_fnet)


def _group_of(i, group0, tiles_per_group):
    if tiles_per_group is None:
        return group0
    return group0 + lax.div(i, jnp.int32(tiles_per_group))


def _inproj_body(x_ref, sh_ref, sc_ref, g_ref, w_ref, *rest, rope, cache_layer, group0, tiles_per_group):
    if rope:
        cos_ref, sin_ref, qkv_ref, f_ref = rest
    else:
        qkv_ref, f_ref, k32_ref, v32_ref = rest[-4:]
    grp = _group_of(pl.program_id(0), group0, tiles_per_group)
    sh = sh_ref[pl.ds(grp, 1), :]
    sc = sc_ref[pl.ds(grp, 1), :]
    hn = _rms(x_ref[...], g_ref[...]) * (1.0 + sc) + sh
    proj = jnp.dot(hn.astype(BF16), w_ref[...], preferred_element_type=F32)
    qk = proj[:, :2 * ATTN_W]
    v = proj[:, 2 * ATTN_W:QKV_W]
    if rope:
        cos = jnp.tile(cos_ref[...], (1, 2 * ATTN_W // HEAD_W))
        sin = jnp.tile(sin_ref[...], (1, 2 * ATTN_W // HEAD_W))
        lane = lax.broadcasted_iota(jnp.int32, qk.shape, 1)
        low = (lane % 32) < 16
        rot = jnp.where(low, pltpu.roll(qk, 2 * ATTN_W - 16, 1), pltpu.roll(qk, 16, 1))
        qk = qk * cos + rot * sin
    else:
        for ref, val in ((k32_ref, qk[:, ATTN_W:]), (v32_ref, v)):
            val = val.reshape(TM // T_CTX, T_CTX, ATTN_W)
            if cache_layer == 0:
                ref[:, 0] = val
                ref[:, 1:] = jnp.zeros((TM // T_CTX, DEPTH - 1, T_CTX, ATTN_W), F32)
            else:
                ref[...] = val
    qkv_ref[:, 0:ATTN_W] = (qk[:, :ATTN_W] * (QK_DIM ** -0.5 * math.log2(math.e))).astype(BF16)
    qkv_ref[:, ATTN_W:2 * ATTN_W] = qk[:, ATTN_W:].astype(BF16)
    qkv_ref[:, 2 * ATTN_W:] = v.astype(BF16)
    f_ref[...] = proj[:, QKV_W:]


def _inproj(x, mod, layer, g_pre, w_in_bf, rope_tabs, group0, tiles_per_group, cache=None):
    rows = x.shape[0]
    rope = rope_tabs is not None
    aliases = {}
    row = lambda i: (i, 0)
    const = lambda i: (0, 0)
    in_specs = [pl.BlockSpec((TM, D), row),
                pl.BlockSpec((None, N_GROUPS_PAD, D), lambda i: (layer, 0, 0)),
                pl.BlockSpec((None, N_GROUPS_PAD, D), lambda i: (layer, 0, 1)),
                pl.BlockSpec((1, D), const),
                pl.BlockSpec((D, IN_W), const)]
    args = [x, mod, mod, g_pre.reshape(1, D), w_in_bf]
    out_specs = [pl.BlockSpec((TM, QKV_W), row), pl.BlockSpec((TM, FOUR_W), row)]
    out_shape = [jax.ShapeDtypeStruct((rows, QKV_W), BF16), jax.ShapeDtypeStruct((rows, FOUR_W), F32)]
    if rope:
        tiles_per_seq = T_LAT // TM
        tab = pl.BlockSpec((TM, HEAD_W), lambda i: (i % tiles_per_seq, 0))
        in_specs += [tab, tab]
        args += list(rope_tabs)
    else:
        nb = TM // T_CTX
        if cache is None:
            cache_spec = pl.BlockSpec((nb, DEPTH, T_CTX, ATTN_W), lambda i: (i, 0, 0, 0))
        else:
            cache_spec = pl.BlockSpec((nb, None, T_CTX, ATTN_W), lambda i: (i, layer, 0, 0))
            aliases = {len(args): 2, len(args) + 1: 3}
            in_specs += [pl.BlockSpec(memory_space=pl.ANY)] * 2
            args += list(cache)
        out_specs += [cache_spec] * 2
        out_shape += [jax.ShapeDtypeStruct((N_CTX_B, DEPTH, T_CTX, ATTN_W), F32)] * 2
    return pl.pallas_call(
        functools.partial(_inproj_body, rope=rope, cache_layer=None if rope else (0 if cache is None else layer),
                          group0=group0, tiles_per_group=tiles_per_group),
        grid=(rows // TM,),
        in_specs=in_specs, out_specs=out_specs, out_shape=out_shape,
        input_output_aliases=aliases,
        compiler_params=_cparams(("arbitrary",)),
        name="inproj_lat" if rope else "inproj_ctx",
    )(*args)


def _rope_tables():
    half = QK_DIM // 2
    inv = 1.0 / (ROPE_THETA ** (np.arange(0, half, 2, dtype=np.float64) / half))
    pos = np.arange(T_LAT)
    def tab(p):
        ang = p[:, None].astype(np.float64) * inv[None, :]
        return np.concatenate([ang, ang], axis=-1)
    ang = np.concatenate([tab(pos // GRID_W), tab(pos % GRID_W)], axis=-1)
    sign = np.where((np.arange(QK_DIM) % 32) < 16, -1.0, 1.0)
    cos = np.tile(np.cos(ang), (1, 2))
    sin = np.tile(np.sin(ang) * sign[None, :], (1, 2))
    return jnp.asarray(cos, F32), jnp.asarray(sin, F32)


def _lam(lam_ref, lam_init):
    lp = lam_ref[...]
    return (jnp.exp(jnp.sum(lp[0:1] * lp[1:2], keepdims=True))
            - jnp.exp(jnp.sum(lp[2:3] * lp[3:4], keepdims=True)) + lam_init)


def _stack_maps(q):
    lane = lax.broadcasted_iota(jnp.int32, q.shape, 1)
    zero = jnp.zeros_like(q)
    return jnp.concatenate([jnp.where(lane < QK_DIM, q, zero), jnp.where(lane >= QK_DIM, q, zero)], axis=0)


def _softmax_step(qq, kb, vb, carry):
    m, l, acc = carry
    s = lax.dot_general(qq, kb, (((1,), (1,)), ((), ())), preferred_element_type=F32)
    m_new = jnp.maximum(m, jnp.max(s, axis=-1, keepdims=True))
    alpha = jnp.exp2(m - m_new)
    p = jnp.exp2(s - m_new)
    l = alpha * l + jnp.sum(p, axis=-1, keepdims=True)
    acc = alpha * acc + jnp.dot(p.astype(BF16), vb, preferred_element_type=F32)
    return m_new, l, acc


def _softmax_init(rows):
    return (jnp.full((rows, 1), -jnp.inf, F32), jnp.zeros((rows, 1), F32), jnp.zeros((rows, HEAD_W), F32))


def _diff_out(carry, tq, lam, gs, lam_init):
    _, l, acc = carry
    o = acc / l
    a = o[:tq] - lam * o[tq:]
    return _rms(a, gs) * (1.0 - lam_init)


def _attn_ctx_body(lam_ref, gs_ref, q_ref, k_ref, v_ref, o_ref, *, lam_init):
    lam = _lam(lam_ref, lam_init)
    for h in range(N_HEADS):
        sl = slice(h * HEAD_W, (h + 1) * HEAD_W)
        carry = _softmax_step(_stack_maps(q_ref[:, sl]), k_ref[:, sl], v_ref[:, sl], _softmax_init(2 * T_CTX))
        o_ref[:, sl] = _diff_out(carry, T_CTX, lam, gs_ref[...], lam_init).astype(o_ref.dtype)


def _attn_ctx(qkv, lam_params, g_subln, layer, lam_init):
    qkv3 = qkv.reshape(N_CTX_B, T_CTX, QKV_W)
    blk = lambda part: pl.BlockSpec((None, T_CTX, ATTN_W), lambda b: (b, 0, part))
    return pl.pallas_call(
        functools.partial(_attn_ctx_body, lam_init=lam_init),
        grid=(N_CTX_B,),
        in_specs=[pl.BlockSpec((None, 4, QK_DIM), lambda b: (layer, 0, 0)),
                  pl.BlockSpec((None, 1, HEAD_W), lambda b: (layer, 0, 0)),
                  blk(0), blk(1), blk(2)],
        out_specs=pl.BlockSpec((None, T_CTX, ATTN_W), lambda b: (b, 0, 0)),
        out_shape=jax.ShapeDtypeStruct((N_CTX_B, T_CTX, ATTN_W), BF16),
        compiler_params=_cparams(("arbitrary",)),
        name="attn_ctx",
    )(lam_params, g_subln.reshape(DEPTH, 1, HEAD_W), qkv3, qkv3, qkv3)


def _attn_lat_body(lam_ref, gs_ref, q_ref, kc_ref, vc_ref, kl_ref, vl_ref, o_ref, kcat, vcat, *, lam_init):
    @pl.when(pl.program_id(2) == 0)
    def _():
        kcat[0:T_PAST, :] = kc_ref[...]
        kcat[T_PAST:, :] = kl_ref[...]
        vcat[0:T_PAST, :] = vc_ref[...]
        vcat[T_PAST:, :] = vl_ref[...]

    lam = _lam(lam_ref, lam_init)
    for sub in range(Q_SUB):
        rows = slice(sub * TQ, (sub + 1) * TQ)
        qq = _stack_maps(q_ref[rows, :])

        def body(c, carry, qq=qq):
            start = pl.multiple_of(c * TK, TK)
            return _softmax_step(qq, kcat[pl.ds(start, TK), :], vcat[pl.ds(start, TK), :], carry)

        carry = lax.fori_loop(0, (T_PAST + T_LAT) // TK, body, _softmax_init(2 * TQ), unroll=ATTN_UNROLL)
        o_ref[rows, :] = _diff_out(carry, TQ, lam, gs_ref[...], lam_init).astype(o_ref.dtype)


def _attn_lat(qkv, ck, cv, lam_params, g_subln, layer, lam_init):
    qkv3 = qkv.reshape(N_LAT_B, T_LAT, QKV_W)
    full = lambda off: pl.BlockSpec((None, T_LAT, HEAD_W), lambda b, h, i: (b, 0, off + h))
    past = pl.BlockSpec((None, T_PAST, HEAD_W), lambda b, h, i: (b, 0, h))
    return pl.pallas_call(
        functools.partial(_attn_lat_body, lam_init=lam_init),
        grid=(N_LAT_B, N_HEADS, T_LAT // (Q_SUB * TQ)),
        in_specs=[pl.BlockSpec((None, 4, QK_DIM), lambda b, h, i: (layer, 0, 0)),
                  pl.BlockSpec((None, 1, HEAD_W), lambda b, h, i: (layer, 0, 0)),
                  pl.BlockSpec((None, Q_SUB * TQ, HEAD_W), lambda b, h, i: (b, i, h)),
                  past, past, full(N_HEADS), full(2 * N_HEADS)],
        out_specs=pl.BlockSpec((None, Q_SUB * TQ, HEAD_W), lambda b, h, i: (b, i, h)),
        out_shape=jax.ShapeDtypeStruct((N_LAT_B, T_LAT, ATTN_W), BF16),
        scratch_shapes=[pltpu.VMEM((T_PAST + T_LAT, HEAD_W), BF16)] * 2,
        compiler_params=_cparams(("arbitrary", "arbitrary", "arbitrary")),
        name="attn_lat",
    )(lam_params, g_subln.reshape(DEPTH, 1, HEAD_W), qkv3, ck, cv, qkv3, qkv3)


def _fold_groups(ur, ui, wf_ref, scale):
    outs = []
    for g in range(F_GROUPS):
        sl = slice(g * F_CH, (g + 1) * F_CH)
        lhs = jnp.concatenate([ur[:, sl], ui[:, sl]], axis=1).astype(BF16)
        outs.append(jnp.dot(lhs, wf_ref[g], preferred_element_type=F32))
    return jnp.concatenate(outs, axis=1) * scale


def _four_ctx_body(f_ref, dft_ref, wf_ref, o_ref):
    u = jnp.dot(dft_ref[...], f_ref[...].astype(BF16), preferred_element_type=F32)
    o_ref[...] = _fold_groups(u[:T_CTX], u[T_CTX:], wf_ref, 1.0 / math.sqrt(T_CTX * F_CH)).astype(o_ref.dtype)


def _four_ctx(f, wfold_bf, layer):
    ang = 2.0 * np.pi * np.outer(np.arange(T_CTX), np.arange(T_CTX)) / T_CTX
    dft = jnp.asarray(np.concatenate([np.cos(ang), -np.sin(ang)], axis=0), F32).astype(BF16)
    return pl.pallas_call(
        _four_ctx_body,
        grid=(N_CTX_B,),
        in_specs=[pl.BlockSpec((None, T_CTX, FOUR_W), lambda b: (b, 0, 0)),
                  pl.BlockSpec((2 * T_CTX, T_CTX), lambda b: (0, 0)),
                  pl.BlockSpec((None, F_GROUPS, 2 * F_CH, F_CH), lambda b: (layer, 0, 0, 0))],
        out_specs=pl.BlockSpec((None, T_CTX, FOUR_W), lambda b: (b, 0, 0)),
        out_shape=jax.ShapeDtypeStruct((N_CTX_B, T_CTX, FOUR_W), BF16),
        compiler_params=_cparams(("arbitrary",)),
        name="fourier_ctx",
    )(f.reshape(N_CTX_B, T_CTX, FOUR_W), dft, wfold_bf)


def _fft_body(x_ref, ma_ref, twc_ref, tws_ref, mc_ref, ms_ref, wf_ref, o_ref, h_ref):
    nb = FFT_R * FFT_S
    nj = FFT_R // FFT_S
    j = pl.program_id(1)

    @pl.when(j < nj)
    def _():
        x = x_ref[...].reshape(nb, FOUR_W).astype(BF16)
        g = jnp.dot(ma_ref[...], x, preferred_element_type=F32)
        gr, gi = g[:nb], g[nb:]
        twc = jnp.tile(twc_ref[...], (1, FOUR_W // 128))
        tws = jnp.tile(tws_ref[...], (1, FOUR_W // 128))
        lo = pl.ds(pl.multiple_of(j * FFT_S, FFT_S), FFT_S)
        h_ref[lo, :, 0:FOUR_W] = (gr * twc + gi * tws).reshape(FFT_S, FFT_R, FOUR_W)
        h_ref[lo, :, FOUR_W:] = (gi * twc - gr * tws).reshape(FFT_S, FFT_R, FOUR_W)

    @pl.when(j >= nj)
    def _():
        hi = pl.ds(pl.multiple_of((j - nj) * FFT_S, FFT_S), FFT_S)
        h = h_ref[:, hi, :].reshape(nb, 2 * FOUR_W).astype(BF16)
        p = jnp.dot(mc_ref[...], h, preferred_element_type=F32)
        q = jnp.dot(ms_ref[...], h, preferred_element_type=F32)
        ur = p[:, :FOUR_W] + q[:, FOUR_W:]
        ui = p[:, FOUR_W:] - q[:, :FOUR_W]
        out = _fold_groups(ur, ui, wf_ref, 1.0 / math.sqrt(T_LAT * F_CH))
        o_ref[...] = out.reshape(FFT_R, FFT_S, FOUR_W)


def _fft_consts():
    r, s = FFT_R, FFT_S
    nb = r * s
    ang = 2.0 * np.pi * np.outer(np.arange(r), np.arange(r)) / r
    c, sn = np.cos(ang), np.sin(ang)
    eye = np.eye(s)
    ma = np.concatenate([np.einsum('pb,ts->tpbs', c, eye).reshape(nb, nb),
                         np.einsum('pb,ts->tpbs', -sn, eye).reshape(nb, nb)], axis=0)
    mbc = np.einsum('pa,ts->ptas', c, eye).reshape(nb, nb)
    mbs = np.einsum('pa,ts->ptas', sn, eye).reshape(nb, nb)
    tw = 2.0 * np.pi * np.outer(np.arange(r), np.arange(r)).reshape(-1) / (r * r)
    twc = np.broadcast_to(np.cos(tw)[:, None], (r * r, 128))
    tws = np.broadcast_to(np.sin(tw)[:, None], (r * r, 128))
    bf = lambda a: jnp.asarray(a, F32).astype(BF16)
    return bf(ma), bf(mbc), bf(mbs), jnp.asarray(twc, F32), jnp.asarray(tws, F32)


def _four_lat(f, wfold_bf, layer, consts):
    ma, mbc, mbs, twc, tws = consts
    r, s = FFT_R, FFT_S
    nb = r * s
    nj = r // s
    f4 = f.reshape(N_LAT_B, r, r, FOUR_W)
    a_step = lambda j: jnp.minimum(j, nj - 1)
    b_step = lambda j: jnp.maximum(j - nj, 0)
    const = lambda b, j: (0, 0)
    out = pl.pallas_call(
        _fft_body,
        grid=(N_LAT_B, 2 * nj),
        in_specs=[pl.BlockSpec((None, r, s, FOUR_W), lambda b, j: (b, 0, a_step(j), 0)),
                  pl.BlockSpec((2 * nb, nb), const),
                  pl.BlockSpec((nb, 128), lambda b, j: (a_step(j), 0)),
                  pl.BlockSpec((nb, 128), lambda b, j: (a_step(j), 0)),
                  pl.BlockSpec((nb, nb), const),
                  pl.BlockSpec((nb, nb), const),
                  pl.BlockSpec((None, F_GROUPS, 2 * F_CH, F_CH), lambda b, j: (layer, 0, 0, 0))],
        out_specs=pl.BlockSpec((None, r, s, FOUR_W), lambda b, j: (b, 0, b_step(j), 0)),
        out_shape=jax.ShapeDtypeStruct((N_LAT_B, r, r, FOUR_W), F32),
        scratch_shapes=[pltpu.VMEM((r, r, 2 * FOUR_W), F32)],
        compiler_params=_cparams(("arbitrary", "arbitrary")),
        name="fft_lat",
    )(f4, ma, twc, tws, mbc, mbs, wfold_bf)
    return out.reshape(N_LAT_B * T_LAT, FOUR_W)


def _outproj_body(a_ref, f_ref, x_ref, gt_ref, sh_ref, sc_ref, gpost_ref, gpre_ref, wo_ref, *rest,
                  moe, group0, tiles_per_group):
    if moe:
        wr_ref, x1_ref, hn_ref, gates_ref, sel_ref = rest
    else:
        x1_ref, hn_ref = rest
    grp = _group_of(pl.program_id(0), group0, tiles_per_group)
    gt, sh, sc = gt_ref[pl.ds(grp, 1), :], sh_ref[pl.ds(grp, 1), :], sc_ref[pl.ds(grp, 1), :]
    for half in range(TM_OUT // TM):
        rows = slice(half * TM, (half + 1) * TM)
        mix_in = jnp.concatenate([a_ref[rows, :], f_ref[rows, :].astype(BF16)], axis=1)
        mixed = jnp.dot(mix_in, wo_ref[...], preferred_element_type=F32)
        x1 = x_ref[rows, :] + gt * _rms(mixed, gpost_ref[...])
        x1_ref[rows, :] = x1
        hn = _rms(x1, gpre_ref[...]) * (1.0 + sc) + sh
        hn_ref[rows, :] = hn.astype(hn_ref.dtype)
        if moe:
            gates, sel = _route_top2(hn, wr_ref[...])
            gates_ref[rows, :] = gates
            sel_ref[rows, :] = sel.astype(sel_ref.dtype)


def _route_top2(hn, wr):
    wr_hi = wr.astype(BF16)
    wr_lo = (wr - wr_hi.astype(F32)).astype(BF16)
    hn_hi = hn.astype(BF16)
    hn_lo = (hn - hn_hi.astype(F32)).astype(BF16)
    hh = jnp.dot(hn_hi, jnp.concatenate([wr_hi, wr_lo], axis=1), preferred_element_type=F32)
    logits = hh[:, :E_PAD] + hh[:, E_PAD:] + jnp.dot(hn_lo, wr_hi, preferred_element_type=F32)
    lane = lax.broadcasted_iota(jnp.int32, logits.shape, 1).astype(F32)
    neg = jnp.float32(-jnp.inf)
    logits = jnp.where(lane < float(N_EXPERTS), logits, neg)
    m1 = jnp.max(logits, axis=-1, keepdims=True)
    i1 = jnp.min(jnp.where(logits == m1, lane, float(E_PAD)), axis=-1, keepdims=True)
    rest_l = jnp.where(lane == i1, neg, logits)
    m2 = jnp.max(rest_l, axis=-1, keepdims=True)
    i2 = jnp.min(jnp.where(rest_l == m2, lane, float(E_PAD)), axis=-1, keepdims=True)
    e2 = jnp.exp(m2 - m1)
    w1 = 1.0 / (1.0 + e2)
    w2 = e2 / (1.0 + e2)
    gates = jnp.where(lane == i1, w1, 0.0) + jnp.where(lane == i2, w2, 0.0)
    sel = jnp.where((lane == i1) | (lane == i2), 1.0, 0.0)
    return gates, sel


def _outproj(attn_o, four_o, x, mod, layer, g_post, g_ffn_pre, w_out_bf, w_router_pad, group0, tiles_per_group):
    rows = x.shape[0]
    moe = w_router_pad is not None
    row = lambda i: (i, 0)
    const = lambda i: (0, 0)
    modspec = lambda k: pl.BlockSpec((None, N_GROUPS_PAD, D), lambda i: (layer, 0, k))
    blocks_per_group = None if tiles_per_group is None else tiles_per_group * TM // TM_OUT
    in_specs = [pl.BlockSpec((TM_OUT, ATTN_W), row), pl.BlockSpec((TM_OUT, FOUR_W), row),
                pl.BlockSpec((TM_OUT, D), row),
                modspec(2), modspec(3), modspec(4),
                pl.BlockSpec((1, D), const), pl.BlockSpec((1, D), const), pl.BlockSpec((D, D), const)]
    args = [attn_o, four_o, x, mod, mod, mod, g_post.reshape(1, D), g_ffn_pre.reshape(1, D), w_out_bf]
    out_specs = [pl.BlockSpec((TM_OUT, D), row), pl.BlockSpec((TM_OUT, D), row)]
    out_shape = [jax.ShapeDtypeStruct((rows, D), F32), jax.ShapeDtypeStruct((rows, D), F32 if moe else BF16)]
    if moe:
        in_specs.append(pl.BlockSpec((D, E_PAD), const))
        args.append(w_router_pad)
        out_specs += [pl.BlockSpec((TM_OUT, E_PAD), row)] * 2
        out_shape += [jax.ShapeDtypeStruct((rows, E_PAD), F32), jax.ShapeDtypeStruct((rows, E_PAD), BF16)]
    return pl.pallas_call(
        functools.partial(_outproj_body, moe=moe, group0=group0, tiles_per_group=blocks_per_group),
        grid=(rows // TM_OUT,),
        in_specs=in_specs, out_specs=out_specs, out_shape=out_shape,
        compiler_params=_cparams(("arbitrary",)),
        name="outproj_moe" if moe else "outproj",
    )(*args)


def _swiglu(hn, wg, wu, wd):
    g = jnp.dot(hn, wg, preferred_element_type=F32)
    u = jnp.dot(hn, wu, preferred_element_type=F32)
    return jnp.dot((_silu(g) * u).astype(BF16), wd, preferred_element_type=F32)


def _ffn_dense_body(hn_ref, x1_ref, gt_ref, gpost_ref, wg_ref, wu_ref, wd_ref, o_ref, *, group0, tiles_per_group):
    grp = _group_of(pl.program_id(0), group0, tiles_per_group)
    ff = _swiglu(hn_ref[...], wg_ref[...], wu_ref[...], wd_ref[...])
    o_ref[...] = x1_ref[...] + gt_ref[pl.ds(grp, 1), :] * _rms(ff, gpost_ref[...])


def _ffn_dense(hn, x1, mod, layer, g_post, wg, wu, wd, group0, tiles_per_group):
    rows = x1.shape[0]
    row = lambda i: (i, 0)
    const = lambda i: (0, 0)
    resident = lambda shape: pl.BlockSpec(shape, const, pipeline_mode=pl.Buffered(1))
    return pl.pallas_call(
        functools.partial(_ffn_dense_body, group0=group0, tiles_per_group=tiles_per_group),
        grid=(rows // TM,),
        in_specs=[pl.BlockSpec((TM, D), row), pl.BlockSpec((TM, D), row),
                  pl.BlockSpec((None, N_GROUPS_PAD, D), lambda i: (layer, 0, 5)),
                  pl.BlockSpec((1, D), const),
                  resident((D, D_FF)), resident((D, D_FF)), resident((D_FF, D))],
        out_specs=pl.BlockSpec((TM, D), row),
        out_shape=jax.ShapeDtypeStruct((rows, D), F32),
        compiler_params=_cparams(("arbitrary",)),
        name="ffn_dense",
    )(hn, x1, mod, g_post.reshape(1, D), wg, wu, wd)


def _route_body(*refs, firsts):
    ns = len(firsts)
    gates_refs, sel_refs = refs[:ns], refs[ns:2 * ns]
    ltri_ref, slot_ref, meta_ref, cnt_ref, off_ref, run_ref = refs[2 * ns:]
    phase, t = pl.program_id(0), pl.program_id(1)

    def this_stream(stream_refs):
        val = stream_refs[0][...]
        for k in range(1, ns):
            val = jnp.where(t >= firsts[k], stream_refs[k][...], val)
        return val

    sel = this_stream(sel_refs)

    @pl.when((phase == 0) & (t == 0))
    def _():
        cnt_ref[...] = jnp.zeros_like(cnt_ref)

    @pl.when(phase == 0)
    def _():
        cnt_ref[...] += jnp.sum(sel.astype(F32), axis=0, keepdims=True)

    @pl.when((phase == 1) & (t == 0))
    def _():
        cnt = cnt_ref[...]
        padded = jnp.ceil(cnt * (1.0 / TMOE)) * TMOE
        lane = lax.broadcasted_iota(jnp.int32, cnt.shape, 1).astype(F32)
        start = lane * TMOE
        off = jnp.zeros_like(cnt)
        tile_e = jnp.zeros_like(cnt)
        tile_n = jnp.zeros_like(cnt)
        running = jnp.zeros((1, 1), F32)
        for e in range(N_EXPERTS):
            pick = lane == float(e)
            cnt_e = jnp.sum(jnp.where(pick, cnt, 0.0), axis=-1, keepdims=True)
            pad_e = jnp.sum(jnp.where(pick, padded, 0.0), axis=-1, keepdims=True)
            off = jnp.where(pick, running, off)
            in_seg = (start >= running) & (start < running + pad_e)
            tile_e = jnp.where(in_seg, float(e), tile_e)
            tile_n = jnp.where(in_seg, jnp.clip(cnt_e - (start - running), 0.0, float(TMOE)), tile_n)
            running = running + pad_e
        tile_e = jnp.where(start >= running, float(N_EXPERTS - 1), tile_e)
        off_ref[...] = off
        run_ref[...] = jnp.zeros_like(run_ref)
        row = lax.broadcasted_iota(jnp.int32, meta_ref.shape, 0)
        last_tile = jnp.where(lane < float(N_EXPERTS), off + padded - TMOE, running)
        last_tile = jnp.where((lane < float(N_EXPERTS)) & (padded == 0.0), -1.0, last_tile)
        meta_ref[...] = jnp.where(row == 0, tile_e, jnp.where(row == 1, tile_n,
                                  jnp.where(row == 2, last_tile, 0.0)))

    @pl.when(phase == 1)
    def _():
        earlier = jnp.dot(ltri_ref[...], sel, preferred_element_type=F32)
        pos = earlier + run_ref[...] + off_ref[...]
        run_ref[...] += jnp.sum(sel.astype(F32), axis=0, keepdims=True)
        chosen = sel > 0
        s_lo = jnp.min(jnp.where(chosen, pos, float(NP)), axis=-1, keepdims=True)
        s_hi = jnp.max(jnp.where(chosen, pos, -1.0), axis=-1, keepdims=True)
        g = this_stream(gates_refs)
        w_lo = jnp.sum(jnp.where(chosen & (pos == s_lo), g, 0.0), axis=-1, keepdims=True)
        w_hi = jnp.sum(jnp.where(chosen & (pos == s_hi), g, 0.0), axis=-1, keepdims=True)
        lane = lax.broadcasted_iota(jnp.int32, pos.shape, 1)
        slot_ref[...] = jnp.where(lane == 0, s_lo, jnp.where(lane == 1, s_hi,
                                  jnp.where(lane == 2, w_lo, jnp.where(lane == 3, w_hi, 0.0))))


def _route(gates, sels):
    tiles = [g.shape[0] // TM for g in gates]
    firsts = [int(f) for f in np.cumsum([0] + tiles)]
    n, rows = firsts[-1], firsts[-1] * TM
    ltri = jnp.asarray(np.tril(np.ones((TM, TM)), -1), F32).astype(BF16)
    stream_spec = lambda lo, cnt: pl.BlockSpec((TM, E_PAD), lambda p, t: (jnp.clip(t - lo, 0, cnt - 1), 0))
    stream_specs = [stream_spec(lo, cnt) for lo, cnt in zip(firsts[:-1], tiles)]
    return pl.pallas_call(
        functools.partial(_route_body, firsts=tuple(firsts[:-1])),
        grid=(2, n),
        in_specs=stream_specs + stream_specs + [pl.BlockSpec((TM, TM), lambda p, t: (0, 0))],
        out_specs=[pl.BlockSpec((TM, E_PAD), lambda p, t: (t * p, 0)),
                   pl.BlockSpec((N_GROUPS_PAD, E_PAD), lambda p, t: (0, 0))],
        out_shape=[jax.ShapeDtypeStruct((rows, E_PAD), F32), jax.ShapeDtypeStruct((N_GROUPS_PAD, E_PAD), F32)],
        scratch_shapes=[pltpu.VMEM((1, E_PAD), F32)] * 3,
        compiler_params=_cparams(("arbitrary", "arbitrary")),
        name="moe_route",
    )(*gates, *sels, ltri)


def _row_copies(slot_ref, src_of, dst_of, sem, *, start=True, wait=True):
    sub = 8

    def copy(g, j, k):
        r = pl.multiple_of(g * sub, sub) + j
        s = slot_ref[0, 2 * r + k]
        return pltpu.make_async_copy(src_of(r, k, s), dst_of(r, k, s), sem)

    def each(act):
        def trip(g, carry):
            for j in range(sub):
                for k in range(TOP_K):
                    act(copy(g, j, k))
            return carry
        lax.fori_loop(0, TM // sub, trip, 0)

    if start:
        each(lambda c: c.start())
    if wait:
        each(lambda c: c.wait())


def _dispatch_body(slot_ref, pad_ref, *rest, tiles):
    hn_refs, (xs_ref, zeros, sem) = rest[:len(tiles)], rest[len(tiles):]
    i = pl.program_id(0)

    @pl.when(i == 0)
    def _():
        zeros[...] = jnp.zeros_like(zeros)
        fills = []
        for e in range(N_EXPERTS):
            s = pad_ref[0, e]
            fills.append((s >= 0, pl.multiple_of(jnp.maximum(s, 0), TMOE)))
        for j in range(N_EXPERTS):
            s = NP - (j + 1) * TMOE
            fills.append((s >= pad_ref[0, N_EXPERTS], s))
        fill = lambda s: pltpu.make_async_copy(zeros, xs_ref.at[pl.ds(s, TMOE), :], sem)
        for live, s in fills:
            @pl.when(live)
            def _(s=s):
                fill(s).start()
        for live, s in fills:
            @pl.when(live)
            def _(s=s):
                fill(s).wait()

    first = 0
    for hn_ref, n in zip(hn_refs, tiles):
        @pl.when((i >= first) & (i < first + n))
        def _(hn_ref=hn_ref):
            _row_copies(slot_ref,
                        lambda r, k, s: hn_ref.at[pl.ds(r, 1), :],
                        lambda r, k, s: xs_ref.at[pl.ds(s, 1), :], sem)
        first += n


def _dispatch(hns, slots, pad_starts):
    tiles = [h.shape[0] // TM for h in hns]
    firsts = np.cumsum([0] + tiles)
    n = int(firsts[-1])
    stream_spec = lambda lo, cnt: pl.BlockSpec((TM, D), lambda i: (jnp.clip(i - lo, 0, cnt - 1), 0))
    return pl.pallas_call(
        functools.partial(_dispatch_body, tiles=tuple(tiles)),
        grid=(n,),
        in_specs=[pl.BlockSpec((None, 1, 2 * TM), lambda i: (i, 0, 0), memory_space=pltpu.SMEM),
                  pl.BlockSpec(memory_space=pltpu.SMEM)]
                 + [stream_spec(int(lo), cnt) for lo, cnt in zip(firsts[:-1], tiles)],
        out_specs=pl.BlockSpec(memory_space=pl.ANY),
        out_shape=jax.ShapeDtypeStruct((NP, D), F32),
        scratch_shapes=[pltpu.VMEM((TMOE, D), F32), pltpu.SemaphoreType.DMA],
        compiler_params=_cparams(("arbitrary",)),
        name="moe_dispatch",
    )(slots.reshape(n, 1, 2 * TM), pad_starts, *hns)


def _experts_body(te_ref, tn_ref, xs_ref, wg_ref, wu_ref, wd_ref, ys_ref, wg_bf, wu_bf, wd_bf):
    i = pl.program_id(0)
    fresh = (i == 0) | (te_ref[i] != te_ref[jnp.maximum(i - 1, 0)])

    @pl.when(fresh)
    def _():
        wg_bf[...] = wg_ref[...].astype(BF16)
        wu_bf[...] = wu_ref[...].astype(BF16)
        wd_bf[...] = wd_ref[...].astype(BF16)

    n_real = tn_ref[i]

    @pl.when(n_real > 0)
    def _():
        ys_ref[...] = _swiglu(xs_ref[...].astype(BF16), wg_bf[...], wu_bf[...], wd_bf[...])

    @pl.when(n_real == 0)
    def _():
        ys_ref[...] = jnp.zeros_like(ys_ref)


def _experts(xs, tile_e, tile_n, eg, eu, ed):
    tile = lambda i, te, tn: (i, 0)
    wspec = lambda shape: pl.BlockSpec((None,) + shape, lambda i, te, tn: (te[i], 0, 0))
    return pl.pallas_call(
        _experts_body,
        grid_spec=pltpu.PrefetchScalarGridSpec(
            num_scalar_prefetch=2,
            grid=(N_TILES,),
            in_specs=[pl.BlockSpec((TMOE, D), tile), wspec((D, D_FF_E)), wspec((D, D_FF_E)), wspec((D_FF_E, D))],
            out_specs=pl.BlockSpec((TMOE, D), tile),
            scratch_shapes=[pltpu.VMEM((D, D_FF_E), BF16), pltpu.VMEM((D, D_FF_E), BF16),
                            pltpu.VMEM((D_FF_E, D), BF16)]),
        out_shape=jax.ShapeDtypeStruct((NP, D), F32),
        compiler_params=_cparams(("arbitrary",)),
        name="moe_experts",
    )(tile_e, tile_n, xs, eg, eu, ed)


def _combine_body(slot_ref, next_slot_ref, w_ref, x1_ref, gt_ref, gpost_ref, ys_ref, o_ref, ybuf, sems,
                  *, group0, tiles_per_group):
    i = pl.program_id(0)
    grp = _group_of(i, group0, tiles_per_group)
    cur = lax.rem(i, 2)
    src = lambda r, k, s: ys_ref.at[pl.ds(s, 1), :]
    dst = lambda half: (lambda r, k, s: ybuf.at[half, k, pl.ds(r, 1), :])

    @pl.when(i == 0)
    def _():
        _row_copies(slot_ref, src, dst(0), sems.at[0], wait=False)

    @pl.when(i + 1 < pl.num_programs(0))
    def _():
        _row_copies(next_slot_ref, src, dst(1 - cur), sems.at[1 - cur], wait=False)

    _row_copies(slot_ref, src, dst(cur), sems.at[cur], start=False)
    w = w_ref[...]
    lane = lax.broadcasted_iota(jnp.int32, w.shape, 1)
    w_lo = jnp.sum(jnp.where(lane == 2, w, 0.0), axis=-1, keepdims=True)
    w_hi = jnp.sum(jnp.where(lane == 3, w, 0.0), axis=-1, keepdims=True)
    ff = w_lo * ybuf[cur, 0] + w_hi * ybuf[cur, 1]
    o_ref[...] = x1_ref[...] + gt_ref[pl.ds(grp, 1), :] * _rms(ff, gpost_ref[...])


def _combine(ys, slots, slot_w, tile0, x1, mod, layer, g_post, group0, tiles_per_group):
    rows = x1.shape[0]
    n = rows // TM
    row = lambda i: (i, 0)
    slots3 = slots.reshape(-1, 1, 2 * TM)
    return pl.pallas_call(
        functools.partial(_combine_body, group0=group0, tiles_per_group=tiles_per_group),
        grid=(n,),
        in_specs=[pl.BlockSpec((None, 1, 2 * TM), lambda i: (tile0 + i, 0, 0), memory_space=pltpu.SMEM),
                  pl.BlockSpec((None, 1, 2 * TM), lambda i: (tile0 + jnp.minimum(i + 1, n - 1), 0, 0),
                               memory_space=pltpu.SMEM),
                  pl.BlockSpec((TM, E_PAD), lambda i: (tile0 + i, 0)), pl.BlockSpec((TM, D), row),
                  pl.BlockSpec((None, N_GROUPS_PAD, D), lambda i: (layer, 0, 5)),
                  pl.BlockSpec((1, D), lambda i: (0, 0)),
                  pl.BlockSpec(memory_space=pl.ANY)],
        out_specs=pl.BlockSpec((TM, D), row),
        out_shape=jax.ShapeDtypeStruct((rows, D), F32),
        scratch_shapes=[pltpu.VMEM((2, TOP_K, TM, D), F32), pltpu.SemaphoreType.DMA((2,))],
        compiler_params=_cparams(("arbitrary",)),
        name="moe_combine",
    )(slots3, slots3, slot_w, x1, mod, g_post.reshape(1, D), ys)


def kernel(x_prompt, x_sample, cache_k, cache_v, c, c_ctx, g_attn_pre, g_attn_post, g_ffn_pre, g_ffn_post,
           w_ada, b_ada, w_in, lam_params, g_subln, w_fnet, w_out, w_gate, w_up, w_down, w_router,
           e_gate, e_up, e_down):
    assert x_prompt.shape == (N_CTX_B, T_CTX, D) and x_sample.shape == (N_LAT_B, T_LAT, D)
    assert cache_k.shape == (N_LAT_B, DEPTH, T_PAST, N_HEADS, 2, QK_DIM)

    cond = jnp.zeros((N_GROUPS_PAD, D), F32).at[0].set(c_ctx).at[1:1 + N_LAT_B].set(c)
    mod = _ada(cond, w_ada, b_ada)
    wfold_bf = _wfold(w_fnet).astype(BF16)
    rope_tabs = _rope_tables()
    fft_consts = _fft_consts()

    w_in_bf = w_in.astype(BF16)
    w_out_bf = w_out.astype(BF16)
    w_router_pad = jnp.zeros((DEPTH // 2, D, E_PAD), F32).at[:, :, :N_EXPERTS].set(w_router)
    ck_bf = cache_k.reshape(N_LAT_B, DEPTH, T_PAST, ATTN_W).astype(BF16)
    cv_bf = cache_v.reshape(N_LAT_B, DEPTH, T_PAST, ATTN_W).astype(BF16)

    lat_tiles = T_LAT // TM
    streams = [dict(x=x_prompt.reshape(N_CTX_B * T_CTX, D), group0=0, tiles=None, lat=False),
               dict(x=x_sample.reshape(N_LAT_B * T_LAT, D), group0=1, tiles=lat_tiles, lat=True)]
    new_k = new_v = None
    for l in range(DEPTH):
        lam_init = 0.8 - 0.6 * math.exp(-0.3 * l)
        i = l // 2
        routed = []
        for s in streams:
            g0, tiles = s["group0"], s["tiles"]
            if s["lat"]:
                qkv, f = _inproj(s["x"], mod, l, g_attn_pre[l], w_in_bf[l], rope_tabs, g0, tiles)
                attn_o = _attn_lat(qkv, ck_bf[:, l], cv_bf[:, l], lam_params, g_subln, l, lam_init)
                attn_o = attn_o.reshape(N_LAT_B * T_LAT, ATTN_W)
                four_o = _four_lat(f, wfold_bf, l, fft_consts)
            else:
                qkv, f, new_k, new_v = _inproj(s["x"], mod, l, g_attn_pre[l], w_in_bf[l], None, g0, tiles,
                                               cache=None if l == 0 else (new_k, new_v))
                attn_o = _attn_ctx(qkv, lam_params, g_subln, l, lam_init).reshape(N_CTX_B * T_CTX, ATTN_W)
                four_o = _four_ctx(f, wfold_bf, l).reshape(N_CTX_B * T_CTX, FOUR_W)
            if l % 2 == 0:
                x1, hn = _outproj(attn_o, four_o, s["x"], mod, l, g_attn_post[l], g_ffn_pre[l], w_out_bf[l],
                                  None, g0, tiles)
                s["x"] = _ffn_dense(hn, x1, mod, l, g_ffn_post[l], w_gate[i].astype(BF16), w_up[i].astype(BF16),
                                    w_down[i].astype(BF16), g0, tiles)
            else:
                routed.append(_outproj(attn_o, four_o, s["x"], mod, l, g_attn_post[l], g_ffn_pre[l],
                                       w_out_bf[l], w_router_pad[i], g0, tiles))
        if l % 2 == 1:
            slot_w, meta = _route([r[2] for r in routed], [r[3] for r in routed])
            slots = slot_w[:, :2].astype(jnp.int32)
            tile_e = meta[0, :N_TILES].astype(jnp.int32)
            tile_n = meta[1, :N_TILES].astype(jnp.int32)
            pad_starts = meta[2:3, :N_EXPERTS + 1].astype(jnp.int32)
            bounds = np.cumsum([0] + [r[0].shape[0] for r in routed])
            xs = _dispatch([r[1] for r in routed], slots, pad_starts)
            ys = _experts(xs, tile_e, tile_n, e_gate[i], e_up[i], e_down[i])
            for s, r, lo in zip(streams, routed, bounds[:-1]):
                s["x"] = _combine(ys, slots, slot_w, int(lo) // TM, r[0], mod, l, g_ffn_post[l],
                                  s["group0"], s["tiles"])

    y_prompt = streams[0]["x"].reshape(N_CTX_B, T_CTX, D)
    y_sample = streams[1]["x"].reshape(N_LAT_B, T_LAT, D)
    new_cache_k = new_k.reshape(N_CTX_B, DEPTH, T_CTX, N_HEADS, 2, QK_DIM)
    new_cache_v = new_v.reshape(N_CTX_B, DEPTH, T_CTX, N_HEADS, HEAD_W)
    return (y_prompt, y_sample, new_cache_k, new_cache_v)
```

```python
import functools
import math

import numpy as np
import jax
import jax.numpy as jnp
from jax import lax
from jax.experimental import pallas as pl
from jax.experimental.pallas import tpu as pltpu

F32 = jnp.float32
BF16 = jnp.bfloat16

D = 1024
N_CTX_B = 16
T_CTX = 256
N_LAT_B = 2
T_LAT = 4096
T_PAST = 512
DEPTH = 2
GRID_W = 64
N_HEADS = 4
QK_DIM = 64
HEAD_W = 2 * QK_DIM
ATTN_W = N_HEADS * HEAD_W
F_GROUPS = 4
F_CH = 128
FOUR_W = F_GROUPS * F_CH
IN_W = 3 * ATTN_W + FOUR_W
QKV_W = 3 * ATTN_W
D_FF = 2816
N_EXPERTS = 8
D_FF_E = 1024
N_MOD = 6
EPS = 1e-6
ROPE_THETA = 10000.0
N_GROUPS_PAD = 8
E_PAD = 128

TM = 512
TM_OUT = 1024
TOP_K = 2
TMOE = 512
N_TOKENS = N_CTX_B * T_CTX + N_LAT_B * T_LAT
GROUP = 8
R_TILE = TOP_K * TM + N_EXPERTS * GROUP
N_TILES = -(-(TOP_K * N_TOKENS + (N_TOKENS // TM) * N_EXPERTS * (GROUP - 1)) // TMOE) + N_EXPERTS
NP = N_TILES * TMOE
TQ = 512
Q_SUB = 2
TK = 1536
ATTN_UNROLL = 3
FFT_R = 64
FFT_S = 8
VMEM_LIMIT = 56 * 1024 * 1024


def _cparams(sem):
    return pltpu.CompilerParams(dimension_semantics=sem, vmem_limit_bytes=VMEM_LIMIT)


def _rms(x, g):
    ms = jnp.mean(x * x, axis=-1, keepdims=True)
    return x * lax.rsqrt(ms + EPS) * g


def _silu(x):
    return x * (1.0 / (1.0 + jnp.exp(-x)))


def _ada_body(c_ref, w_ref, b_ref, o_ref):
    s = _silu(c_ref[...])
    o_ref[...] = jnp.dot(s, w_ref[...], preferred_element_type=F32,
                         precision=lax.Precision.HIGHEST) + b_ref[...]


def _ada(cond, w_ada, b_ada):
    tn = 1536
    return pl.pallas_call(
        _ada_body,
        grid=(DEPTH, N_MOD * D // tn),
        in_specs=[pl.BlockSpec((N_GROUPS_PAD, D), lambda l, j: (0, 0)),
                  pl.BlockSpec((None, D, tn), lambda l, j: (l, 0, j)),
                  pl.BlockSpec((None, 1, tn), lambda l, j: (l, 0, j))],
        out_specs=pl.BlockSpec((None, N_GROUPS_PAD, tn), lambda l, j: (l, 0, j)),
        out_shape=jax.ShapeDtypeStruct((DEPTH, N_GROUPS_PAD, N_MOD * D), F32),
        compiler_params=_cparams(("arbitrary", "arbitrary")),
        name="ada_mod",
    )(cond, w_ada, b_ada.reshape(DEPTH, 1, N_MOD * D))


def _wfold_body(cc_ref, sc_ref, w_ref, o_ref):
    w = w_ref[...]
    o_ref[0:F_CH, :] = jnp.dot(cc_ref[...], w, preferred_element_type=F32,
                               precision=lax.Precision.HIGHEST)
    o_ref[F_CH:2 * F_CH, :] = jnp.dot(sc_ref[...], w, preferred_element_type=F32,
                                      precision=lax.Precision.HIGHEST)


def _wfold(w_fnet):
    ang = 2.0 * np.pi * np.outer(np.arange(F_CH), np.arange(F_CH)) / F_CH
    cc = jnp.asarray(np.cos(ang), F32)
    sc = jnp.asarray(np.sin(ang), F32)
    cspec = pl.BlockSpec((F_CH, F_CH), lambda l, g: (0, 0))
    return pl.pallas_call(
        _wfold_body,
        grid=(DEPTH, F_GROUPS),
        in_specs=[cspec, cspec, pl.BlockSpec((None, None, F_CH, F_CH), lambda l, g: (l, g, 0, 0))],
        out_specs=pl.BlockSpec((None, None, 2 * F_CH, F_CH), lambda l, g: (l, g, 0, 0)),
        out_shape=jax.ShapeDtypeStruct((DEPTH, F_GROUPS, 2 * F_CH, F_CH), F32),
        compiler_params=_cparams(("arbitrary", "arbitrary")),
        name="fnet_fold",
    )(cc, sc, w_fnet)


def _group_of(i, group0, tiles_per_group):
    if tiles_per_group is None:
        return group0
    return group0 + lax.div(i, jnp.int32(tiles_per_group))


def _inproj_body(x_ref, sh_ref, sc_ref, g_ref, w_ref, *rest, rope, cache_layer, group0, tiles_per_group):
    if rope:
        cos_ref, sin_ref, qkv_ref, f_ref = rest
    else:
        qkv_ref, f_ref, k32_ref, v32_ref = rest[-4:]
    grp = _group_of(pl.program_id(0), group0, tiles_per_group)
    sh = sh_ref[pl.ds(grp, 1), :]
    sc = sc_ref[pl.ds(grp, 1), :]
    hn = _rms(x_ref[...], g_ref[...]) * (1.0 + sc) + sh
    proj = jnp.dot(hn.astype(BF16), w_ref[...], preferred_element_type=F32)
    qk = proj[:, :2 * ATTN_W]
    v = proj[:, 2 * ATTN_W:QKV_W]
    if rope:
        cos = jnp.tile(cos_ref[...], (1, 2 * ATTN_W // HEAD_W))
        sin = jnp.tile(sin_ref[...], (1, 2 * ATTN_W // HEAD_W))
        lane = lax.broadcasted_iota(jnp.int32, qk.shape, 1)
        low = (lane % 32) < 16
        rot = jnp.where(low, pltpu.roll(qk, 2 * ATTN_W - 16, 1), pltpu.roll(qk, 16, 1))
        qk = qk * cos + rot * sin
    else:
        for ref, val in ((k32_ref, qk[:, ATTN_W:]), (v32_ref, v)):
            val = val.reshape(TM // T_CTX, T_CTX, ATTN_W)
            if cache_layer == 0:
                ref[:, 0] = val
                ref[:, 1:] = jnp.zeros((TM // T_CTX, DEPTH - 1, T_CTX, ATTN_W), F32)
            else:
                ref[...] = val
    qkv_ref[:, 0:ATTN_W] = (qk[:, :ATTN_W] * (QK_DIM ** -0.5 * math.log2(math.e))).astype(BF16)
    qkv_ref[:, ATTN_W:2 * ATTN_W] = qk[:, ATTN_W:].astype(BF16)
    qkv_ref[:, 2 * ATTN_W:] = v.astype(BF16)
    f_ref[...] = proj[:, QKV_W:]


def _inproj(x, mod, layer, g_pre, w_in_bf, rope_tabs, group0, tiles_per_group, cache=None):
    rows = x.shape[0]
    rope = rope_tabs is not None
    aliases = {}
    row = lambda i: (i, 0)
    const = lambda i: (0, 0)
    in_specs = [pl.BlockSpec((TM, D), row),
                pl.BlockSpec((None, N_GROUPS_PAD, D), lambda i: (layer, 0, 0)),
                pl.BlockSpec((None, N_GROUPS_PAD, D), lambda i: (layer, 0, 1)),
                pl.BlockSpec((1, D), const),
                pl.BlockSpec((D, IN_W), const)]
    args = [x, mod, mod, g_pre.reshape(1, D), w_in_bf]
    out_specs = [pl.BlockSpec((TM, QKV_W), row), pl.BlockSpec((TM, FOUR_W), row)]
    out_shape = [jax.ShapeDtypeStruct((rows, QKV_W), BF16), jax.ShapeDtypeStruct((rows, FOUR_W), F32)]
    if rope:
        tiles_per_seq = T_LAT // TM
        tab = pl.BlockSpec((TM, HEAD_W), lambda i: (i % tiles_per_seq, 0))
        in_specs += [tab, tab]
        args += list(rope_tabs)
    else:
        nb = TM // T_CTX
        if cache is None:
            cache_spec = pl.BlockSpec((nb, DEPTH, T_CTX, ATTN_W), lambda i: (i, 0, 0, 0))
        else:
            cache_spec = pl.BlockSpec((nb, None, T_CTX, ATTN_W), lambda i: (i, layer, 0, 0))
            aliases = {len(args): 2, len(args) + 1: 3}
            in_specs += [pl.BlockSpec(memory_space=pl.ANY)] * 2
            args += list(cache)
        out_specs += [cache_spec] * 2
        out_shape += [jax.ShapeDtypeStruct((N_CTX_B, DEPTH, T_CTX, ATTN_W), F32)] * 2
    return pl.pallas_call(
        functools.partial(_inproj_body, rope=rope, cache_layer=None if rope else (0 if cache is None else layer),
                          group0=group0, tiles_per_group=tiles_per_group),
        grid=(rows // TM,),
        in_specs=in_specs, out_specs=out_specs, out_shape=out_shape,
        input_output_aliases=aliases,
        compiler_params=_cparams(("arbitrary",)),
        name="inproj_lat" if rope else "inproj_ctx",
    )(*args)


def _rope_tables():
    half = QK_DIM // 2
    inv = 1.0 / (ROPE_THETA ** (np.arange(0, half, 2, dtype=np.float64) / half))
    pos = np.arange(T_LAT)
    def tab(p):
        ang = p[:, None].astype(np.float64) * inv[None, :]
        return np.concatenate([ang, ang], axis=-1)
    ang = np.concatenate([tab(pos // GRID_W), tab(pos % GRID_W)], axis=-1)
    sign = np.where((np.arange(QK_DIM) % 32) < 16, -1.0, 1.0)
    cos = np.tile(np.cos(ang), (1, 2))
    sin = np.tile(np.sin(ang) * sign[None, :], (1, 2))
    return jnp.asarray(cos, F32), jnp.asarray(sin, F32)


def _lam(lam_ref, lam_init):
    lp = lam_ref[...]
    return (jnp.exp(jnp.sum(lp[0:1] * lp[1:2], keepdims=True))
            - jnp.exp(jnp.sum(lp[2:3] * lp[3:4], keepdims=True)) + lam_init)


def _stack_maps(q):
    lane = lax.broadcasted_iota(jnp.int32, q.shape, 1)
    zero = jnp.zeros_like(q)
    return jnp.concatenate([jnp.where(lane < QK_DIM, q, zero), jnp.where(lane >= QK_DIM, q, zero)], axis=0)


def _softmax_step(qq, kb, vb, carry):
    m, l, acc = carry
    s = lax.dot_general(qq, kb, (((1,), (1,)), ((), ())), preferred_element_type=F32)
    m_new = jnp.maximum(m, jnp.max(s, axis=-1, keepdims=True))
    alpha = jnp.exp2(m - m_new)
    p = jnp.exp2(s - m_new)
    l = alpha * l + jnp.sum(p, axis=-1, keepdims=True)
    acc = alpha * acc + jnp.dot(p.astype(BF16), vb, preferred_element_type=F32)
    return m_new, l, acc


def _softmax_init(rows):
    return (jnp.full((rows, 1), -jnp.inf, F32), jnp.zeros((rows, 1), F32), jnp.zeros((rows, HEAD_W), F32))


def _diff_out(carry, tq, lam, gs, lam_init):
    _, l, acc = carry
    o = acc / l
    a = o[:tq] - lam * o[tq:]
    return _rms(a, gs) * (1.0 - lam_init)


def _attn_ctx_body(lam_ref, gs_ref, q_ref, k_ref, v_ref, o_ref, *, lam_init):
    lam = _lam(lam_ref, lam_init)
    for h in range(N_HEADS):
        sl = slice(h * HEAD_W, (h + 1) * HEAD_W)
        carry = _softmax_step(_stack_maps(q_ref[:, sl]), k_ref[:, sl], v_ref[:, sl], _softmax_init(2 * T_CTX))
        o_ref[:, sl] = _diff_out(carry, T_CTX, lam, gs_ref[...], lam_init).astype(o_ref.dtype)


def _attn_ctx(qkv, lam_params, g_subln, layer, lam_init):
    qkv3 = qkv.reshape(N_CTX_B, T_CTX, QKV_W)
    blk = lambda part: pl.BlockSpec((None, T_CTX, ATTN_W), lambda b: (b, 0, part))
    return pl.pallas_call(
        functools.partial(_attn_ctx_body, lam_init=lam_init),
        grid=(N_CTX_B,),
        in_specs=[pl.BlockSpec((None, 4, QK_DIM), lambda b: (layer, 0, 0)),
                  pl.BlockSpec((None, 1, HEAD_W), lambda b: (layer, 0, 0)),
                  blk(0), blk(1), blk(2)],
        out_specs=pl.BlockSpec((None, T_CTX, ATTN_W), lambda b: (b, 0, 0)),
        out_shape=jax.ShapeDtypeStruct((N_CTX_B, T_CTX, ATTN_W), BF16),
        compiler_params=_cparams(("arbitrary",)),
        name="attn_ctx",
    )(lam_params, g_subln.reshape(DEPTH, 1, HEAD_W), qkv3, qkv3, qkv3)


def _attn_lat_body(lam_ref, gs_ref, q_ref, kc_ref, vc_ref, kl_ref, vl_ref, o_ref, kcat, vcat, *, lam_init):
    @pl.when(pl.program_id(2) == 0)
    def _():
        kcat[0:T_PAST, :] = kc_ref[...]
        kcat[T_PAST:, :] = kl_ref[...]
        vcat[0:T_PAST, :] = vc_ref[...]
        vcat[T_PAST:, :] = vl_ref[...]

    lam = _lam(lam_ref, lam_init)
    for sub in range(Q_SUB):
        rows = slice(sub * TQ, (sub + 1) * TQ)
        qq = _stack_maps(q_ref[rows, :])

        def body(c, carry, qq=qq):
            start = pl.multiple_of(c * TK, TK)
            return _softmax_step(qq, kcat[pl.ds(start, TK), :], vcat[pl.ds(start, TK), :], carry)

        carry = lax.fori_loop(0, (T_PAST + T_LAT) // TK, body, _softmax_init(2 * TQ), unroll=ATTN_UNROLL)
        o_ref[rows, :] = _diff_out(carry, TQ, lam, gs_ref[...], lam_init).astype(o_ref.dtype)


def _attn_lat(qkv, ck, cv, lam_params, g_subln, layer, lam_init):
    qkv3 = qkv.reshape(N_LAT_B, T_LAT, QKV_W)
    full = lambda off: pl.BlockSpec((None, T_LAT, HEAD_W), lambda b, h, i: (b, 0, off + h))
    past = pl.BlockSpec((None, T_PAST, HEAD_W), lambda b, h, i: (b, 0, h))
    return pl.pallas_call(
        functools.partial(_attn_lat_body, lam_init=lam_init),
        grid=(N_LAT_B, N_HEADS, T_LAT // (Q_SUB * TQ)),
        in_specs=[pl.BlockSpec((None, 4, QK_DIM), lambda b, h, i: (layer, 0, 0)),
                  pl.BlockSpec((None, 1, HEAD_W), lambda b, h, i: (layer, 0, 0)),
                  pl.BlockSpec((None, Q_SUB * TQ, HEAD_W), lambda b, h, i: (b, i, h)),
                  past, past, full(N_HEADS), full(2 * N_HEADS)],
        out_specs=pl.BlockSpec((None, Q_SUB * TQ, HEAD_W), lambda b, h, i: (b, i, h)),
        out_shape=jax.ShapeDtypeStruct((N_LAT_B, T_LAT, ATTN_W), BF16),
        scratch_shapes=[pltpu.VMEM((T_PAST + T_LAT, HEAD_W), BF16)] * 2,
        compiler_params=_cparams(("arbitrary", "arbitrary", "arbitrary")),
        name="attn_lat",
    )(lam_params, g_subln.reshape(DEPTH, 1, HEAD_W), qkv3, ck, cv, qkv3, qkv3)


def _fold_groups(ur, ui, wf_ref, scale):
    outs = []
    for g in range(F_GROUPS):
        sl = slice(g * F_CH, (g + 1) * F_CH)
        lhs = jnp.concatenate([ur[:, sl], ui[:, sl]], axis=1).astype(BF16)
        outs.append(jnp.dot(lhs, wf_ref[g], preferred_element_type=F32))
    return jnp.concatenate(outs, axis=1) * scale


def _four_ctx_body(f_ref, dft_ref, wf_ref, o_ref):
    u = jnp.dot(dft_ref[...], f_ref[...].astype(BF16), preferred_element_type=F32)
    o_ref[...] = _fold_groups(u[:T_CTX], u[T_CTX:], wf_ref, 1.0 / math.sqrt(T_CTX * F_CH)).astype(o_ref.dtype)


def _four_ctx(f, wfold_bf, layer):
    ang = 2.0 * np.pi * np.outer(np.arange(T_CTX), np.arange(T_CTX)) / T_CTX
    dft = jnp.asarray(np.concatenate([np.cos(ang), -np.sin(ang)], axis=0), F32).astype(BF16)
    return pl.pallas_call(
        _four_ctx_body,
        grid=(N_CTX_B,),
        in_specs=[pl.BlockSpec((None, T_CTX, FOUR_W), lambda b: (b, 0, 0)),
                  pl.BlockSpec((2 * T_CTX, T_CTX), lambda b: (0, 0)),
                  pl.BlockSpec((None, F_GROUPS, 2 * F_CH, F_CH), lambda b: (layer, 0, 0, 0))],
        out_specs=pl.BlockSpec((None, T_CTX, FOUR_W), lambda b: (b, 0, 0)),
        out_shape=jax.ShapeDtypeStruct((N_CTX_B, T_CTX, FOUR_W), BF16),
        compiler_params=_cparams(("arbitrary",)),
        name="fourier_ctx",
    )(f.reshape(N_CTX_B, T_CTX, FOUR_W), dft, wfold_bf)


def _fft_body(x_ref, ma_ref, twc_ref, tws_ref, mc_ref, ms_ref, wf_ref, o_ref, h_ref):
    nb = FFT_R * FFT_S
    nj = FFT_R // FFT_S
    j = pl.program_id(1)

    @pl.when(j < nj)
    def _():
        x = x_ref[...].reshape(nb, FOUR_W).astype(BF16)
        g = jnp.dot(ma_ref[...], x, preferred_element_type=F32)
        gr, gi = g[:nb], g[nb:]
        twc = jnp.tile(twc_ref[...], (1, FOUR_W // 128))
        tws = jnp.tile(tws_ref[...], (1, FOUR_W // 128))
        lo = pl.ds(pl.multiple_of(j * FFT_S, FFT_S), FFT_S)
        h_ref[lo, :, 0:FOUR_W] = (gr * twc + gi * tws).reshape(FFT_S, FFT_R, FOUR_W)
        h_ref[lo, :, FOUR_W:] = (gi * twc - gr * tws).reshape(FFT_S, FFT_R, FOUR_W)

    @pl.when(j >= nj)
    def _():
        hi = pl.ds(pl.multiple_of((j - nj) * FFT_S, FFT_S), FFT_S)
        h = h_ref[:, hi, :].reshape(nb, 2 * FOUR_W).astype(BF16)
        p = jnp.dot(mc_ref[...], h, preferred_element_type=F32)
        q = jnp.dot(ms_ref[...], h, preferred_element_type=F32)
        ur = p[:, :FOUR_W] + q[:, FOUR_W:]
        ui = p[:, FOUR_W:] - q[:, :FOUR_W]
        out = _fold_groups(ur, ui, wf_ref, 1.0 / math.sqrt(T_LAT * F_CH))
        o_ref[...] = out.reshape(FFT_R, FFT_S, FOUR_W)


def _fft_consts():
    r, s = FFT_R, FFT_S
    nb = r * s
    ang = 2.0 * np.pi * np.outer(np.arange(r), np.arange(r)) / r
    c, sn = np.cos(ang), np.sin(ang)
    eye = np.eye(s)
    ma = np.concatenate([np.einsum('pb,ts->tpbs', c, eye).reshape(nb, nb),
                         np.einsum('pb,ts->tpbs', -sn, eye).reshape(nb, nb)], axis=0)
    mbc = np.einsum('pa,ts->ptas', c, eye).reshape(nb, nb)
    mbs = np.einsum('pa,ts->ptas', sn, eye).reshape(nb, nb)
    tw = 2.0 * np.pi * np.outer(np.arange(r), np.arange(r)).reshape(-1) / (r * r)
    twc = np.broadcast_to(np.cos(tw)[:, None], (r * r, 128))
    tws = np.broadcast_to(np.sin(tw)[:, None], (r * r, 128))
    bf = lambda a: jnp.asarray(a, F32).astype(BF16)
    return bf(ma), bf(mbc), bf(mbs), jnp.asarray(twc, F32), jnp.asarray(tws, F32)


def _four_lat(f, wfold_bf, layer, consts):
    ma, mbc, mbs, twc, tws = consts
    r, s = FFT_R, FFT_S
    nb = r * s
    nj = r // s
    f4 = f.reshape(N_LAT_B, r, r, FOUR_W)
    a_step = lambda j: jnp.minimum(j, nj - 1)
    b_step = lambda j: jnp.maximum(j - nj, 0)
    const = lambda b, j: (0, 0)
    out = pl.pallas_call(
        _fft_body,
        grid=(N_LAT_B, 2 * nj),
        in_specs=[pl.BlockSpec((None, r, s, FOUR_W), lambda b, j: (b, 0, a_step(j), 0)),
                  pl.BlockSpec((2 * nb, nb), const),
                  pl.BlockSpec((nb, 128), lambda b, j: (a_step(j), 0)),
                  pl.BlockSpec((nb, 128), lambda b, j: (a_step(j), 0)),
                  pl.BlockSpec((nb, nb), const),
                  pl.BlockSpec((nb, nb), const),
                  pl.BlockSpec((None, F_GROUPS, 2 * F_CH, F_CH), lambda b, j: (layer, 0, 0, 0))],
        out_specs=pl.BlockSpec((None, r, s, FOUR_W), lambda b, j: (b, 0, b_step(j), 0)),
        out_shape=jax.ShapeDtypeStruct((N_LAT_B, r, r, FOUR_W), F32),
        scratch_shapes=[pltpu.VMEM((r, r, 2 * FOUR_W), F32)],
        compiler_params=_cparams(("arbitrary", "arbitrary")),
        name="fft_lat",
    )(f4, ma, twc, tws, mbc, mbs, wfold_bf)
    return out.reshape(N_LAT_B * T_LAT, FOUR_W)


def _outproj_body(a_ref, f_ref, x_ref, gt_ref, sh_ref, sc_ref, gpost_ref, gpre_ref, wo_ref, *rest,
                  moe, group0, tiles_per_group):
    if moe:
        wr_ref, x1_ref, hn_ref, gates_ref, sel_ref = rest
    else:
        x1_ref, hn_ref = rest
    grp = _group_of(pl.program_id(0), group0, tiles_per_group)
    gt, sh, sc = gt_ref[pl.ds(grp, 1), :], sh_ref[pl.ds(grp, 1), :], sc_ref[pl.ds(grp, 1), :]
    for half in range(TM_OUT // TM):
        rows = slice(half * TM, (half + 1) * TM)
        mix_in = jnp.concatenate([a_ref[rows, :], f_ref[rows, :].astype(BF16)], axis=1)
        mixed = jnp.dot(mix_in, wo_ref[...], preferred_element_type=F32)
        x1 = x_ref[rows, :] + gt * _rms(mixed, gpost_ref[...])
        x1_ref[rows, :] = x1
        hn = _rms(x1, gpre_ref[...]) * (1.0 + sc) + sh
        hn_ref[rows, :] = hn.astype(hn_ref.dtype)
        if moe:
            gates, sel = _route_top2(hn, wr_ref[...])
            gates_ref[rows, :] = gates
            sel_ref[rows, :] = sel.astype(sel_ref.dtype)


def _route_top2(hn, wr):
    wr_hi = wr.astype(BF16)
    wr_lo = (wr - wr_hi.astype(F32)).astype(BF16)
    hn_hi = hn.astype(BF16)
    hn_lo = (hn - hn_hi.astype(F32)).astype(BF16)
    hh = jnp.dot(hn_hi, jnp.concatenate([wr_hi, wr_lo], axis=1), preferred_element_type=F32)
    logits = hh[:, :E_PAD] + hh[:, E_PAD:] + jnp.dot(hn_lo, wr_hi, preferred_element_type=F32)
    lane = lax.broadcasted_iota(jnp.int32, logits.shape, 1).astype(F32)
    neg = jnp.float32(-jnp.inf)
    logits = jnp.where(lane < float(N_EXPERTS), logits, neg)
    m1 = jnp.max(logits, axis=-1, keepdims=True)
    i1 = jnp.min(jnp.where(logits == m1, lane, float(E_PAD)), axis=-1, keepdims=True)
    rest_l = jnp.where(lane == i1, neg, logits)
    m2 = jnp.max(rest_l, axis=-1, keepdims=True)
    i2 = jnp.min(jnp.where(rest_l == m2, lane, float(E_PAD)), axis=-1, keepdims=True)
    e2 = jnp.exp(m2 - m1)
    w1 = 1.0 / (1.0 + e2)
    w2 = e2 / (1.0 + e2)
    gates = jnp.where(lane == i1, w1, 0.0) + jnp.where(lane == i2, w2, 0.0)
    sel = jnp.where((lane == i1) | (lane == i2), 1.0, 0.0)
    return gates, sel


def _outproj(attn_o, four_o, x, mod, layer, g_post, g_ffn_pre, w_out_bf, w_router_pad, group0, tiles_per_group):
    rows = x.shape[0]
    moe = w_router_pad is not None
    row = lambda i: (i, 0)
    const = lambda i: (0, 0)
    modspec = lambda k: pl.BlockSpec((None, N_GROUPS_PAD, D), lambda i: (layer, 0, k))
    blocks_per_group = None if tiles_per_group is None else tiles_per_group * TM // TM_OUT
    in_specs = [pl.BlockSpec((TM_OUT, ATTN_W), row), pl.BlockSpec((TM_OUT, FOUR_W), row),
                pl.BlockSpec((TM_OUT, D), row),
                modspec(2), modspec(3), modspec(4),
                pl.BlockSpec((1, D), const), pl.BlockSpec((1, D), const), pl.BlockSpec((D, D), const)]
    args = [attn_o, four_o, x, mod, mod, mod, g_post.reshape(1, D), g_ffn_pre.reshape(1, D), w_out_bf]
    out_specs = [pl.BlockSpec((TM_OUT, D), row), pl.BlockSpec((TM_OUT, D), row)]
    out_shape = [jax.ShapeDtypeStruct((rows, D), F32), jax.ShapeDtypeStruct((rows, D), BF16)]
    if moe:
        in_specs.append(pl.BlockSpec((D, E_PAD), const))
        args.append(w_router_pad)
        out_specs += [pl.BlockSpec((TM_OUT, E_PAD), row)] * 2
        out_shape += [jax.ShapeDtypeStruct((rows, E_PAD), F32), jax.ShapeDtypeStruct((rows, E_PAD), BF16)]
    return pl.pallas_call(
        functools.partial(_outproj_body, moe=moe, group0=group0, tiles_per_group=blocks_per_group),
        grid=(rows // TM_OUT,),
        in_specs=in_specs, out_specs=out_specs, out_shape=out_shape,
        compiler_params=_cparams(("arbitrary",)),
        name="outproj_moe" if moe else "outproj",
    )(*args)


def _swiglu(hn, wg, wu, wd):
    g = jnp.dot(hn, wg, preferred_element_type=F32)
    u = jnp.dot(hn, wu, preferred_element_type=F32)
    return jnp.dot((_silu(g) * u).astype(BF16), wd, preferred_element_type=F32)


def _ffn_dense_body(hn_ref, x1_ref, gt_ref, gpost_ref, wg_ref, wu_ref, wd_ref, o_ref, *, group0, tiles_per_group):
    grp = _group_of(pl.program_id(0), group0, tiles_per_group)
    ff = _swiglu(hn_ref[...], wg_ref[...], wu_ref[...], wd_ref[...])
    o_ref[...] = x1_ref[...] + gt_ref[pl.ds(grp, 1), :] * _rms(ff, gpost_ref[...])


def _ffn_dense(hn, x1, mod, layer, g_post, wg, wu, wd, group0, tiles_per_group):
    rows = x1.shape[0]
    row = lambda i: (i, 0)
    const = lambda i: (0, 0)
    resident = lambda shape: pl.BlockSpec(shape, const, pipeline_mode=pl.Buffered(1))
    return pl.pallas_call(
        functools.partial(_ffn_dense_body, group0=group0, tiles_per_group=tiles_per_group),
        grid=(rows // TM,),
        in_specs=[pl.BlockSpec((TM, D), row), pl.BlockSpec((TM, D), row),
                  pl.BlockSpec((None, N_GROUPS_PAD, D), lambda i: (layer, 0, 5)),
                  pl.BlockSpec((1, D), const),
                  resident((D, D_FF)), resident((D, D_FF)), resident((D_FF, D))],
        out_specs=pl.BlockSpec((TM, D), row),
        out_shape=jax.ShapeDtypeStruct((rows, D), F32),
        compiler_params=_cparams(("arbitrary",)),
        name="ffn_dense",
    )(hn, x1, mod, g_post.reshape(1, D), wg, wu, wd)


def _experts_body(te_ref, tn_ref, xs_ref, wg_ref, wu_ref, wd_ref, ys_ref, wg_bf, wu_bf, wd_bf):
    i = pl.program_id(0)
    fresh = (i == 0) | (te_ref[i] != te_ref[jnp.maximum(i - 1, 0)])

    @pl.when(fresh)
    def _():
        wg_bf[...] = wg_ref[...].astype(BF16)
        wu_bf[...] = wu_ref[...].astype(BF16)
        wd_bf[...] = wd_ref[...].astype(BF16)

    n_real = tn_ref[i]

    @pl.when(n_real > 0)
    def _():
        ys_ref[...] = _swiglu(xs_ref[...].astype(BF16), wg_bf[...], wu_bf[...], wd_bf[...])

    @pl.when(n_real == 0)
    def _():
        ys_ref[...] = jnp.zeros_like(ys_ref)


def _experts(xs, tile_e, tile_n, eg, eu, ed):
    tile = lambda i, te, tn: (i, 0)
    wspec = lambda shape: pl.BlockSpec((None,) + shape, lambda i, te, tn: (te[i], 0, 0))
    return pl.pallas_call(
        _experts_body,
        grid_spec=pltpu.PrefetchScalarGridSpec(
            num_scalar_prefetch=2,
            grid=(N_TILES,),
            in_specs=[pl.BlockSpec((TMOE, D), tile), wspec((D, D_FF_E)), wspec((D, D_FF_E)), wspec((D_FF_E, D))],
            out_specs=pl.BlockSpec((TMOE, D), tile),
            scratch_shapes=[pltpu.VMEM((D, D_FF_E), BF16), pltpu.VMEM((D, D_FF_E), BF16),
                            pltpu.VMEM((D_FF_E, D), BF16)]),
        out_shape=jax.ShapeDtypeStruct((NP, D), F32),
        compiler_params=_cparams(("arbitrary",)),
        name="moe_experts",
    )(tile_e, tile_n, xs, eg, eu, ed)


def _route_body(*refs, firsts):
    ns = len(firsts)
    gates_refs, sel_refs = refs[:ns], refs[ns:2 * ns]
    ltri_ref, w_ref, rho_ref, tmeta_ref, meta_ref, cnt_ref, off_ref, run_ref = refs[2 * ns:]
    phase, t = pl.program_id(0), pl.program_id(1)

    def this_stream(stream_refs):
        val = stream_refs[0][...]
        for k in range(1, ns):
            val = jnp.where(t >= firsts[k], stream_refs[k][...], val)
        return val

    sel = this_stream(sel_refs)
    group_rows = jnp.ceil(jnp.sum(sel.astype(F32), axis=0, keepdims=True) * (1.0 / GROUP)) * GROUP

    @pl.when((phase == 0) & (t == 0))
    def _():
        cnt_ref[...] = jnp.zeros_like(cnt_ref)

    @pl.when(phase == 0)
    def _():
        cnt_ref[...] += group_rows

    @pl.when((phase == 1) & (t == 0))
    def _():
        off_ref[...] = _segment_meta(cnt_ref[...], meta_ref)
        run_ref[...] = jnp.zeros_like(run_ref)

    @pl.when(phase == 1)
    def _():
        lane1 = lax.broadcasted_iota(jnp.int32, group_rows.shape, 1).astype(F32)
        buf0 = jnp.zeros_like(group_rows)
        running = jnp.zeros((1, 1), F32)
        for e in range(N_EXPERTS):
            pick = lane1 == float(e)
            buf0 = jnp.where(pick, running, buf0)
            running = running + jnp.sum(jnp.where(pick, group_rows, 0.0), axis=-1, keepdims=True)
        slot0 = off_ref[...] + run_ref[...]
        run_ref[...] += group_rows
        row8 = lax.broadcasted_iota(jnp.int32, tmeta_ref.shape, 0)
        tmeta_ref[...] = jnp.where(row8 == 0, slot0, jnp.where(row8 == 1, group_rows,
                                   jnp.where(row8 == 2, buf0, 0.0)))

        earlier = jnp.dot(ltri_ref[...], sel, preferred_element_type=F32)
        rho = earlier + buf0
        chosen = sel > 0
        r_lo = jnp.min(jnp.where(chosen, rho, float(R_TILE)), axis=-1, keepdims=True)
        r_hi = jnp.max(jnp.where(chosen, rho, -1.0), axis=-1, keepdims=True)
        g = this_stream(gates_refs)
        w_lo = jnp.sum(jnp.where(chosen & (rho == r_lo), g, 0.0), axis=-1, keepdims=True)
        w_hi = jnp.sum(jnp.where(chosen & (rho == r_hi), g, 0.0), axis=-1, keepdims=True)
        lane = lax.broadcasted_iota(jnp.int32, rho.shape, 1)
        w_ref[...] = jnp.where(lane == 2, w_lo, jnp.where(lane == 3, w_hi,
                               jnp.where(lane == 4, r_lo, jnp.where(lane == 5, r_hi, 0.0))))
        cols = jnp.where(lane == 0, r_lo, jnp.where(lane == 1, r_hi, 0.0))
        rho_ref[...] = jnp.transpose(cols)[0:N_GROUPS_PAD, :]


def _segment_meta(cnt, meta_ref):
    padded = jnp.ceil(cnt * (1.0 / TMOE)) * TMOE
    lane = lax.broadcasted_iota(jnp.int32, cnt.shape, 1).astype(F32)
    start = lane * TMOE
    off = jnp.zeros_like(cnt)
    tile_e = jnp.zeros_like(cnt)
    tile_n = jnp.zeros_like(cnt)
    running = jnp.zeros((1, 1), F32)
    for e in range(N_EXPERTS):
        pick = lane == float(e)
        cnt_e = jnp.sum(jnp.where(pick, cnt, 0.0), axis=-1, keepdims=True)
        pad_e = jnp.sum(jnp.where(pick, padded, 0.0), axis=-1, keepdims=True)
        off = jnp.where(pick, running, off)
        in_seg = (start >= running) & (start < running + pad_e)
        tile_e = jnp.where(in_seg, float(e), tile_e)
        tile_n = jnp.where(in_seg, jnp.clip(cnt_e - (start - running), 0.0, float(TMOE)), tile_n)
        running = running + pad_e
    tile_e = jnp.where(start >= running, float(N_EXPERTS - 1), tile_e)
    row = lax.broadcasted_iota(jnp.int32, meta_ref.shape, 0)
    last_tile = jnp.where(lane < float(N_EXPERTS), off + padded - TMOE, running)
    last_tile = jnp.where((lane < float(N_EXPERTS)) & (padded == 0.0), -1.0, last_tile)
    meta_ref[...] = jnp.where(row == 0, tile_e, jnp.where(row == 1, tile_n,
                              jnp.where(row == 2, last_tile, 0.0)))
    return off


def _route(gates, sels):
    tiles = [g.shape[0] // TM for g in gates]
    firsts = [int(f) for f in np.cumsum([0] + tiles)]
    n, rows = firsts[-1], firsts[-1] * TM
    ltri = jnp.asarray(np.tril(np.ones((TM, TM)), -1), F32).astype(BF16)
    stream_spec = lambda lo, cnt: pl.BlockSpec((TM, E_PAD), lambda p, t: (jnp.clip(t - lo, 0, cnt - 1), 0))
    stream_specs = [stream_spec(lo, cnt) for lo, cnt in zip(firsts[:-1], tiles)]
    per_tile = lambda p, t: (t * p, 0, 0)
    return pl.pallas_call(
        functools.partial(_route_body, firsts=tuple(firsts[:-1])),
        grid=(2, n),
        in_specs=stream_specs + stream_specs + [pl.BlockSpec((TM, TM), lambda p, t: (0, 0))],
        out_specs=[pl.BlockSpec((TM, E_PAD), lambda p, t: (t * p, 0)),
                   pl.BlockSpec((None, N_GROUPS_PAD, TM), per_tile),
                   pl.BlockSpec((None, N_GROUPS_PAD, E_PAD), per_tile),
                   pl.BlockSpec((N_GROUPS_PAD, E_PAD), lambda p, t: (0, 0))],
        out_shape=[jax.ShapeDtypeStruct((rows, E_PAD), F32),
                   jax.ShapeDtypeStruct((n, N_GROUPS_PAD, TM), F32),
                   jax.ShapeDtypeStruct((n, N_GROUPS_PAD, E_PAD), F32),
                   jax.ShapeDtypeStruct((N_GROUPS_PAD, E_PAD), F32)],
        scratch_shapes=[pltpu.VMEM((1, E_PAD), F32)] * 3,
        compiler_params=_cparams(("arbitrary", "arbitrary")),
        name="moe_route",
    )(*gates, *sels, ltri)


def _group_copies(tmeta_ref, slot_window, buf_window, sem, to_slots, act):
    for e in range(N_EXPERTS):
        slot0, n, buf0 = tmeta_ref[0, e], tmeta_ref[1, e], tmeta_ref[2, e]
        done = jnp.int32(0)
        size = TM
        while size >= GROUP:
            take = (n - done) >= size
            s_at = pl.ds(pl.multiple_of(slot0 + done, GROUP), size)
            b_at = pl.ds(pl.multiple_of(buf0 + done, GROUP), size)
            src, dst = (buf_window(b_at), slot_window(s_at)) if to_slots else (slot_window(s_at), buf_window(b_at))

            @pl.when(take)
            def _(src=src, dst=dst):
                act(pltpu.make_async_copy(src, dst, sem))

            done = done + jnp.where(take, size, 0)
            size //= 2


def _dispatch_body(tmeta_ref, pad_ref, rho_ref, *rest, firsts):
    ns = len(firsts)
    hn_refs, (xs_ref, pbuf, zeros, sem) = rest[:ns], rest[ns:]
    i = pl.program_id(0)

    @pl.when(i == 0)
    def _():
        zeros[...] = jnp.zeros_like(zeros)
        fills = []
        for e in range(N_EXPERTS):
            s = pad_ref[0, e]
            fills.append((s >= 0, pl.multiple_of(jnp.maximum(s, 0), TMOE)))
        for j in range(N_TILES - TOP_K * N_TOKENS // TMOE):
            s = NP - (j + 1) * TMOE
            fills.append((s >= pad_ref[0, N_EXPERTS], s))
        fill = lambda s: pltpu.make_async_copy(zeros, xs_ref.at[pl.ds(s, TMOE), :], sem)
        for live, s in fills:
            @pl.when(live)
            def _(s=s):
                fill(s).start()
        for live, s in fills:
            @pl.when(live)
            def _(s=s):
                fill(s).wait()

    hn = hn_refs[0][...]
    for k in range(1, ns):
        hn = jnp.where(i >= firsts[k], hn_refs[k][...], hn)
    row = lax.broadcasted_iota(jnp.int32, (R_TILE, TM), 0).astype(F32)
    onehot = jnp.where((row == rho_ref[0:1, :]) | (row == rho_ref[1:2, :]), 1.0, 0.0).astype(BF16)
    pbuf[...] = jnp.dot(onehot, hn, preferred_element_type=F32)
    windows = (lambda at: xs_ref.at[at, :], lambda at: pbuf.at[at, :])
    _group_copies(tmeta_ref, *windows, sem, True, lambda c: c.start())
    _group_copies(tmeta_ref, *windows, sem, True, lambda c: c.wait())


def _dispatch(hns, tmeta, rho, pad_starts):
    tiles = [h.shape[0] // TM for h in hns]
    firsts = [int(f) for f in np.cumsum([0] + tiles)]
    n = firsts[-1]
    stream_spec = lambda lo, cnt: pl.BlockSpec((TM, D), lambda i: (jnp.clip(i - lo, 0, cnt - 1), 0))
    return pl.pallas_call(
        functools.partial(_dispatch_body, firsts=tuple(firsts[:-1])),
        grid=(n,),
        in_specs=[pl.BlockSpec((None, N_GROUPS_PAD, E_PAD), lambda i: (i, 0, 0), memory_space=pltpu.SMEM),
                  pl.BlockSpec(memory_space=pltpu.SMEM),
                  pl.BlockSpec((None, N_GROUPS_PAD, TM), lambda i: (i, 0, 0))]
                 + [stream_spec(lo, cnt) for lo, cnt in zip(firsts[:-1], tiles)],
        out_specs=pl.BlockSpec(memory_space=pl.ANY),
        out_shape=jax.ShapeDtypeStruct((NP, D), F32),
        scratch_shapes=[pltpu.VMEM((R_TILE, D), F32), pltpu.VMEM((TMOE, D), F32), pltpu.SemaphoreType.DMA],
        compiler_params=_cparams(("arbitrary",)),
        name="moe_dispatch",
    )(tmeta, pad_starts, rho, *hns)


def _combine_body(tmeta_ref, next_tmeta_ref, w_ref, x1_ref, gt_ref, gpost_ref, ys_ref, o_ref, ybuf, sems,
                   *, group0, tiles_per_group):
    i = pl.program_id(0)
    grp = _group_of(i, group0, tiles_per_group)
    cur = lax.rem(i, 2)

    def fetch(meta_ref, half, act):
        _group_copies(meta_ref, lambda at: ys_ref.at[at, :], lambda at: ybuf.at[half, at, :], sems.at[half],
                      False, act)

    @pl.when(i == 0)
    def _():
        ybuf[...] = jnp.zeros_like(ybuf)
        fetch(tmeta_ref, 0, lambda c: c.start())

    @pl.when(i + 1 < pl.num_programs(0))
    def _():
        fetch(next_tmeta_ref, 1 - cur, lambda c: c.start())

    fetch(tmeta_ref, cur, lambda c: c.wait())
    w = w_ref[...]
    lane = lax.broadcasted_iota(jnp.int32, w.shape, 1)
    col = lambda j: jnp.sum(jnp.where(lane == j, w, 0.0), axis=-1, keepdims=True)
    w_lo, w_hi, r_lo, r_hi = col(2), col(3), col(4), col(5)
    y = ybuf[cur].astype(BF16)
    pos = lax.broadcasted_iota(jnp.int32, (TM, R_TILE), 1).astype(F32)
    pick = lambda r: jnp.dot(jnp.where(pos == r, 1.0, 0.0).astype(BF16), y, preferred_element_type=F32)
    ff = w_lo * pick(r_lo) + w_hi * pick(r_hi)
    o_ref[...] = x1_ref[...] + gt_ref[pl.ds(grp, 1), :] * _rms(ff, gpost_ref[...])


def _combine(ys, tmeta, slot_w, tile0, x1, mod, layer, g_post, group0, tiles_per_group):
    rows = x1.shape[0]
    n = rows // TM
    row = lambda i: (i, 0)
    return pl.pallas_call(
        functools.partial(_combine_body, group0=group0, tiles_per_group=tiles_per_group),
        grid=(n,),
        in_specs=[pl.BlockSpec((None, N_GROUPS_PAD, E_PAD), lambda i: (tile0 + i, 0, 0), memory_space=pltpu.SMEM),
                  pl.BlockSpec((None, N_GROUPS_PAD, E_PAD), lambda i: (tile0 + jnp.minimum(i + 1, n - 1), 0, 0),
                               memory_space=pltpu.SMEM),
                  pl.BlockSpec((TM, E_PAD), lambda i: (tile0 + i, 0)), pl.BlockSpec((TM, D), row),
                  pl.BlockSpec((None, N_GROUPS_PAD, D), lambda i: (layer, 0, 5)),
                  pl.BlockSpec((1, D), lambda i: (0, 0)),
                  pl.BlockSpec(memory_space=pl.ANY)],
        out_specs=pl.BlockSpec((TM, D), row),
        out_shape=jax.ShapeDtypeStruct((rows, D), F32),
        scratch_shapes=[pltpu.VMEM((2, R_TILE, D), F32), pltpu.SemaphoreType.DMA((2,))],
        compiler_params=_cparams(("arbitrary",)),
        name="moe_combine",
    )(tmeta, tmeta, slot_w, x1, mod, g_post.reshape(1, D), ys)


def kernel(x_prompt, x_sample, cache_k, cache_v, c, c_ctx, g_attn_pre, g_attn_post, g_ffn_pre, g_ffn_post,
           w_ada, b_ada, w_in, lam_params, g_subln, w_fnet, w_out, w_gate, w_up, w_down, w_router,
           e_gate, e_up, e_down):
    assert x_prompt.shape == (N_CTX_B, T_CTX, D) and x_sample.shape == (N_LAT_B, T_LAT, D)
    assert cache_k.shape == (N_LAT_B, DEPTH, T_PAST, N_HEADS, 2, QK_DIM)

    cond = jnp.zeros((N_GROUPS_PAD, D), F32).at[0].set(c_ctx).at[1:1 + N_LAT_B].set(c)
    mod = _ada(cond, w_ada, b_ada)
    wfold_bf = _wfold(w_fnet).astype(BF16)
    rope_tabs = _rope_tables()
    fft_consts = _fft_consts()

    w_in_bf = w_in.astype(BF16)
    w_out_bf = w_out.astype(BF16)
    w_router_pad = jnp.zeros((DEPTH // 2, D, E_PAD), F32).at[:, :, :N_EXPERTS].set(w_router)
    ck_bf = cache_k.reshape(N_LAT_B, DEPTH, T_PAST, ATTN_W).astype(BF16)
    cv_bf = cache_v.reshape(N_LAT_B, DEPTH, T_PAST, ATTN_W).astype(BF16)

    lat_tiles = T_LAT // TM
    streams = [dict(x=x_prompt.reshape(N_CTX_B * T_CTX, D), group0=0, tiles=None, lat=False),
               dict(x=x_sample.reshape(N_LAT_B * T_LAT, D), group0=1, tiles=lat_tiles, lat=True)]
    new_k = new_v = None
    for l in range(DEPTH):
        lam_init = 0.8 - 0.6 * math.exp(-0.3 * l)
        i = l // 2
        routed = []
        for s in streams:
            g0, tiles = s["group0"], s["tiles"]
            if s["lat"]:
                qkv, f = _inproj(s["x"], mod, l, g_attn_pre[l], w_in_bf[l], rope_tabs, g0, tiles)
                attn_o = _attn_lat(qkv, ck_bf[:, l], cv_bf[:, l], lam_params, g_subln, l, lam_init)
                attn_o = attn_o.reshape(N_LAT_B * T_LAT, ATTN_W)
                four_o = _four_lat(f, wfold_bf, l, fft_consts)
            else:
                qkv, f, new_k, new_v = _inproj(s["x"], mod, l, g_attn_pre[l], w_in_bf[l], None, g0, tiles,
                                               cache=None if l == 0 else (new_k, new_v))
                attn_o = _attn_ctx(qkv, lam_params, g_subln, l, lam_init).reshape(N_CTX_B * T_CTX, ATTN_W)
                four_o = _four_ctx(f, wfold_bf, l).reshape(N_CTX_B * T_CTX, FOUR_W)
            if l % 2 == 0:
                x1, hn = _outproj(attn_o, four_o, s["x"], mod, l, g_attn_post[l], g_ffn_pre[l], w_out_bf[l],
                                  None, g0, tiles)
                s["x"] = _ffn_dense(hn, x1, mod, l, g_ffn_post[l], w_gate[i].astype(BF16), w_up[i].astype(BF16),
                                    w_down[i].astype(BF16), g0, tiles)
            else:
                routed.append(_outproj(attn_o, four_o, s["x"], mod, l, g_attn_post[l], g_ffn_pre[l],
                                       w_out_bf[l], w_router_pad[i], g0, tiles))
        if l % 2 == 1:
            slot_w, rho, tmeta, meta = _route([r[2] for r in routed], [r[3] for r in routed])
            tmeta = tmeta.astype(jnp.int32)
            tile_e = meta[0, :N_TILES].astype(jnp.int32)
            tile_n = meta[1, :N_TILES].astype(jnp.int32)
            pad_starts = meta[2:3, :N_EXPERTS + 1].astype(jnp.int32)
            bounds = np.cumsum([0] + [r[0].shape[0] for r in routed])
            xs = _dispatch([r[1] for r in routed], tmeta, rho, pad_starts)
            ys = _experts(xs, tile_e, tile_n, e_gate[i], e_up[i], e_down[i])
            for s, r, lo in zip(streams, routed, bounds[:-1]):
                s["x"] = _combine(ys, tmeta, slot_w, int(lo) // TM, r[0], mod, l, g_ffn_post[l],
                                   s["group0"], s["tiles"])

    y_prompt = streams[0]["x"].reshape(N_CTX_B, T_CTX, D)
    y_sample = streams[1]["x"].reshape(N_LAT_B, T_LAT, D)
    new_cache_k = new_k.reshape(N_CTX_B, DEPTH, T_CTX, N_HEADS, 2, QK_DIM)
    new_cache_v = new_v.reshape(N_CTX_B, DEPTH, T_CTX, N_HEADS, HEAD_W)
    return (y_prompt, y_sample, new_cache_k, new_cache_v)
```

```python
import functools
import math

import numpy as np
import jax
import jax.numpy as jnp
from jax import lax
from jax.experimental import pallas as pl
from jax.experimental.pallas import tpu as pltpu

F32 = jnp.float32
BF16 = jnp.bfloat16

D = 1024
N_CTX_B = 16
T_CTX = 256
N_LAT_B = 2
T_LAT = 4096
T_PAST = 512
DEPTH = 2
GRID_W = 64
N_HEADS = 4
QK_DIM = 64
HEAD_W = 2 * QK_DIM
ATTN_W = N_HEADS * HEAD_W
F_GROUPS = 4
F_CH = 128
FOUR_W = F_GROUPS * F_CH
IN_W = 3 * ATTN_W + FOUR_W
QKV_W = 3 * ATTN_W
D_FF = 2816
N_EXPERTS = 8
D_FF_E = 1024
N_MOD = 6
EPS = 1e-6
ROPE_THETA = 10000.0
N_GROUPS_PAD = 8
E_PAD = 128

TM = 512
TM_OUT = 1024
TOP_K = 2
TMOE = 512
N_TOKENS = N_CTX_B * T_CTX + N_LAT_B * T_LAT
GROUP = 8
R_TILE = TOP_K * TM + N_EXPERTS * GROUP
N_TILES = -(-(TOP_K * N_TOKENS + (N_TOKENS // TM) * N_EXPERTS * (GROUP - 1)) // TMOE) + N_EXPERTS
NP = N_TILES * TMOE
TQ = 512
Q_SUB = 2
TK = 1536
ATTN_UNROLL = 3
FFT_R = 64
FFT_S = 8
VMEM_LIMIT = 56 * 1024 * 1024


def _cparams(sem):
    return pltpu.CompilerParams(dimension_semantics=sem, vmem_limit_bytes=VMEM_LIMIT)


def _rms(x, g):
    ms = jnp.mean(x * x, axis=-1, keepdims=True)
    return x * lax.rsqrt(ms + EPS) * g


def _silu(x):
    return x * (1.0 / (1.0 + jnp.exp(-x)))


def _ada_body(c_ref, w_ref, b_ref, o_ref):
    s = _silu(c_ref[...])
    o_ref[...] = jnp.dot(s, w_ref[...], preferred_element_type=F32,
                         precision=lax.Precision.HIGHEST) + b_ref[...]


def _ada(cond, w_ada, b_ada):
    tn = 1536
    return pl.pallas_call(
        _ada_body,
        grid=(DEPTH, N_MOD * D // tn),
        in_specs=[pl.BlockSpec((N_GROUPS_PAD, D), lambda l, j: (0, 0)),
                  pl.BlockSpec((None, D, tn), lambda l, j: (l, 0, j)),
                  pl.BlockSpec((None, 1, tn), lambda l, j: (l, 0, j))],
        out_specs=pl.BlockSpec((None, N_GROUPS_PAD, tn), lambda l, j: (l, 0, j)),
        out_shape=jax.ShapeDtypeStruct((DEPTH, N_GROUPS_PAD, N_MOD * D), F32),
        compiler_params=_cparams(("arbitrary", "arbitrary")),
        name="ada_mod",
    )(cond, w_ada, b_ada.reshape(DEPTH, 1, N_MOD * D))


def _wfold_body(cc_ref, sc_ref, w_ref, o_ref):
    w = w_ref[...]
    o_ref[0:F_CH, :] = jnp.dot(cc_ref[...], w, preferred_element_type=F32,
                               precision=lax.Precision.HIGHEST)
    o_ref[F_CH:2 * F_CH, :] = jnp.dot(sc_ref[...], w, preferred_element_type=F32,
                                      precision=lax.Precision.HIGHEST)


def _wfold(w_fnet):
    ang = 2.0 * np.pi * np.outer(np.arange(F_CH), np.arange(F_CH)) / F_CH
    cc = jnp.asarray(np.cos(ang), F32)
    sc = jnp.asarray(np.sin(ang), F32)
    cspec = pl.BlockSpec((F_CH, F_CH), lambda l, g: (0, 0))
    return pl.pallas_call(
        _wfold_body,
        grid=(DEPTH, F_GROUPS),
        in_specs=[cspec, cspec, pl.BlockSpec((None, None, F_CH, F_CH), lambda l, g: (l, g, 0, 0))],
        out_specs=pl.BlockSpec((None, None, 2 * F_CH, F_CH), lambda l, g: (l, g, 0, 0)),
        out_shape=jax.ShapeDtypeStruct((DEPTH, F_GROUPS, 2 * F_CH, F_CH), F32),
        compiler_params=_cparams(("arbitrary", "arbitrary")),
        name="fnet_fold",
    )(cc, sc, w_fnet)


def _group_of(i, group0, tiles_per_group):
    if tiles_per_group is None:
        return group0
    return group0 + lax.div(i, jnp.int32(tiles_per_group))


def _inproj_body(x_ref, sh_ref, sc_ref, g_ref, w_ref, *rest, rope, cache_layer, group0, tiles_per_group):
    if rope:
        cos_ref, sin_ref, qkv_ref, f_ref = rest
    else:
        qkv_ref, f_ref, k32_ref, v32_ref = rest[-4:]
    grp = _group_of(pl.program_id(0), group0, tiles_per_group)
    sh = sh_ref[pl.ds(grp, 1), :]
    sc = sc_ref[pl.ds(grp, 1), :]
    hn = _rms(x_ref[...], g_ref[...]) * (1.0 + sc) + sh
    proj = jnp.dot(hn.astype(BF16), w_ref[...], preferred_element_type=F32)
    qk = proj[:, :2 * ATTN_W]
    v = proj[:, 2 * ATTN_W:QKV_W]
    if rope:
        cos = jnp.tile(cos_ref[...], (1, 2 * ATTN_W // HEAD_W))
        sin = jnp.tile(sin_ref[...], (1, 2 * ATTN_W // HEAD_W))
        lane = lax.broadcasted_iota(jnp.int32, qk.shape, 1)
        low = (lane % 32) < 16
        rot = jnp.where(low, pltpu.roll(qk, 2 * ATTN_W - 16, 1), pltpu.roll(qk, 16, 1))
        qk = qk * cos + rot * sin
    else:
        for ref, val in ((k32_ref, qk[:, ATTN_W:]), (v32_ref, v)):
            val = val.reshape(TM // T_CTX, T_CTX, ATTN_W)
            if cache_layer == 0:
                ref[:, 0] = val
                ref[:, 1:] = jnp.zeros((TM // T_CTX, DEPTH - 1, T_CTX, ATTN_W), F32)
            else:
                ref[...] = val
    qkv_ref[:, 0:ATTN_W] = (qk[:, :ATTN_W] * (QK_DIM ** -0.5 * math.log2(math.e))).astype(BF16)
    qkv_ref[:, ATTN_W:2 * ATTN_W] = qk[:, ATTN_W:].astype(BF16)
    qkv_ref[:, 2 * ATTN_W:] = v.astype(BF16)
    f_ref[...] = proj[:, QKV_W:]


def _inproj(x, mod, layer, g_pre, w_in_bf, rope_tabs, group0, tiles_per_group, cache=None):
    rows = x.shape[0]
    rope = rope_tabs is not None
    aliases = {}
    row = lambda i: (i, 0)
    const = lambda i: (0, 0)
    in_specs = [pl.BlockSpec((TM, D), row),
                pl.BlockSpec((None, N_GROUPS_PAD, D), lambda i: (layer, 0, 0)),
                pl.BlockSpec((None, N_GROUPS_PAD, D), lambda i: (layer, 0, 1)),
                pl.BlockSpec((1, D), const),
                pl.BlockSpec((D, IN_W), const)]
    args = [x, mod, mod, g_pre.reshape(1, D), w_in_bf]
    out_specs = [pl.BlockSpec((TM, QKV_W), row), pl.BlockSpec((TM, FOUR_W), row)]
    out_shape = [jax.ShapeDtypeStruct((rows, QKV_W), BF16), jax.ShapeDtypeStruct((rows, FOUR_W), F32)]
    if rope:
        tiles_per_seq = T_LAT // TM
        tab = pl.BlockSpec((TM, HEAD_W), lambda i: (i % tiles_per_seq, 0))
        in_specs += [tab, tab]
        args += list(rope_tabs)
    else:
        nb = TM // T_CTX
        if cache is None:
            cache_spec = pl.BlockSpec((nb, DEPTH, T_CTX, ATTN_W), lambda i: (i, 0, 0, 0))
        else:
            cache_spec = pl.BlockSpec((nb, None, T_CTX, ATTN_W), lambda i: (i, layer, 0, 0))
            aliases = {len(args): 2, len(args) + 1: 3}
            in_specs += [pl.BlockSpec(memory_space=pl.ANY)] * 2
            args += list(cache)
        out_specs += [cache_spec] * 2
        out_shape += [jax.ShapeDtypeStruct((N_CTX_B, DEPTH, T_CTX, ATTN_W), F32)] * 2
    return pl.pallas_call(
        functools.partial(_inproj_body, rope=rope, cache_layer=None if rope else (0 if cache is None else layer),
                          group0=group0, tiles_per_group=tiles_per_group),
        grid=(rows // TM,),
        in_specs=in_specs, out_specs=out_specs, out_shape=out_shape,
        input_output_aliases=aliases,
        compiler_params=_cparams(("arbitrary",)),
        name="inproj_lat" if rope else "inproj_ctx",
    )(*args)


def _rope_tables():
    half = QK_DIM // 2
    inv = 1.0 / (ROPE_THETA ** (np.arange(0, half, 2, dtype=np.float64) / half))
    pos = np.arange(T_LAT)
    def tab(p):
        ang = p[:, None].astype(np.float64) * inv[None, :]
        return np.concatenate([ang, ang], axis=-1)
    ang = np.concatenate([tab(pos // GRID_W), tab(pos % GRID_W)], axis=-1)
    sign = np.where((np.arange(QK_DIM) % 32) < 16, -1.0, 1.0)
    cos = np.tile(np.cos(ang), (1, 2))
    sin = np.tile(np.sin(ang) * sign[None, :], (1, 2))
    return jnp.asarray(cos, F32), jnp.asarray(sin, F32)


def _lam(lam_ref, lam_init):
    lp = lam_ref[...]
    return (jnp.exp(jnp.sum(lp[0:1] * lp[1:2], keepdims=True))
            - jnp.exp(jnp.sum(lp[2:3] * lp[3:4], keepdims=True)) + lam_init)


def _stack_maps(q):
    lane = lax.broadcasted_iota(jnp.int32, q.shape, 1)
    zero = jnp.zeros_like(q)
    return jnp.concatenate([jnp.where(lane < QK_DIM, q, zero), jnp.where(lane >= QK_DIM, q, zero)], axis=0)


def _softmax_step(qq, kb, vb, carry):
    m, l, acc = carry
    s = lax.dot_general(qq, kb, (((1,), (1,)), ((), ())), preferred_element_type=F32)
    m_new = jnp.maximum(m, jnp.max(s, axis=-1, keepdims=True))
    alpha = jnp.exp2(m - m_new)
    p = jnp.exp2(s - m_new)
    l = alpha * l + jnp.sum(p, axis=-1, keepdims=True)
    acc = alpha * acc + jnp.dot(p.astype(BF16), vb, preferred_element_type=F32)
    return m_new, l, acc


def _softmax_init(rows):
    return (jnp.full((rows, 1), -jnp.inf, F32), jnp.zeros((rows, 1), F32), jnp.zeros((rows, HEAD_W), F32))


def _diff_out(carry, tq, lam, gs, lam_init):
    _, l, acc = carry
    o = acc / l
    a = o[:tq] - lam * o[tq:]
    return _rms(a, gs) * (1.0 - lam_init)


def _attn_ctx_body(lam_ref, gs_ref, q_ref, k_ref, v_ref, o_ref, *, lam_init):
    lam = _lam(lam_ref, lam_init)
    for h in range(N_HEADS):
        sl = slice(h * HEAD_W, (h + 1) * HEAD_W)
        carry = _softmax_step(_stack_maps(q_ref[:, sl]), k_ref[:, sl], v_ref[:, sl], _softmax_init(2 * T_CTX))
        o_ref[:, sl] = _diff_out(carry, T_CTX, lam, gs_ref[...], lam_init).astype(o_ref.dtype)


def _attn_ctx(qkv, lam_params, g_subln, layer, lam_init):
    qkv3 = qkv.reshape(N_CTX_B, T_CTX, QKV_W)
    blk = lambda part: pl.BlockSpec((None, T_CTX, ATTN_W), lambda b: (b, 0, part))
    return pl.pallas_call(
        functools.partial(_attn_ctx_body, lam_init=lam_init),
        grid=(N_CTX_B,),
        in_specs=[pl.BlockSpec((None, 4, QK_DIM), lambda b: (layer, 0, 0)),
                  pl.BlockSpec((None, 1, HEAD_W), lambda b: (layer, 0, 0)),
                  blk(0), blk(1), blk(2)],
        out_specs=pl.BlockSpec((None, T_CTX, ATTN_W), lambda b: (b, 0, 0)),
        out_shape=jax.ShapeDtypeStruct((N_CTX_B, T_CTX, ATTN_W), BF16),
        compiler_params=_cparams(("arbitrary",)),
        name="attn_ctx",
    )(lam_params, g_subln.reshape(DEPTH, 1, HEAD_W), qkv3, qkv3, qkv3)


def _attn_lat_body(lam_ref, gs_ref, q_ref, kc_ref, vc_ref, kl_ref, vl_ref, o_ref, kcat, vcat, *, lam_init):
    @pl.when(pl.program_id(2) == 0)
    def _():
        kcat[0:T_PAST, :] = kc_ref[...]
        kcat[T_PAST:, :] = kl_ref[...]
        vcat[0:T_PAST, :] = vc_ref[...]
        vcat[T_PAST:, :] = vl_ref[...]

    lam = _lam(lam_ref, lam_init)
    for sub in range(Q_SUB):
        rows = slice(sub * TQ, (sub + 1) * TQ)
        qq = _stack_maps(q_ref[rows, :])

        def body(c, carry, qq=qq):
            start = pl.multiple_of(c * TK, TK)
            return _softmax_step(qq, kcat[pl.ds(start, TK), :], vcat[pl.ds(start, TK), :], carry)

        carry = lax.fori_loop(0, (T_PAST + T_LAT) // TK, body, _softmax_init(2 * TQ), unroll=ATTN_UNROLL)
        o_ref[rows, :] = _diff_out(carry, TQ, lam, gs_ref[...], lam_init).astype(o_ref.dtype)


def _attn_lat(qkv, ck, cv, lam_params, g_subln, layer, lam_init):
    qkv3 = qkv.reshape(N_LAT_B, T_LAT, QKV_W)
    full = lambda off: pl.BlockSpec((None, T_LAT, HEAD_W), lambda b, h, i: (b, 0, off + h))
    past = pl.BlockSpec((None, T_PAST, HEAD_W), lambda b, h, i: (b, 0, h))
    return pl.pallas_call(
        functools.partial(_attn_lat_body, lam_init=lam_init),
        grid=(N_LAT_B, N_HEADS, T_LAT // (Q_SUB * TQ)),
        in_specs=[pl.BlockSpec((None, 4, QK_DIM), lambda b, h, i: (layer, 0, 0)),
                  pl.BlockSpec((None, 1, HEAD_W), lambda b, h, i: (layer, 0, 0)),
                  pl.BlockSpec((None, Q_SUB * TQ, HEAD_W), lambda b, h, i: (b, i, h)),
                  past, past, full(N_HEADS), full(2 * N_HEADS)],
        out_specs=pl.BlockSpec((None, Q_SUB * TQ, HEAD_W), lambda b, h, i: (b, i, h)),
        out_shape=jax.ShapeDtypeStruct((N_LAT_B, T_LAT, ATTN_W), BF16),
        scratch_shapes=[pltpu.VMEM((T_PAST + T_LAT, HEAD_W), BF16)] * 2,
        compiler_params=_cparams(("arbitrary", "arbitrary", "arbitrary")),
        name="attn_lat",
    )(lam_params, g_subln.reshape(DEPTH, 1, HEAD_W), qkv3, ck, cv, qkv3, qkv3)


def _fold_groups(ur, ui, wf_ref, scale):
    outs = []
    for g in range(F_GROUPS):
        sl = slice(g * F_CH, (g + 1) * F_CH)
        lhs = jnp.concatenate([ur[:, sl], ui[:, sl]], axis=1).astype(BF16)
        outs.append(jnp.dot(lhs, wf_ref[g], preferred_element_type=F32))
    return jnp.concatenate(outs, axis=1) * scale


def _four_ctx_body(f_ref, dft_ref, wf_ref, o_ref):
    u = jnp.dot(dft_ref[...], f_ref[...].astype(BF16), preferred_element_type=F32)
    o_ref[...] = _fold_groups(u[:T_CTX], u[T_CTX:], wf_ref, 1.0 / math.sqrt(T_CTX * F_CH)).astype(o_ref.dtype)


def _four_ctx(f, wfold_bf, layer):
    ang = 2.0 * np.pi * np.outer(np.arange(T_CTX), np.arange(T_CTX)) / T_CTX
    dft = jnp.asarray(np.concatenate([np.cos(ang), -np.sin(ang)], axis=0), F32).astype(BF16)
    return pl.pallas_call(
        _four_ctx_body,
        grid=(N_CTX_B,),
        in_specs=[pl.BlockSpec((None, T_CTX, FOUR_W), lambda b: (b, 0, 0)),
                  pl.BlockSpec((2 * T_CTX, T_CTX), lambda b: (0, 0)),
                  pl.BlockSpec((None, F_GROUPS, 2 * F_CH, F_CH), lambda b: (layer, 0, 0, 0))],
        out_specs=pl.BlockSpec((None, T_CTX, FOUR_W), lambda b: (b, 0, 0)),
        out_shape=jax.ShapeDtypeStruct((N_CTX_B, T_CTX, FOUR_W), BF16),
        compiler_params=_cparams(("arbitrary",)),
        name="fourier_ctx",
    )(f.reshape(N_CTX_B, T_CTX, FOUR_W), dft, wfold_bf)


def _fft_body(x_ref, ma_ref, twc_ref, tws_ref, mc_ref, ms_ref, wf_ref, o_ref, h_ref):
    nb = FFT_R * FFT_S
    nj = FFT_R // FFT_S
    j = pl.program_id(1)

    @pl.when(j < nj)
    def _():
        x = x_ref[...].reshape(nb, FOUR_W).astype(BF16)
        g = jnp.dot(ma_ref[...], x, preferred_element_type=F32)
        gr, gi = g[:nb], g[nb:]
        twc = jnp.tile(twc_ref[...], (1, FOUR_W // 128))
        tws = jnp.tile(tws_ref[...], (1, FOUR_W // 128))
        lo = pl.ds(pl.multiple_of(j * FFT_S, FFT_S), FFT_S)
        h_ref[lo, :, 0:FOUR_W] = (gr * twc + gi * tws).reshape(FFT_S, FFT_R, FOUR_W)
        h_ref[lo, :, FOUR_W:] = (gi * twc - gr * tws).reshape(FFT_S, FFT_R, FOUR_W)

    @pl.when(j >= nj)
    def _():
        hi = pl.ds(pl.multiple_of((j - nj) * FFT_S, FFT_S), FFT_S)
        h = h_ref[:, hi, :].reshape(nb, 2 * FOUR_W).astype(BF16)
        p = jnp.dot(mc_ref[...], h, preferred_element_type=F32)
        q = jnp.dot(ms_ref[...], h, preferred_element_type=F32)
        ur = p[:, :FOUR_W] + q[:, FOUR_W:]
        ui = p[:, FOUR_W:] - q[:, :FOUR_W]
        out = _fold_groups(ur, ui, wf_ref, 1.0 / math.sqrt(T_LAT * F_CH))
        o_ref[...] = out.reshape(FFT_R, FFT_S, FOUR_W)


def _fft_consts():
    r, s = FFT_R, FFT_S
    nb = r * s
    ang = 2.0 * np.pi * np.outer(np.arange(r), np.arange(r)) / r
    c, sn = np.cos(ang), np.sin(ang)
    eye = np.eye(s)
    ma = np.concatenate([np.einsum('pb,ts->tpbs', c, eye).reshape(nb, nb),
                         np.einsum('pb,ts->tpbs', -sn, eye).reshape(nb, nb)], axis=0)
    mbc = np.einsum('pa,ts->ptas', c, eye).reshape(nb, nb)
    mbs = np.einsum('pa,ts->ptas', sn, eye).reshape(nb, nb)
    tw = 2.0 * np.pi * np.outer(np.arange(r), np.arange(r)).reshape(-1) / (r * r)
    twc = np.broadcast_to(np.cos(tw)[:, None], (r * r, 128))
    tws = np.broadcast_to(np.sin(tw)[:, None], (r * r, 128))
    bf = lambda a: jnp.asarray(a, F32).astype(BF16)
    return bf(ma), bf(mbc), bf(mbs), jnp.asarray(twc, F32), jnp.asarray(tws, F32)


def _four_lat(f, wfold_bf, layer, consts):
    ma, mbc, mbs, twc, tws = consts
    r, s = FFT_R, FFT_S
    nb = r * s
    nj = r // s
    f4 = f.reshape(N_LAT_B, r, r, FOUR_W)
    a_step = lambda j: jnp.minimum(j, nj - 1)
    b_step = lambda j: jnp.maximum(j - nj, 0)
    const = lambda b, j: (0, 0)
    out = pl.pallas_call(
        _fft_body,
        grid=(N_LAT_B, 2 * nj),
        in_specs=[pl.BlockSpec((None, r, s, FOUR_W), lambda b, j: (b, 0, a_step(j), 0)),
                  pl.BlockSpec((2 * nb, nb), const),
                  pl.BlockSpec((nb, 128), lambda b, j: (a_step(j), 0)),
                  pl.BlockSpec((nb, 128), lambda b, j: (a_step(j), 0)),
                  pl.BlockSpec((nb, nb), const),
                  pl.BlockSpec((nb, nb), const),
                  pl.BlockSpec((None, F_GROUPS, 2 * F_CH, F_CH), lambda b, j: (layer, 0, 0, 0))],
        out_specs=pl.BlockSpec((None, r, s, FOUR_W), lambda b, j: (b, 0, b_step(j), 0)),
        out_shape=jax.ShapeDtypeStruct((N_LAT_B, r, r, FOUR_W), F32),
        scratch_shapes=[pltpu.VMEM((r, r, 2 * FOUR_W), F32)],
        compiler_params=_cparams(("arbitrary", "arbitrary")),
        name="fft_lat",
    )(f4, ma, twc, tws, mbc, mbs, wfold_bf)
    return out.reshape(N_LAT_B * T_LAT, FOUR_W)


def _outproj_body(a_ref, f_ref, x_ref, gt_ref, sh_ref, sc_ref, gpost_ref, gpre_ref, wo_ref, *rest,
                  moe, group0, tiles_per_group):
    if moe:
        wr_ref, x1_ref, hn_ref, gates_ref, sel_ref = rest
    else:
        x1_ref, hn_ref = rest
    grp = _group_of(pl.program_id(0), group0, tiles_per_group)
    gt, sh, sc = gt_ref[pl.ds(grp, 1), :], sh_ref[pl.ds(grp, 1), :], sc_ref[pl.ds(grp, 1), :]
    for half in range(TM_OUT // TM):
        rows = slice(half * TM, (half + 1) * TM)
        mix_in = jnp.concatenate([a_ref[rows, :], f_ref[rows, :].astype(BF16)], axis=1)
        mixed = jnp.dot(mix_in, wo_ref[...], preferred_element_type=F32)
        x1 = x_ref[rows, :] + gt * _rms(mixed, gpost_ref[...])
        x1_ref[rows, :] = x1
        hn = _rms(x1, gpre_ref[...]) * (1.0 + sc) + sh
        hn_ref[rows, :] = hn.astype(hn_ref.dtype)
        if moe:
            gates, sel = _route_top2(hn, wr_ref[...])
            gates_ref[rows, :] = gates
            sel_ref[rows, :] = sel.astype(sel_ref.dtype)


def _route_top2(hn, wr):
    wr_hi = wr.astype(BF16)
    wr_lo = (wr - wr_hi.astype(F32)).astype(BF16)
    hn_hi = hn.astype(BF16)
    hn_lo = (hn - hn_hi.astype(F32)).astype(BF16)
    hh = jnp.dot(hn_hi, jnp.concatenate([wr_hi, wr_lo], axis=1), preferred_element_type=F32)
    logits = hh[:, :E_PAD] + hh[:, E_PAD:] + jnp.dot(hn_lo, wr_hi, preferred_element_type=F32)
    lane = lax.broadcasted_iota(jnp.int32, logits.shape, 1).astype(F32)
    neg = jnp.float32(-jnp.inf)
    logits = jnp.where(lane < float(N_EXPERTS), logits, neg)
    m1 = jnp.max(logits, axis=-1, keepdims=True)
    i1 = jnp.min(jnp.where(logits == m1, lane, float(E_PAD)), axis=-1, keepdims=True)
    rest_l = jnp.where(lane == i1, neg, logits)
    m2 = jnp.max(rest_l, axis=-1, keepdims=True)
    i2 = jnp.min(jnp.where(rest_l == m2, lane, float(E_PAD)), axis=-1, keepdims=True)
    e2 = jnp.exp(m2 - m1)
    w1 = 1.0 / (1.0 + e2)
    w2 = e2 / (1.0 + e2)
    gates = jnp.where(lane == i1, w1, 0.0) + jnp.where(lane == i2, w2, 0.0)
    sel = jnp.where((lane == i1) | (lane == i2), 1.0, 0.0)
    return gates, sel


def _outproj(attn_o, four_o, x, mod, layer, g_post, g_ffn_pre, w_out_bf, w_router_pad, group0, tiles_per_group):
    rows = x.shape[0]
    moe = w_router_pad is not None
    row = lambda i: (i, 0)
    const = lambda i: (0, 0)
    modspec = lambda k: pl.BlockSpec((None, N_GROUPS_PAD, D), lambda i: (layer, 0, k))
    blocks_per_group = None if tiles_per_group is None else tiles_per_group * TM // TM_OUT
    in_specs = [pl.BlockSpec((TM_OUT, ATTN_W), row), pl.BlockSpec((TM_OUT, FOUR_W), row),
                pl.BlockSpec((TM_OUT, D), row),
                modspec(2), modspec(3), modspec(4),
                pl.BlockSpec((1, D), const), pl.BlockSpec((1, D), const), pl.BlockSpec((D, D), const)]
    args = [attn_o, four_o, x, mod, mod, mod, g_post.reshape(1, D), g_ffn_pre.reshape(1, D), w_out_bf]
    out_specs = [pl.BlockSpec((TM_OUT, D), row), pl.BlockSpec((TM_OUT, D), row)]
    out_shape = [jax.ShapeDtypeStruct((rows, D), F32), jax.ShapeDtypeStruct((rows, D), BF16)]
    if moe:
        in_specs.append(pl.BlockSpec((D, E_PAD), const))
        args.append(w_router_pad)
        out_specs += [pl.BlockSpec((TM_OUT, E_PAD), row)] * 2
        out_shape += [jax.ShapeDtypeStruct((rows, E_PAD), F32), jax.ShapeDtypeStruct((rows, E_PAD), BF16)]
    return pl.pallas_call(
        functools.partial(_outproj_body, moe=moe, group0=group0, tiles_per_group=blocks_per_group),
        grid=(rows // TM_OUT,),
        in_specs=in_specs, out_specs=out_specs, out_shape=out_shape,
        compiler_params=_cparams(("arbitrary",)),
        name="outproj_moe" if moe else "outproj",
    )(*args)


def _swiglu(hn, wg, wu, wd):
    g = jnp.dot(hn, wg, preferred_element_type=F32)
    u = jnp.dot(hn, wu, preferred_element_type=F32)
    return jnp.dot((_silu(g) * u).astype(BF16), wd, preferred_element_type=F32)


def _ffn_dense_body(hn_ref, x1_ref, gt_ref, gpost_ref, wg_ref, wu_ref, wd_ref, o_ref, *, group0, tiles_per_group):
    grp = _group_of(pl.program_id(0), group0, tiles_per_group)
    ff = _swiglu(hn_ref[...], wg_ref[...], wu_ref[...], wd_ref[...])
    o_ref[...] = x1_ref[...] + gt_ref[pl.ds(grp, 1), :] * _rms(ff, gpost_ref[...])


def _ffn_dense(hn, x1, mod, layer, g_post, wg, wu, wd, group0, tiles_per_group):
    rows = x1.shape[0]
    row = lambda i: (i, 0)
    const = lambda i: (0, 0)
    resident = lambda shape: pl.BlockSpec(shape, const, pipeline_mode=pl.Buffered(1))
    return pl.pallas_call(
        functools.partial(_ffn_dense_body, group0=group0, tiles_per_group=tiles_per_group),
        grid=(rows // TM,),
        in_specs=[pl.BlockSpec((TM, D), row), pl.BlockSpec((TM, D), row),
                  pl.BlockSpec((None, N_GROUPS_PAD, D), lambda i: (layer, 0, 5)),
                  pl.BlockSpec((1, D), const),
                  resident((D, D_FF)), resident((D, D_FF)), resident((D_FF, D))],
        out_specs=pl.BlockSpec((TM, D), row),
        out_shape=jax.ShapeDtypeStruct((rows, D), F32),
        compiler_params=_cparams(("arbitrary",)),
        name="ffn_dense",
    )(hn, x1, mod, g_post.reshape(1, D), wg, wu, wd)


def _experts_body(te_ref, tn_ref, xs_ref, wg_ref, wu_ref, wd_ref, ys_ref, wg_bf, wu_bf, wd_bf):
    i = pl.program_id(0)
    fresh = (i == 0) | (te_ref[i] != te_ref[jnp.maximum(i - 1, 0)])

    @pl.when(fresh)
    def _():
        wg_bf[...] = wg_ref[...].astype(BF16)
        wu_bf[...] = wu_ref[...].astype(BF16)
        wd_bf[...] = wd_ref[...].astype(BF16)

    n_real = tn_ref[i]

    @pl.when(n_real > 0)
    def _():
        ys_ref[...] = _swiglu(xs_ref[...].astype(BF16), wg_bf[...], wu_bf[...], wd_bf[...])

    @pl.when(n_real == 0)
    def _():
        ys_ref[...] = jnp.zeros_like(ys_ref)


def _experts(xs, tile_e, tile_n, eg, eu, ed):
    tile = lambda i, te, tn: (i, 0)
    wspec = lambda shape: pl.BlockSpec((None,) + shape, lambda i, te, tn: (te[i], 0, 0))
    return pl.pallas_call(
        _experts_body,
        grid_spec=pltpu.PrefetchScalarGridSpec(
            num_scalar_prefetch=2,
            grid=(N_TILES,),
            in_specs=[pl.BlockSpec((TMOE, D), tile), wspec((D, D_FF_E)), wspec((D, D_FF_E)), wspec((D_FF_E, D))],
            out_specs=pl.BlockSpec((TMOE, D), tile),
            scratch_shapes=[pltpu.VMEM((D, D_FF_E), BF16), pltpu.VMEM((D, D_FF_E), BF16),
                            pltpu.VMEM((D_FF_E, D), BF16)]),
        out_shape=jax.ShapeDtypeStruct((NP, D), F32),
        compiler_params=_cparams(("arbitrary",)),
        name="moe_experts",
    )(tile_e, tile_n, xs, eg, eu, ed)


def _route_body(*refs, firsts):
    ns = len(firsts)
    gates_refs, sel_refs = refs[:ns], refs[ns:2 * ns]
    ltri_ref, w_ref, rho_ref, tmeta_ref, meta_ref, cnt_ref, off_ref, run_ref = refs[2 * ns:]
    phase, t = pl.program_id(0), pl.program_id(1)

    def this_stream(stream_refs):
        val = stream_refs[0][...]
        for k in range(1, ns):
            val = jnp.where(t >= firsts[k], stream_refs[k][...], val)
        return val

    sel = this_stream(sel_refs)
    group_rows = jnp.ceil(jnp.sum(sel.astype(F32), axis=0, keepdims=True) * (1.0 / GROUP)) * GROUP

    @pl.when((phase == 0) & (t == 0))
    def _():
        cnt_ref[...] = jnp.zeros_like(cnt_ref)

    @pl.when(phase == 0)
    def _():
        cnt_ref[...] += group_rows

    @pl.when((phase == 1) & (t == 0))
    def _():
        off_ref[...] = _segment_meta(cnt_ref[...], meta_ref)
        run_ref[...] = jnp.zeros_like(run_ref)

    @pl.when(phase == 1)
    def _():
        lane1 = lax.broadcasted_iota(jnp.int32, group_rows.shape, 1).astype(F32)
        buf0 = jnp.zeros_like(group_rows)
        running = jnp.zeros((1, 1), F32)
        for e in range(N_EXPERTS):
            pick = lane1 == float(e)
            buf0 = jnp.where(pick, running, buf0)
            running = running + jnp.sum(jnp.where(pick, group_rows, 0.0), axis=-1, keepdims=True)
        slot0 = off_ref[...] + run_ref[...]
        run_ref[...] += group_rows
        row8 = lax.broadcasted_iota(jnp.int32, tmeta_ref.shape, 0)
        tmeta_ref[...] = jnp.where(row8 == 0, slot0, jnp.where(row8 == 1, group_rows,
                                   jnp.where(row8 == 2, buf0, 0.0)))

        earlier = jnp.dot(ltri_ref[...], sel, preferred_element_type=F32)
        rho = earlier + buf0
        chosen = sel > 0
        r_lo = jnp.min(jnp.where(chosen, rho, float(R_TILE)), axis=-1, keepdims=True)
        r_hi = jnp.max(jnp.where(chosen, rho, -1.0), axis=-1, keepdims=True)
        g = this_stream(gates_refs)
        w_lo = jnp.sum(jnp.where(chosen & (rho == r_lo), g, 0.0), axis=-1, keepdims=True)
        w_hi = jnp.sum(jnp.where(chosen & (rho == r_hi), g, 0.0), axis=-1, keepdims=True)
        lane = lax.broadcasted_iota(jnp.int32, rho.shape, 1)
        w_ref[...] = jnp.where(lane == 2, w_lo, jnp.where(lane == 3, w_hi,
                               jnp.where(lane == 4, r_lo, jnp.where(lane == 5, r_hi, 0.0))))
        cols = jnp.where(lane == 0, r_lo, jnp.where(lane == 1, r_hi, 0.0))
        rho_ref[...] = jnp.transpose(cols)[0:N_GROUPS_PAD, :]


def _segment_meta(cnt, meta_ref):
    padded = jnp.ceil(cnt * (1.0 / TMOE)) * TMOE
    lane = lax.broadcasted_iota(jnp.int32, cnt.shape, 1).astype(F32)
    start = lane * TMOE
    off = jnp.zeros_like(cnt)
    tile_e = jnp.zeros_like(cnt)
    tile_n = jnp.zeros_like(cnt)
    running = jnp.zeros((1, 1), F32)
    for e in range(N_EXPERTS):
        pick = lane == float(e)
        cnt_e = jnp.sum(jnp.where(pick, cnt, 0.0), axis=-1, keepdims=True)
        pad_e = jnp.sum(jnp.where(pick, padded, 0.0), axis=-1, keepdims=True)
        off = jnp.where(pick, running, off)
        in_seg = (start >= running) & (start < running + pad_e)
        tile_e = jnp.where(in_seg, float(e), tile_e)
        tile_n = jnp.where(in_seg, jnp.clip(cnt_e - (start - running), 0.0, float(TMOE)), tile_n)
        running = running + pad_e
    tile_e = jnp.where(start >= running, float(N_EXPERTS - 1), tile_e)
    row = lax.broadcasted_iota(jnp.int32, meta_ref.shape, 0)
    last_tile = jnp.where(lane < float(N_EXPERTS), off + padded - TMOE, running)
    last_tile = jnp.where((lane < float(N_EXPERTS)) & (padded == 0.0), -1.0, last_tile)
    meta_ref[...] = jnp.where(row == 0, tile_e, jnp.where(row == 1, tile_n,
                              jnp.where(row == 2, last_tile, 0.0)))
    return off


def _route(gates, sels):
    tiles = [g.shape[0] // TM for g in gates]
    firsts = [int(f) for f in np.cumsum([0] + tiles)]
    n, rows = firsts[-1], firsts[-1] * TM
    ltri = jnp.asarray(np.tril(np.ones((TM, TM)), -1), F32).astype(BF16)
    stream_spec = lambda lo, cnt: pl.BlockSpec((TM, E_PAD), lambda p, t: (jnp.clip(t - lo, 0, cnt - 1), 0))
    stream_specs = [stream_spec(lo, cnt) for lo, cnt in zip(firsts[:-1], tiles)]
    per_tile = lambda p, t: (t * p, 0, 0)
    return pl.pallas_call(
        functools.partial(_route_body, firsts=tuple(firsts[:-1])),
        grid=(2, n),
        in_specs=stream_specs + stream_specs + [pl.BlockSpec((TM, TM), lambda p, t: (0, 0))],
        out_specs=[pl.BlockSpec((TM, E_PAD), lambda p, t: (t * p, 0)),
                   pl.BlockSpec((None, N_GROUPS_PAD, TM), per_tile),
                   pl.BlockSpec((None, N_GROUPS_PAD, E_PAD), per_tile),
                   pl.BlockSpec((N_GROUPS_PAD, E_PAD), lambda p, t: (0, 0))],
        out_shape=[jax.ShapeDtypeStruct((rows, E_PAD), F32),
                   jax.ShapeDtypeStruct((n, N_GROUPS_PAD, TM), F32),
                   jax.ShapeDtypeStruct((n, N_GROUPS_PAD, E_PAD), F32),
                   jax.ShapeDtypeStruct((N_GROUPS_PAD, E_PAD), F32)],
        scratch_shapes=[pltpu.VMEM((1, E_PAD), F32)] * 3,
        compiler_params=_cparams(("arbitrary", "arbitrary")),
        name="moe_route",
    )(*gates, *sels, ltri)


def _group_copies(tmeta_ref, slot_window, buf_window, sem, to_slots, act):
    for e in range(N_EXPERTS):
        slot0, n, buf0 = tmeta_ref[0, e], tmeta_ref[1, e], tmeta_ref[2, e]
        done = jnp.int32(0)
        size = TM
        while size >= GROUP:
            take = (n - done) >= size
            s_at = pl.ds(pl.multiple_of(slot0 + done, GROUP), size)
            b_at = pl.ds(pl.multiple_of(buf0 + done, GROUP), size)
            src, dst = (buf_window(b_at), slot_window(s_at)) if to_slots else (slot_window(s_at), buf_window(b_at))

            @pl.when(take)
            def _(src=src, dst=dst):
                act(pltpu.make_async_copy(src, dst, sem))

            done = done + jnp.where(take, size, 0)
            size //= 2


def _dispatch_body(tmeta_ref, prev_tmeta_ref, pad_ref, rho_ref, *rest, firsts):
    ns = len(firsts)
    hn_refs, (xs_ref, pbuf, zeros, sems) = rest[:ns], rest[ns:]
    i = pl.program_id(0)
    cur = lax.rem(i, 2)
    sem = sems.at[0]

    @pl.when(i == 0)
    def _():
        zeros[...] = jnp.zeros_like(zeros)
        fills = []
        for e in range(N_EXPERTS):
            s = pad_ref[0, e]
            fills.append((s >= 0, pl.multiple_of(jnp.maximum(s, 0), TMOE)))
        for j in range(N_TILES - TOP_K * N_TOKENS // TMOE):
            s = NP - (j + 1) * TMOE
            fills.append((s >= pad_ref[0, N_EXPERTS], s))
        fill = lambda s: pltpu.make_async_copy(zeros, xs_ref.at[pl.ds(s, TMOE), :], sem)
        for live, s in fills:
            @pl.when(live)
            def _(s=s):
                fill(s).start()
        for live, s in fills:
            @pl.when(live)
            def _(s=s):
                fill(s).wait()

    hn = hn_refs[0][...]
    for k in range(1, ns):
        hn = jnp.where(i >= firsts[k], hn_refs[k][...], hn)
    row = lax.broadcasted_iota(jnp.int32, (R_TILE, TM), 0).astype(F32)
    onehot = jnp.where((row == rho_ref[0:1, :]) | (row == rho_ref[1:2, :]), 1.0, 0.0).astype(BF16)
    pbuf[cur] = jnp.dot(onehot, hn, preferred_element_type=F32)

    def store(meta_ref, half, act):
        _group_copies(meta_ref, lambda at: xs_ref.at[at, :], lambda at: pbuf.at[half, at, :], sems.at[half],
                      True, act)

    store(tmeta_ref, cur, lambda c: c.start())

    @pl.when(i > 0)
    def _():
        store(prev_tmeta_ref, 1 - cur, lambda c: c.wait())

    @pl.when(i + 1 == pl.num_programs(0))
    def _():
        store(tmeta_ref, cur, lambda c: c.wait())


def _dispatch(hns, tmeta, rho, pad_starts):
    tiles = [h.shape[0] // TM for h in hns]
    firsts = [int(f) for f in np.cumsum([0] + tiles)]
    n = firsts[-1]
    stream_spec = lambda lo, cnt: pl.BlockSpec((TM, D), lambda i: (jnp.clip(i - lo, 0, cnt - 1), 0))
    return pl.pallas_call(
        functools.partial(_dispatch_body, firsts=tuple(firsts[:-1])),
        grid=(n,),
        in_specs=[pl.BlockSpec((None, N_GROUPS_PAD, E_PAD), lambda i: (i, 0, 0), memory_space=pltpu.SMEM),
                  pl.BlockSpec((None, N_GROUPS_PAD, E_PAD), lambda i: (jnp.maximum(i - 1, 0), 0, 0),
                               memory_space=pltpu.SMEM),
                  pl.BlockSpec(memory_space=pltpu.SMEM),
                  pl.BlockSpec((None, N_GROUPS_PAD, TM), lambda i: (i, 0, 0))]
                 + [stream_spec(lo, cnt) for lo, cnt in zip(firsts[:-1], tiles)],
        out_specs=pl.BlockSpec(memory_space=pl.ANY),
        out_shape=jax.ShapeDtypeStruct((NP, D), F32),
        scratch_shapes=[pltpu.VMEM((2, R_TILE, D), F32), pltpu.VMEM((TMOE, D), F32),
                        pltpu.SemaphoreType.DMA((2,))],
        compiler_params=_cparams(("arbitrary",)),
        name="moe_dispatch",
    )(tmeta, tmeta, pad_starts, rho, *hns)


def _combine_body(tmeta_ref, next_tmeta_ref, w_ref, x1_ref, gt_ref, gpost_ref, ys_ref, o_ref, ybuf, sems,
                   *, group0, tiles_per_group):
    i = pl.program_id(0)
    grp = _group_of(i, group0, tiles_per_group)
    cur = lax.rem(i, 2)

    def fetch(meta_ref, half, act):
        _group_copies(meta_ref, lambda at: ys_ref.at[at, :], lambda at: ybuf.at[half, at, :], sems.at[half],
                      False, act)

    @pl.when(i == 0)
    def _():
        ybuf[...] = jnp.zeros_like(ybuf)
        fetch(tmeta_ref, 0, lambda c: c.start())

    @pl.when(i + 1 < pl.num_programs(0))
    def _():
        fetch(next_tmeta_ref, 1 - cur, lambda c: c.start())

    fetch(tmeta_ref, cur, lambda c: c.wait())
    w = w_ref[...]
    lane = lax.broadcasted_iota(jnp.int32, w.shape, 1)
    col = lambda j: jnp.sum(jnp.where(lane == j, w, 0.0), axis=-1, keepdims=True)
    w_lo, w_hi, r_lo, r_hi = col(2), col(3), col(4), col(5)
    y = ybuf[cur].astype(BF16)
    pos = lax.broadcasted_iota(jnp.int32, (TM, R_TILE), 1).astype(F32)
    pick = lambda r: jnp.dot(jnp.where(pos == r, 1.0, 0.0).astype(BF16), y, preferred_element_type=F32)
    ff = w_lo * pick(r_lo) + w_hi * pick(r_hi)
    o_ref[...] = x1_ref[...] + gt_ref[pl.ds(grp, 1), :] * _rms(ff, gpost_ref[...])


def _combine(ys, tmeta, slot_w, tile0, x1, mod, layer, g_post, group0, tiles_per_group):
    rows = x1.shape[0]
    n = rows // TM
    row = lambda i: (i, 0)
    return pl.pallas_call(
        functools.partial(_combine_body, group0=group0, tiles_per_group=tiles_per_group),
        grid=(n,),
        in_specs=[pl.BlockSpec((None, N_GROUPS_PAD, E_PAD), lambda i: (tile0 + i, 0, 0), memory_space=pltpu.SMEM),
                  pl.BlockSpec((None, N_GROUPS_PAD, E_PAD), lambda i: (tile0 + jnp.minimum(i + 1, n - 1), 0, 0),
                               memory_space=pltpu.SMEM),
                  pl.BlockSpec((TM, E_PAD), lambda i: (tile0 + i, 0)), pl.BlockSpec((TM, D), row),
                  pl.BlockSpec((None, N_GROUPS_PAD, D), lambda i: (layer, 0, 5)),
                  pl.BlockSpec((1, D), lambda i: (0, 0)),
                  pl.BlockSpec(memory_space=pl.ANY)],
        out_specs=pl.BlockSpec((TM, D), row),
        out_shape=jax.ShapeDtypeStruct((rows, D), F32),
        scratch_shapes=[pltpu.VMEM((2, R_TILE, D), F32), pltpu.SemaphoreType.DMA((2,))],
        compiler_params=_cparams(("arbitrary",)),
        name="moe_combine",
    )(tmeta, tmeta, slot_w, x1, mod, g_post.reshape(1, D), ys)


def kernel(x_prompt, x_sample, cache_k, cache_v, c, c_ctx, g_attn_pre, g_attn_post, g_ffn_pre, g_ffn_post,
           w_ada, b_ada, w_in, lam_params, g_subln, w_fnet, w_out, w_gate, w_up, w_down, w_router,
           e_gate, e_up, e_down):
    assert x_prompt.shape == (N_CTX_B, T_CTX, D) and x_sample.shape == (N_LAT_B, T_LAT, D)
    assert cache_k.shape == (N_LAT_B, DEPTH, T_PAST, N_HEADS, 2, QK_DIM)

    cond = jnp.zeros((N_GROUPS_PAD, D), F32).at[0].set(c_ctx).at[1:1 + N_LAT_B].set(c)
    mod = _ada(cond, w_ada, b_ada)
    wfold_bf = _wfold(w_fnet).astype(BF16)
    rope_tabs = _rope_tables()
    fft_consts = _fft_consts()

    w_in_bf = w_in.astype(BF16)
    w_out_bf = w_out.astype(BF16)
    w_router_pad = jnp.zeros((DEPTH // 2, D, E_PAD), F32).at[:, :, :N_EXPERTS].set(w_router)
    ck_bf = cache_k.reshape(N_LAT_B, DEPTH, T_PAST, ATTN_W).astype(BF16)
    cv_bf = cache_v.reshape(N_LAT_B, DEPTH, T_PAST, ATTN_W).astype(BF16)

    lat_tiles = T_LAT // TM
    streams = [dict(x=x_prompt.reshape(N_CTX_B * T_CTX, D), group0=0, tiles=None, lat=False),
               dict(x=x_sample.reshape(N_LAT_B * T_LAT, D), group0=1, tiles=lat_tiles, lat=True)]
    new_k = new_v = None
    for l in range(DEPTH):
        lam_init = 0.8 - 0.6 * math.exp(-0.3 * l)
        i = l // 2
        routed = []
        for s in streams:
            g0, tiles = s["group0"], s["tiles"]
            if s["lat"]:
                qkv, f = _inproj(s["x"], mod, l, g_attn_pre[l], w_in_bf[l], rope_tabs, g0, tiles)
                attn_o = _attn_lat(qkv, ck_bf[:, l], cv_bf[:, l], lam_params, g_subln, l, lam_init)
                attn_o = attn_o.reshape(N_LAT_B * T_LAT, ATTN_W)
                four_o = _four_lat(f, wfold_bf, l, fft_consts)
            else:
                qkv, f, new_k, new_v = _inproj(s["x"], mod, l, g_attn_pre[l], w_in_bf[l], None, g0, tiles,
                                               cache=None if l == 0 else (new_k, new_v))
                attn_o = _attn_ctx(qkv, lam_params, g_subln, l, lam_init).reshape(N_CTX_B * T_CTX, ATTN_W)
                four_o = _four_ctx(f, wfold_bf, l).reshape(N_CTX_B * T_CTX, FOUR_W)
            if l % 2 == 0:
                x1, hn = _outproj(attn_o, four_o, s["x"], mod, l, g_attn_post[l], g_ffn_pre[l], w_out_bf[l],
                                  None, g0, tiles)
                s["x"] = _ffn_dense(hn, x1, mod, l, g_ffn_post[l], w_gate[i].astype(BF16), w_up[i].astype(BF16),
                                    w_down[i].astype(BF16), g0, tiles)
            else:
                routed.append(_outproj(attn_o, four_o, s["x"], mod, l, g_attn_post[l], g_ffn_pre[l],
                                       w_out_bf[l], w_router_pad[i], g0, tiles))
        if l % 2 == 1:
            slot_w, rho, tmeta, meta = _route([r[2] for r in routed], [r[3] for r in routed])
            tmeta = tmeta.astype(jnp.int32)
            tile_e = meta[0, :N_TILES].astype(jnp.int32)
            tile_n = meta[1, :N_TILES].astype(jnp.int32)
            pad_starts = meta[2:3, :N_EXPERTS + 1].astype(jnp.int32)
            bounds = np.cumsum([0] + [r[0].shape[0] for r in routed])
            xs = _dispatch([r[1] for r in routed], tmeta, rho, pad_starts)
            ys = _experts(xs, tile_e, tile_n, e_gate[i], e_up[i], e_down[i])
            for s, r, lo in zip(streams, routed, bounds[:-1]):
                s["x"] = _combine(ys, tmeta, slot_w, int(lo) // TM, r[0], mod, l, g_ffn_post[l],
                                   s["group0"], s["tiles"])

    y_prompt = streams[0]["x"].reshape(N_CTX_B, T_CTX, D)
    y_sample = streams[1]["x"].reshape(N_LAT_B, T_LAT, D)
    new_cache_k = new_k.reshape(N_CTX_B, DEPTH, T_CTX, N_HEADS, 2, QK_DIM)
    new_cache_v = new_v.reshape(N_CTX_B, DEPTH, T_CTX, N_HEADS, HEAD_W)
    return (y_prompt, y_sample, new_cache_k, new_cache_v)
```

```python
import functools
import math

import numpy as np
import jax
import jax.numpy as jnp
from jax import lax
from jax.experimental import pallas as pl
from jax.experimental.pallas import tpu as pltpu

F32 = jnp.float32
BF16 = jnp.bfloat16

D = 1024
N_CTX_B = 16
T_CTX = 256
N_LAT_B = 2
T_LAT = 4096
T_PAST = 512
DEPTH = 2
GRID_W = 64
N_HEADS = 4
QK_DIM = 64
HEAD_W = 2 * QK_DIM
ATTN_W = N_HEADS * HEAD_W
F_GROUPS = 4
F_CH = 128
FOUR_W = F_GROUPS * F_CH
IN_W = 3 * ATTN_W + FOUR_W
QKV_W = 3 * ATTN_W
D_FF = 2816
N_EXPERTS = 8
D_FF_E = 1024
N_MOD = 6
EPS = 1e-6
ROPE_THETA = 10000.0
N_GROUPS_PAD = 8
E_PAD = 128

TM = 512
TM_OUT = 1024
TOP_K = 2
TMOE = 512
N_TOKENS = N_CTX_B * T_CTX + N_LAT_B * T_LAT
GROUP = 8
R_TILE = TOP_K * TM + N_EXPERTS * GROUP
N_TILES = -(-(TOP_K * N_TOKENS + (N_TOKENS // TM) * N_EXPERTS * (GROUP - 1)) // TMOE) + N_EXPERTS
NP = N_TILES * TMOE
TQ = 512
Q_SUB = 2
TK = 1536
ATTN_UNROLL = 3
FFT_R = 64
FFT_S = 8
VMEM_LIMIT = 56 * 1024 * 1024


def _cparams(sem):
    return pltpu.CompilerParams(dimension_semantics=sem, vmem_limit_bytes=VMEM_LIMIT)


def _rms(x, g):
    ms = jnp.mean(x * x, axis=-1, keepdims=True)
    return x * lax.rsqrt(ms + EPS) * g


def _silu(x):
    return x * (1.0 / (1.0 + jnp.exp(-x)))


def _ada_body(c_ref, w_ref, b_ref, o_ref):
    s = _silu(c_ref[...])
    o_ref[...] = jnp.dot(s, w_ref[...], preferred_element_type=F32,
                         precision=lax.Precision.HIGHEST) + b_ref[...]


def _ada(cond, w_ada, b_ada):
    tn = 1536
    return pl.pallas_call(
        _ada_body,
        grid=(DEPTH, N_MOD * D // tn),
        in_specs=[pl.BlockSpec((N_GROUPS_PAD, D), lambda l, j: (0, 0)),
                  pl.BlockSpec((None, D, tn), lambda l, j: (l, 0, j)),
                  pl.BlockSpec((None, 1, tn), lambda l, j: (l, 0, j))],
        out_specs=pl.BlockSpec((None, N_GROUPS_PAD, tn), lambda l, j: (l, 0, j)),
        out_shape=jax.ShapeDtypeStruct((DEPTH, N_GROUPS_PAD, N_MOD * D), F32),
        compiler_params=_cparams(("arbitrary", "arbitrary")),
        name="ada_mod",
    )(cond, w_ada, b_ada.reshape(DEPTH, 1, N_MOD * D))


def _wfold_body(cc_ref, sc_ref, w_ref, o_ref):
    w = w_ref[...]
    o_ref[0:F_CH, :] = jnp.dot(cc_ref[...], w, preferred_element_type=F32,
                               precision=lax.Precision.HIGHEST)
    o_ref[F_CH:2 * F_CH, :] = jnp.dot(sc_ref[...], w, preferred_element_type=F32,
                                      precision=lax.Precision.HIGHEST)


def _wfold(w_fnet):
    ang = 2.0 * np.pi * np.outer(np.arange(F_CH), np.arange(F_CH)) / F_CH
    cc = jnp.asarray(np.cos(ang), F32)
    sc = jnp.asarray(np.sin(ang), F32)
    cspec = pl.BlockSpec((F_CH, F_CH), lambda l, g: (0, 0))
    return pl.pallas_call(
        _wfold_body,
        grid=(DEPTH, F_GROUPS),
        in_specs=[cspec, cspec, pl.BlockSpec((None, None, F_CH, F_CH), lambda l, g: (l, g, 0, 0))],
        out_specs=pl.BlockSpec((None, None, 2 * F_CH, F_CH), lambda l, g: (l, g, 0, 0)),
        out_shape=jax.ShapeDtypeStruct((DEPTH, F_GROUPS, 2 * F_CH, F_CH), F32),
        compiler_params=_cparams(("arbitrary", "arbitrary")),
        name="fnet_fold",
    )(cc, sc, w_fnet)


def _group_of(i, group0, tiles_per_group):
    if tiles_per_group is None:
        return group0
    return group0 + lax.div(i, jnp.int32(tiles_per_group))


def _inproj_body(x_ref, sh_ref, sc_ref, g_ref, w_ref, *rest, rope, cache_layer, group0, tiles_per_group):
    if rope:
        cos_ref, sin_ref, qkv_ref, f_ref = rest
    else:
        qkv_ref, f_ref, k32_ref, v32_ref = rest[-4:]
    grp = _group_of(pl.program_id(0), group0, tiles_per_group)
    sh = sh_ref[pl.ds(grp, 1), :]
    sc = sc_ref[pl.ds(grp, 1), :]
    hn = _rms(x_ref[...], g_ref[...]) * (1.0 + sc) + sh
    proj = jnp.dot(hn.astype(BF16), w_ref[...], preferred_element_type=F32)
    qk = proj[:, :2 * ATTN_W]
    v = proj[:, 2 * ATTN_W:QKV_W]
    if rope:
        cos = jnp.tile(cos_ref[...], (1, 2 * ATTN_W // HEAD_W))
        sin = jnp.tile(sin_ref[...], (1, 2 * ATTN_W // HEAD_W))
        lane = lax.broadcasted_iota(jnp.int32, qk.shape, 1)
        low = (lane % 32) < 16
        rot = jnp.where(low, pltpu.roll(qk, 2 * ATTN_W - 16, 1), pltpu.roll(qk, 16, 1))
        qk = qk * cos + rot * sin
    else:
        for ref, val in ((k32_ref, qk[:, ATTN_W:]), (v32_ref, v)):
            val = val.reshape(TM // T_CTX, T_CTX, ATTN_W)
            if cache_layer == 0:
                ref[:, 0] = val
                ref[:, 1:] = jnp.zeros((TM // T_CTX, DEPTH - 1, T_CTX, ATTN_W), F32)
            else:
                ref[...] = val
    qkv_ref[:, 0:ATTN_W] = (qk[:, :ATTN_W] * (QK_DIM ** -0.5 * math.log2(math.e))).astype(BF16)
    qkv_ref[:, ATTN_W:2 * ATTN_W] = qk[:, ATTN_W:].astype(BF16)
    qkv_ref[:, 2 * ATTN_W:] = v.astype(BF16)
    f_ref[...] = proj[:, QKV_W:]


def _inproj(x, mod, layer, g_pre, w_in_bf, rope_tabs, group0, tiles_per_group, cache=None):
    rows = x.shape[0]
    rope = rope_tabs is not None
    aliases = {}
    row = lambda i: (i, 0)
    const = lambda i: (0, 0)
    in_specs = [pl.BlockSpec((TM, D), row),
                pl.BlockSpec((None, N_GROUPS_PAD, D), lambda i: (layer, 0, 0)),
                pl.BlockSpec((None, N_GROUPS_PAD, D), lambda i: (layer, 0, 1)),
                pl.BlockSpec((1, D), const),
                pl.BlockSpec((D, IN_W), const)]
    args = [x, mod, mod, g_pre.reshape(1, D), w_in_bf]
    out_specs = [pl.BlockSpec((TM, QKV_W), row), pl.BlockSpec((TM, FOUR_W), row)]
    out_shape = [jax.ShapeDtypeStruct((rows, QKV_W), BF16), jax.ShapeDtypeStruct((rows, FOUR_W), F32)]
    if rope:
        tiles_per_seq = T_LAT // TM
        tab = pl.BlockSpec((TM, HEAD_W), lambda i: (i % tiles_per_seq, 0))
        in_specs += [tab, tab]
        args += list(rope_tabs)
    else:
        nb = TM // T_CTX
        if cache is None:
            cache_spec = pl.BlockSpec((nb, DEPTH, T_CTX, ATTN_W), lambda i: (i, 0, 0, 0))
        else:
            cache_spec = pl.BlockSpec((nb, None, T_CTX, ATTN_W), lambda i: (i, layer, 0, 0))
            aliases = {len(args): 2, len(args) + 1: 3}
            in_specs += [pl.BlockSpec(memory_space=pl.ANY)] * 2
            args += list(cache)
        out_specs += [cache_spec] * 2
        out_shape += [jax.ShapeDtypeStruct((N_CTX_B, DEPTH, T_CTX, ATTN_W), F32)] * 2
    return pl.pallas_call(
        functools.partial(_inproj_body, rope=rope, cache_layer=None if rope else (0 if cache is None else layer),
                          group0=group0, tiles_per_group=tiles_per_group),
        grid=(rows // TM,),
        in_specs=in_specs, out_specs=out_specs, out_shape=out_shape,
        input_output_aliases=aliases,
        compiler_params=_cparams(("arbitrary",)),
        name="inproj_lat" if rope else "inproj_ctx",
    )(*args)


def _rope_tables():
    half = QK_DIM // 2
    inv = 1.0 / (ROPE_THETA ** (np.arange(0, half, 2, dtype=np.float64) / half))
    pos = np.arange(T_LAT)
    def tab(p):
        ang = p[:, None].astype(np.float64) * inv[None, :]
        return np.concatenate([ang, ang], axis=-1)
    ang = np.concatenate([tab(pos // GRID_W), tab(pos % GRID_W)], axis=-1)
    sign = np.where((np.arange(QK_DIM) % 32) < 16, -1.0, 1.0)
    cos = np.tile(np.cos(ang), (1, 2))
    sin = np.tile(np.sin(ang) * sign[None, :], (1, 2))
    return jnp.asarray(cos, F32), jnp.asarray(sin, F32)


def _lam(lam_ref, lam_init):
    lp = lam_ref[...]
    return (jnp.exp(jnp.sum(lp[0:1] * lp[1:2], keepdims=True))
            - jnp.exp(jnp.sum(lp[2:3] * lp[3:4], keepdims=True)) + lam_init)


def _stack_maps(q):
    lane = lax.broadcasted_iota(jnp.int32, q.shape, 1)
    zero = jnp.zeros_like(q)
    return jnp.concatenate([jnp.where(lane < QK_DIM, q, zero), jnp.where(lane >= QK_DIM, q, zero)], axis=0)


def _softmax_step(qq, kb, vb, carry):
    m, l, acc = carry
    s = lax.dot_general(qq, kb, (((1,), (1,)), ((), ())), preferred_element_type=F32)
    m_new = jnp.maximum(m, jnp.max(s, axis=-1, keepdims=True))
    alpha = jnp.exp2(m - m_new)
    p = jnp.exp2(s - m_new)
    l = alpha * l + jnp.sum(p, axis=-1, keepdims=True)
    acc = alpha * acc + jnp.dot(p.astype(BF16), vb, preferred_element_type=F32)
    return m_new, l, acc


def _softmax_init(rows):
    return (jnp.full((rows, 1), -jnp.inf, F32), jnp.zeros((rows, 1), F32), jnp.zeros((rows, HEAD_W), F32))


def _diff_out(carry, tq, lam, gs, lam_init):
    _, l, acc = carry
    o = acc / l
    a = o[:tq] - lam * o[tq:]
    return _rms(a, gs) * (1.0 - lam_init)


def _attn_ctx_body(lam_ref, gs_ref, q_ref, k_ref, v_ref, o_ref, *, lam_init):
    lam = _lam(lam_ref, lam_init)
    for h in range(N_HEADS):
        sl = slice(h * HEAD_W, (h + 1) * HEAD_W)
        carry = _softmax_step(_stack_maps(q_ref[:, sl]), k_ref[:, sl], v_ref[:, sl], _softmax_init(2 * T_CTX))
        o_ref[:, sl] = _diff_out(carry, T_CTX, lam, gs_ref[...], lam_init).astype(o_ref.dtype)


def _attn_ctx(qkv, lam_params, g_subln, layer, lam_init):
    qkv3 = qkv.reshape(N_CTX_B, T_CTX, QKV_W)
    blk = lambda part: pl.BlockSpec((None, T_CTX, ATTN_W), lambda b: (b, 0, part))
    return pl.pallas_call(
        functools.partial(_attn_ctx_body, lam_init=lam_init),
        grid=(N_CTX_B,),
        in_specs=[pl.BlockSpec((None, 4, QK_DIM), lambda b: (layer, 0, 0)),
                  pl.BlockSpec((None, 1, HEAD_W), lambda b: (layer, 0, 0)),
                  blk(0), blk(1), blk(2)],
        out_specs=pl.BlockSpec((None, T_CTX, ATTN_W), lambda b: (b, 0, 0)),
        out_shape=jax.ShapeDtypeStruct((N_CTX_B, T_CTX, ATTN_W), BF16),
        compiler_params=_cparams(("arbitrary",)),
        name="attn_ctx",
    )(lam_params, g_subln.reshape(DEPTH, 1, HEAD_W), qkv3, qkv3, qkv3)


def _attn_lat_body(lam_ref, gs_ref, q_ref, kc_ref, vc_ref, kl_ref, vl_ref, o_ref, kcat, vcat, *, lam_init):
    @pl.when(pl.program_id(2) == 0)
    def _():
        kcat[0:T_PAST, :] = kc_ref[...]
        kcat[T_PAST:, :] = kl_ref[...]
        vcat[0:T_PAST, :] = vc_ref[...]
        vcat[T_PAST:, :] = vl_ref[...]

    lam = _lam(lam_ref, lam_init)
    for sub in range(Q_SUB):
        rows = slice(sub * TQ, (sub + 1) * TQ)
        qq = _stack_maps(q_ref[rows, :])

        def body(c, carry, qq=qq):
            start = pl.multiple_of(c * TK, TK)
            return _softmax_step(qq, kcat[pl.ds(start, TK), :], vcat[pl.ds(start, TK), :], carry)

        carry = lax.fori_loop(0, (T_PAST + T_LAT) // TK, body, _softmax_init(2 * TQ), unroll=ATTN_UNROLL)
        o_ref[rows, :] = _diff_out(carry, TQ, lam, gs_ref[...], lam_init).astype(o_ref.dtype)


def _attn_lat(qkv, ck, cv, lam_params, g_subln, layer, lam_init):
    qkv3 = qkv.reshape(N_LAT_B, T_LAT, QKV_W)
    full = lambda off: pl.BlockSpec((None, T_LAT, HEAD_W), lambda b, h, i: (b, 0, off + h))
    past = pl.BlockSpec((None, T_PAST, HEAD_W), lambda b, h, i: (b, 0, h))
    return pl.pallas_call(
        functools.partial(_attn_lat_body, lam_init=lam_init),
        grid=(N_LAT_B, N_HEADS, T_LAT // (Q_SUB * TQ)),
        in_specs=[pl.BlockSpec((None, 4, QK_DIM), lambda b, h, i: (layer, 0, 0)),
                  pl.BlockSpec((None, 1, HEAD_W), lambda b, h, i: (layer, 0, 0)),
                  pl.BlockSpec((None, Q_SUB * TQ, HEAD_W), lambda b, h, i: (b, i, h)),
                  past, past, full(N_HEADS), full(2 * N_HEADS)],
        out_specs=pl.BlockSpec((None, Q_SUB * TQ, HEAD_W), lambda b, h, i: (b, i, h)),
        out_shape=jax.ShapeDtypeStruct((N_LAT_B, T_LAT, ATTN_W), BF16),
        scratch_shapes=[pltpu.VMEM((T_PAST + T_LAT, HEAD_W), BF16)] * 2,
        compiler_params=_cparams(("arbitrary", "arbitrary", "arbitrary")),
        name="attn_lat",
    )(lam_params, g_subln.reshape(DEPTH, 1, HEAD_W), qkv3, ck, cv, qkv3, qkv3)


def _fold_groups(ur, ui, wf_ref, scale):
    outs = []
    for g in range(F_GROUPS):
        sl = slice(g * F_CH, (g + 1) * F_CH)
        lhs = jnp.concatenate([ur[:, sl], ui[:, sl]], axis=1).astype(BF16)
        outs.append(jnp.dot(lhs, wf_ref[g], preferred_element_type=F32))
    return jnp.concatenate(outs, axis=1) * scale


def _four_ctx_body(f_ref, dft_ref, wf_ref, o_ref):
    u = jnp.dot(dft_ref[...], f_ref[...].astype(BF16), preferred_element_type=F32)
    o_ref[...] = _fold_groups(u[:T_CTX], u[T_CTX:], wf_ref, 1.0 / math.sqrt(T_CTX * F_CH)).astype(o_ref.dtype)


def _four_ctx(f, wfold_bf, layer):
    ang = 2.0 * np.pi * np.outer(np.arange(T_CTX), np.arange(T_CTX)) / T_CTX
    dft = jnp.asarray(np.concatenate([np.cos(ang), -np.sin(ang)], axis=0), F32).astype(BF16)
    return pl.pallas_call(
        _four_ctx_body,
        grid=(N_CTX_B,),
        in_specs=[pl.BlockSpec((None, T_CTX, FOUR_W), lambda b: (b, 0, 0)),
                  pl.BlockSpec((2 * T_CTX, T_CTX), lambda b: (0, 0)),
                  pl.BlockSpec((None, F_GROUPS, 2 * F_CH, F_CH), lambda b: (layer, 0, 0, 0))],
        out_specs=pl.BlockSpec((None, T_CTX, FOUR_W), lambda b: (b, 0, 0)),
        out_shape=jax.ShapeDtypeStruct((N_CTX_B, T_CTX, FOUR_W), BF16),
        compiler_params=_cparams(("arbitrary",)),
        name="fourier_ctx",
    )(f.reshape(N_CTX_B, T_CTX, FOUR_W), dft, wfold_bf)


def _fft_body(x_ref, ma_ref, twc_ref, tws_ref, mc_ref, ms_ref, wf_ref, o_ref, h_ref):
    nb = FFT_R * FFT_S
    nj = FFT_R // FFT_S
    j = pl.program_id(1)

    @pl.when(j < nj)
    def _():
        x = x_ref[...].reshape(nb, FOUR_W).astype(BF16)
        g = jnp.dot(ma_ref[...], x, preferred_element_type=F32)
        gr, gi = g[:nb], g[nb:]
        twc = jnp.tile(twc_ref[...], (1, FOUR_W // 128))
        tws = jnp.tile(tws_ref[...], (1, FOUR_W // 128))
        lo = pl.ds(pl.multiple_of(j * FFT_S, FFT_S), FFT_S)
        h_ref[lo, :, 0:FOUR_W] = (gr * twc + gi * tws).reshape(FFT_S, FFT_R, FOUR_W)
        h_ref[lo, :, FOUR_W:] = (gi * twc - gr * tws).reshape(FFT_S, FFT_R, FOUR_W)

    @pl.when(j >= nj)
    def _():
        hi = pl.ds(pl.multiple_of((j - nj) * FFT_S, FFT_S), FFT_S)
        h = h_ref[:, hi, :].reshape(nb, 2 * FOUR_W).astype(BF16)
        p = jnp.dot(mc_ref[...], h, preferred_element_type=F32)
        q = jnp.dot(ms_ref[...], h, preferred_element_type=F32)
        ur = p[:, :FOUR_W] + q[:, FOUR_W:]
        ui = p[:, FOUR_W:] - q[:, :FOUR_W]
        out = _fold_groups(ur, ui, wf_ref, 1.0 / math.sqrt(T_LAT * F_CH))
        o_ref[...] = out.reshape(FFT_R, FFT_S, FOUR_W)


def _fft_consts():
    r, s = FFT_R, FFT_S
    nb = r * s
    ang = 2.0 * np.pi * np.outer(np.arange(r), np.arange(r)) / r
    c, sn = np.cos(ang), np.sin(ang)
    eye = np.eye(s)
    ma = np.concatenate([np.einsum('pb,ts->tpbs', c, eye).reshape(nb, nb),
                         np.einsum('pb,ts->tpbs', -sn, eye).reshape(nb, nb)], axis=0)
    mbc = np.einsum('pa,ts->ptas', c, eye).reshape(nb, nb)
    mbs = np.einsum('pa,ts->ptas', sn, eye).reshape(nb, nb)
    tw = 2.0 * np.pi * np.outer(np.arange(r), np.arange(r)).reshape(-1) / (r * r)
    twc = np.broadcast_to(np.cos(tw)[:, None], (r * r, 128))
    tws = np.broadcast_to(np.sin(tw)[:, None], (r * r, 128))
    bf = lambda a: jnp.asarray(a, F32).astype(BF16)
    return bf(ma), bf(mbc), bf(mbs), jnp.asarray(twc, F32), jnp.asarray(tws, F32)


def _four_lat(f, wfold_bf, layer, consts):
    ma, mbc, mbs, twc, tws = consts
    r, s = FFT_R, FFT_S
    nb = r * s
    nj = r // s
    f4 = f.reshape(N_LAT_B, r, r, FOUR_W)
    a_step = lambda j: jnp.minimum(j, nj - 1)
    b_step = lambda j: jnp.maximum(j - nj, 0)
    const = lambda b, j: (0, 0)
    out = pl.pallas_call(
        _fft_body,
        grid=(N_LAT_B, 2 * nj),
        in_specs=[pl.BlockSpec((None, r, s, FOUR_W), lambda b, j: (b, 0, a_step(j), 0)),
                  pl.BlockSpec((2 * nb, nb), const),
                  pl.BlockSpec((nb, 128), lambda b, j: (a_step(j), 0)),
                  pl.BlockSpec((nb, 128), lambda b, j: (a_step(j), 0)),
                  pl.BlockSpec((nb, nb), const),
                  pl.BlockSpec((nb, nb), const),
                  pl.BlockSpec((None, F_GROUPS, 2 * F_CH, F_CH), lambda b, j: (layer, 0, 0, 0))],
        out_specs=pl.BlockSpec((None, r, s, FOUR_W), lambda b, j: (b, 0, b_step(j), 0)),
        out_shape=jax.ShapeDtypeStruct((N_LAT_B, r, r, FOUR_W), F32),
        scratch_shapes=[pltpu.VMEM((r, r, 2 * FOUR_W), F32)],
        compiler_params=_cparams(("arbitrary", "arbitrary")),
        name="fft_lat",
    )(f4, ma, twc, tws, mbc, mbs, wfold_bf)
    return out.reshape(N_LAT_B * T_LAT, FOUR_W)


def _outproj_body(a_ref, f_ref, x_ref, gt_ref, sh_ref, sc_ref, gpost_ref, gpre_ref, wo_ref, *rest,
                  moe, group0, tiles_per_group):
    if moe:
        wr_ref, x1_ref, hn_ref, gates_ref, sel_ref = rest
    else:
        x1_ref, hn_ref = rest
    grp = _group_of(pl.program_id(0), group0, tiles_per_group)
    gt, sh, sc = gt_ref[pl.ds(grp, 1), :], sh_ref[pl.ds(grp, 1), :], sc_ref[pl.ds(grp, 1), :]
    for half in range(TM_OUT // TM):
        rows = slice(half * TM, (half + 1) * TM)
        mix_in = jnp.concatenate([a_ref[rows, :], f_ref[rows, :].astype(BF16)], axis=1)
        mixed = jnp.dot(mix_in, wo_ref[...], preferred_element_type=F32)
        x1 = x_ref[rows, :] + gt * _rms(mixed, gpost_ref[...])
        x1_ref[rows, :] = x1
        hn = _rms(x1, gpre_ref[...]) * (1.0 + sc) + sh
        hn_ref[rows, :] = hn.astype(hn_ref.dtype)
        if moe:
            gates, sel = _route_top2(hn, wr_ref[...])
            gates_ref[rows, :] = gates
            sel_ref[rows, :] = sel.astype(sel_ref.dtype)


def _route_top2(hn, wr):
    wr_hi = wr.astype(BF16)
    wr_lo = (wr - wr_hi.astype(F32)).astype(BF16)
    hn_hi = hn.astype(BF16)
    hn_lo = (hn - hn_hi.astype(F32)).astype(BF16)
    hh = jnp.dot(hn_hi, jnp.concatenate([wr_hi, wr_lo], axis=1), preferred_element_type=F32)
    logits = hh[:, :E_PAD] + hh[:, E_PAD:] + jnp.dot(hn_lo, wr_hi, preferred_element_type=F32)
    lane = lax.broadcasted_iota(jnp.int32, logits.shape, 1).astype(F32)
    neg = jnp.float32(-jnp.inf)
    logits = jnp.where(lane < float(N_EXPERTS), logits, neg)
    m1 = jnp.max(logits, axis=-1, keepdims=True)
    i1 = jnp.min(jnp.where(logits == m1, lane, float(E_PAD)), axis=-1, keepdims=True)
    rest_l = jnp.where(lane == i1, neg, logits)
    m2 = jnp.max(rest_l, axis=-1, keepdims=True)
    i2 = jnp.min(jnp.where(rest_l == m2, lane, float(E_PAD)), axis=-1, keepdims=True)
    e2 = jnp.exp(m2 - m1)
    w1 = 1.0 / (1.0 + e2)
    w2 = e2 / (1.0 + e2)
    gates = jnp.where(lane == i1, w1, 0.0) + jnp.where(lane == i2, w2, 0.0)
    sel = jnp.where((lane == i1) | (lane == i2), 1.0, 0.0)
    return gates, sel


def _outproj(attn_o, four_o, x, mod, layer, g_post, g_ffn_pre, w_out_bf, w_router_pad, group0, tiles_per_group):
    rows = x.shape[0]
    moe = w_router_pad is not None
    row = lambda i: (i, 0)
    const = lambda i: (0, 0)
    modspec = lambda k: pl.BlockSpec((None, N_GROUPS_PAD, D), lambda i: (layer, 0, k))
    blocks_per_group = None if tiles_per_group is None else tiles_per_group * TM // TM_OUT
    in_specs = [pl.BlockSpec((TM_OUT, ATTN_W), row), pl.BlockSpec((TM_OUT, FOUR_W), row),
                pl.BlockSpec((TM_OUT, D), row),
                modspec(2), modspec(3), modspec(4),
                pl.BlockSpec((1, D), const), pl.BlockSpec((1, D), const), pl.BlockSpec((D, D), const)]
    args = [attn_o, four_o, x, mod, mod, mod, g_post.reshape(1, D), g_ffn_pre.reshape(1, D), w_out_bf]
    out_specs = [pl.BlockSpec((TM_OUT, D), row), pl.BlockSpec((TM_OUT, D), row)]
    out_shape = [jax.ShapeDtypeStruct((rows, D), F32), jax.ShapeDtypeStruct((rows, D), BF16)]
    if moe:
        in_specs.append(pl.BlockSpec((D, E_PAD), const))
        args.append(w_router_pad)
        out_specs += [pl.BlockSpec((TM_OUT, E_PAD), row)] * 2
        out_shape += [jax.ShapeDtypeStruct((rows, E_PAD), F32), jax.ShapeDtypeStruct((rows, E_PAD), BF16)]
    return pl.pallas_call(
        functools.partial(_outproj_body, moe=moe, group0=group0, tiles_per_group=blocks_per_group),
        grid=(rows // TM_OUT,),
        in_specs=in_specs, out_specs=out_specs, out_shape=out_shape,
        compiler_params=_cparams(("arbitrary",)),
        name="outproj_moe" if moe else "outproj",
    )(*args)


def _swiglu(hn, wg, wu, wd):
    g = jnp.dot(hn, wg, preferred_element_type=F32)
    u = jnp.dot(hn, wu, preferred_element_type=F32)
    return jnp.dot((_silu(g) * u).astype(BF16), wd, preferred_element_type=F32)


def _ffn_dense_body(hn_ref, x1_ref, gt_ref, gpost_ref, wg_ref, wu_ref, wd_ref, o_ref, *, group0, tiles_per_group):
    grp = _group_of(pl.program_id(0), group0, tiles_per_group)
    ff = _swiglu(hn_ref[...], wg_ref[...], wu_ref[...], wd_ref[...])
    o_ref[...] = x1_ref[...] + gt_ref[pl.ds(grp, 1), :] * _rms(ff, gpost_ref[...])


def _ffn_dense(hn, x1, mod, layer, g_post, wg, wu, wd, group0, tiles_per_group):
    rows = x1.shape[0]
    row = lambda i: (i, 0)
    const = lambda i: (0, 0)
    resident = lambda shape: pl.BlockSpec(shape, const, pipeline_mode=pl.Buffered(1))
    return pl.pallas_call(
        functools.partial(_ffn_dense_body, group0=group0, tiles_per_group=tiles_per_group),
        grid=(rows // TM,),
        in_specs=[pl.BlockSpec((TM, D), row), pl.BlockSpec((TM, D), row),
                  pl.BlockSpec((None, N_GROUPS_PAD, D), lambda i: (layer, 0, 5)),
                  pl.BlockSpec((1, D), const),
                  resident((D, D_FF)), resident((D, D_FF)), resident((D_FF, D))],
        out_specs=pl.BlockSpec((TM, D), row),
        out_shape=jax.ShapeDtypeStruct((rows, D), F32),
        compiler_params=_cparams(("arbitrary",)),
        name="ffn_dense",
    )(hn, x1, mod, g_post.reshape(1, D), wg, wu, wd)


def _experts_body(te_ref, tn_ref, tnext_ref, xs_ref, wg_hbm, wu_hbm, wd_hbm, ys_ref, w32, wg_bf, wu_bf, wd_bf, sem):
    i = pl.program_id(0)
    e = te_ref[i]
    fresh = (i == 0) | (e != te_ref[jnp.maximum(i - 1, 0)])

    def fetch(expert, act):
        for j, hbm in enumerate((wg_hbm, wu_hbm, wd_hbm)):
            act(pltpu.make_async_copy(hbm.at[expert], w32.at[j], sem))

    @pl.when(i == 0)
    def _():
        fetch(e, lambda c: c.start())

    @pl.when(fresh)
    def _():
        fetch(e, lambda c: c.wait())
        wg_bf[...] = w32[0].astype(BF16)
        wu_bf[...] = w32[1].astype(BF16)
        wd_bf[...] = w32[2].astype(BF16)
        nxt = tnext_ref[i]

        @pl.when(nxt != e)
        def _():
            fetch(nxt, lambda c: c.start())

    n_real = tn_ref[i]

    @pl.when(n_real > 0)
    def _():
        ys_ref[...] = _swiglu(xs_ref[...].astype(BF16), wg_bf[...], wu_bf[...], wd_bf[...])

    @pl.when(n_real == 0)
    def _():
        ys_ref[...] = jnp.zeros_like(ys_ref)


def _experts(xs, tile_e, tile_n, tile_next, eg, eu, ed):
    assert D == D_FF_E
    tile = lambda i, te, tn, tx: (i, 0)
    return pl.pallas_call(
        _experts_body,
        grid_spec=pltpu.PrefetchScalarGridSpec(
            num_scalar_prefetch=3,
            grid=(N_TILES,),
            in_specs=[pl.BlockSpec((TMOE, D), tile)] + [pl.BlockSpec(memory_space=pl.ANY)] * 3,
            out_specs=pl.BlockSpec((TMOE, D), tile),
            scratch_shapes=[pltpu.VMEM((3, D, D), F32),
                            pltpu.VMEM((D, D_FF_E), BF16), pltpu.VMEM((D, D_FF_E), BF16),
                            pltpu.VMEM((D_FF_E, D), BF16), pltpu.SemaphoreType.DMA]),
        out_shape=jax.ShapeDtypeStruct((NP, D), F32),
        compiler_params=_cparams(("arbitrary",)),
        name="moe_experts",
    )(tile_e, tile_n, tile_next, xs, eg, eu, ed)


def _route_body(*refs, firsts):
    ns = len(firsts)
    gates_refs, sel_refs = refs[:ns], refs[ns:2 * ns]
    ltri_ref, w_ref, rho_ref, tmeta_ref, meta_ref, cnt_ref, off_ref, run_ref = refs[2 * ns:]
    phase, t = pl.program_id(0), pl.program_id(1)

    def this_stream(stream_refs):
        val = stream_refs[0][...]
        for k in range(1, ns):
            val = jnp.where(t >= firsts[k], stream_refs[k][...], val)
        return val

    sel = this_stream(sel_refs)
    group_rows = jnp.ceil(jnp.sum(sel.astype(F32), axis=0, keepdims=True) * (1.0 / GROUP)) * GROUP

    @pl.when((phase == 0) & (t == 0))
    def _():
        cnt_ref[...] = jnp.zeros_like(cnt_ref)

    @pl.when(phase == 0)
    def _():
        cnt_ref[...] += group_rows

    @pl.when((phase == 1) & (t == 0))
    def _():
        off_ref[...] = _segment_meta(cnt_ref[...], meta_ref)
        run_ref[...] = jnp.zeros_like(run_ref)

    @pl.when(phase == 1)
    def _():
        lane1 = lax.broadcasted_iota(jnp.int32, group_rows.shape, 1).astype(F32)
        buf0 = jnp.zeros_like(group_rows)
        running = jnp.zeros((1, 1), F32)
        for e in range(N_EXPERTS):
            pick = lane1 == float(e)
            buf0 = jnp.where(pick, running, buf0)
            running = running + jnp.sum(jnp.where(pick, group_rows, 0.0), axis=-1, keepdims=True)
        slot0 = off_ref[...] + run_ref[...]
        run_ref[...] += group_rows
        row8 = lax.broadcasted_iota(jnp.int32, tmeta_ref.shape, 0)
        tmeta_ref[...] = jnp.where(row8 == 0, slot0, jnp.where(row8 == 1, group_rows,
                                   jnp.where(row8 == 2, buf0, 0.0)))

        earlier = jnp.dot(ltri_ref[...], sel, preferred_element_type=F32)
        rho = earlier + buf0
        chosen = sel > 0
        r_lo = jnp.min(jnp.where(chosen, rho, float(R_TILE)), axis=-1, keepdims=True)
        r_hi = jnp.max(jnp.where(chosen, rho, -1.0), axis=-1, keepdims=True)
        g = this_stream(gates_refs)
        w_lo = jnp.sum(jnp.where(chosen & (rho == r_lo), g, 0.0), axis=-1, keepdims=True)
        w_hi = jnp.sum(jnp.where(chosen & (rho == r_hi), g, 0.0), axis=-1, keepdims=True)
        lane = lax.broadcasted_iota(jnp.int32, rho.shape, 1)
        w_ref[...] = jnp.where(lane == 2, w_lo, jnp.where(lane == 3, w_hi,
                               jnp.where(lane == 4, r_lo, jnp.where(lane == 5, r_hi, 0.0))))
        cols = jnp.where(lane == 0, r_lo, jnp.where(lane == 1, r_hi, 0.0))
        rho_ref[...] = jnp.transpose(cols)[0:N_GROUPS_PAD, :]


def _segment_meta(cnt, meta_ref):
    padded = jnp.ceil(cnt * (1.0 / TMOE)) * TMOE
    lane = lax.broadcasted_iota(jnp.int32, cnt.shape, 1).astype(F32)
    pad_of = lambda e: jnp.sum(jnp.where(lane == float(e), padded, 0.0), axis=-1, keepdims=True)
    following = [None] * N_EXPERTS
    seen = jnp.full((1, 1), -1.0, F32)
    for e in reversed(range(N_EXPERTS)):
        following[e] = seen
        seen = jnp.where(pad_of(e) > 0.0, float(e), seen)
    start = lane * TMOE
    off = jnp.zeros_like(cnt)
    tile_e = jnp.zeros_like(cnt)
    tile_n = jnp.zeros_like(cnt)
    tile_next = jnp.zeros_like(cnt)
    running = jnp.zeros((1, 1), F32)
    last_e = jnp.zeros((1, 1), F32)
    for e in range(N_EXPERTS):
        pick = lane == float(e)
        cnt_e = jnp.sum(jnp.where(pick, cnt, 0.0), axis=-1, keepdims=True)
        pad_e = pad_of(e)
        off = jnp.where(pick, running, off)
        in_seg = (start >= running) & (start < running + pad_e)
        tile_e = jnp.where(in_seg, float(e), tile_e)
        tile_n = jnp.where(in_seg, jnp.clip(cnt_e - (start - running), 0.0, float(TMOE)), tile_n)
        tile_next = jnp.where(in_seg, jnp.where(following[e] >= 0.0, following[e], float(e)), tile_next)
        last_e = jnp.where(pad_e > 0.0, float(e), last_e)
        running = running + pad_e
    tile_e = jnp.where(start >= running, last_e, tile_e)
    tile_next = jnp.where(start >= running, last_e, tile_next)
    row = lax.broadcasted_iota(jnp.int32, meta_ref.shape, 0)
    last_tile = jnp.where(lane < float(N_EXPERTS), off + padded - TMOE, running)
    last_tile = jnp.where((lane < float(N_EXPERTS)) & (padded == 0.0), -1.0, last_tile)
    meta_ref[...] = jnp.where(row == 0, tile_e, jnp.where(row == 1, tile_n,
                              jnp.where(row == 2, last_tile, jnp.where(row == 3, tile_next, 0.0))))
    return off


def _route(gates, sels):
    tiles = [g.shape[0] // TM for g in gates]
    firsts = [int(f) for f in np.cumsum([0] + tiles)]
    n, rows = firsts[-1], firsts[-1] * TM
    ltri = jnp.asarray(np.tril(np.ones((TM, TM)), -1), F32).astype(BF16)
    stream_spec = lambda lo, cnt: pl.BlockSpec((TM, E_PAD), lambda p, t: (jnp.clip(t - lo, 0, cnt - 1), 0))
    stream_specs = [stream_spec(lo, cnt) for lo, cnt in zip(firsts[:-1], tiles)]
    per_tile = lambda p, t: (t * p, 0, 0)
    return pl.pallas_call(
        functools.partial(_route_body, firsts=tuple(firsts[:-1])),
        grid=(2, n),
        in_specs=stream_specs + stream_specs + [pl.BlockSpec((TM, TM), lambda p, t: (0, 0))],
        out_specs=[pl.BlockSpec((TM, E_PAD), lambda p, t: (t * p, 0)),
                   pl.BlockSpec((None, N_GROUPS_PAD, TM), per_tile),
                   pl.BlockSpec((None, N_GROUPS_PAD, E_PAD), per_tile),
                   pl.BlockSpec((N_GROUPS_PAD, E_PAD), lambda p, t: (0, 0))],
        out_shape=[jax.ShapeDtypeStruct((rows, E_PAD), F32),
                   jax.ShapeDtypeStruct((n, N_GROUPS_PAD, TM), F32),
                   jax.ShapeDtypeStruct((n, N_GROUPS_PAD, E_PAD), F32),
                   jax.ShapeDtypeStruct((N_GROUPS_PAD, E_PAD), F32)],
        scratch_shapes=[pltpu.VMEM((1, E_PAD), F32)] * 3,
        compiler_params=_cparams(("arbitrary", "arbitrary")),
        name="moe_route",
    )(*gates, *sels, ltri)


def _group_copies(tmeta_ref, slot_window, buf_window, sem, to_slots, act):
    for e in range(N_EXPERTS):
        slot0, n, buf0 = tmeta_ref[0, e], tmeta_ref[1, e], tmeta_ref[2, e]
        done = jnp.int32(0)
        size = TM
        while size >= GROUP:
            take = (n - done) >= size
            s_at = pl.ds(pl.multiple_of(slot0 + done, GROUP), size)
            b_at = pl.ds(pl.multiple_of(buf0 + done, GROUP), size)
            src, dst = (buf_window(b_at), slot_window(s_at)) if to_slots else (slot_window(s_at), buf_window(b_at))

            @pl.when(take)
            def _(src=src, dst=dst):
                act(pltpu.make_async_copy(src, dst, sem))

            done = done + jnp.where(take, size, 0)
            size //= 2


def _dispatch_body(tmeta_ref, prev_tmeta_ref, pad_ref, rho_ref, *rest, firsts):
    ns = len(firsts)
    hn_refs, (xs_ref, pbuf, zeros, sems) = rest[:ns], rest[ns:]
    i = pl.program_id(0)
    cur = lax.rem(i, 2)
    sem = sems.at[0]

    @pl.when(i == 0)
    def _():
        zeros[...] = jnp.zeros_like(zeros)
        fills = []
        for e in range(N_EXPERTS):
            s = pad_ref[0, e]
            fills.append((s >= 0, pl.multiple_of(jnp.maximum(s, 0), TMOE)))
        for j in range(N_TILES - TOP_K * N_TOKENS // TMOE):
            s = NP - (j + 1) * TMOE
            fills.append((s >= pad_ref[0, N_EXPERTS], s))
        fill = lambda s: pltpu.make_async_copy(zeros, xs_ref.at[pl.ds(s, TMOE), :], sem)
        for live, s in fills:
            @pl.when(live)
            def _(s=s):
                fill(s).start()
        for live, s in fills:
            @pl.when(live)
            def _(s=s):
                fill(s).wait()

    hn = hn_refs[0][...]
    for k in range(1, ns):
        hn = jnp.where(i >= firsts[k], hn_refs[k][...], hn)
    row = lax.broadcasted_iota(jnp.int32, (R_TILE, TM), 0).astype(F32)
    onehot = jnp.where((row == rho_ref[0:1, :]) | (row == rho_ref[1:2, :]), 1.0, 0.0).astype(BF16)
    pbuf[cur] = jnp.dot(onehot, hn, preferred_element_type=F32)

    def store(meta_ref, half, act):
        _group_copies(meta_ref, lambda at: xs_ref.at[at, :], lambda at: pbuf.at[half, at, :], sems.at[half],
                      True, act)

    store(tmeta_ref, cur, lambda c: c.start())

    @pl.when(i > 0)
    def _():
        store(prev_tmeta_ref, 1 - cur, lambda c: c.wait())

    @pl.when(i + 1 == pl.num_programs(0))
    def _():
        store(tmeta_ref, cur, lambda c: c.wait())


def _dispatch(hns, tmeta, rho, pad_starts):
    tiles = [h.shape[0] // TM for h in hns]
    firsts = [int(f) for f in np.cumsum([0] + tiles)]
    n = firsts[-1]
    stream_spec = lambda lo, cnt: pl.BlockSpec((TM, D), lambda i: (jnp.clip(i - lo, 0, cnt - 1), 0))
    return pl.pallas_call(
        functools.partial(_dispatch_body, firsts=tuple(firsts[:-1])),
        grid=(n,),
        in_specs=[pl.BlockSpec((None, N_GROUPS_PAD, E_PAD), lambda i: (i, 0, 0), memory_space=pltpu.SMEM),
                  pl.BlockSpec((None, N_GROUPS_PAD, E_PAD), lambda i: (jnp.maximum(i - 1, 0), 0, 0),
                               memory_space=pltpu.SMEM),
                  pl.BlockSpec(memory_space=pltpu.SMEM),
                  pl.BlockSpec((None, N_GROUPS_PAD, TM), lambda i: (i, 0, 0))]
                 + [stream_spec(lo, cnt) for lo, cnt in zip(firsts[:-1], tiles)],
        out_specs=pl.BlockSpec(memory_space=pl.ANY),
        out_shape=jax.ShapeDtypeStruct((NP, D), F32),
        scratch_shapes=[pltpu.VMEM((2, R_TILE, D), F32), pltpu.VMEM((TMOE, D), F32),
                        pltpu.SemaphoreType.DMA((2,))],
        compiler_params=_cparams(("arbitrary",)),
        name="moe_dispatch",
    )(tmeta, tmeta, pad_starts, rho, *hns)


def _combine_body(tmeta_ref, next_tmeta_ref, w_ref, x1_ref, gt_ref, gpost_ref, ys_ref, o_ref, ybuf, sems,
                   *, group0, tiles_per_group):
    i = pl.program_id(0)
    grp = _group_of(i, group0, tiles_per_group)
    cur = lax.rem(i, 2)

    def fetch(meta_ref, half, act):
        _group_copies(meta_ref, lambda at: ys_ref.at[at, :], lambda at: ybuf.at[half, at, :], sems.at[half],
                      False, act)

    @pl.when(i == 0)
    def _():
        ybuf[...] = jnp.zeros_like(ybuf)
        fetch(tmeta_ref, 0, lambda c: c.start())

    @pl.when(i + 1 < pl.num_programs(0))
    def _():
        fetch(next_tmeta_ref, 1 - cur, lambda c: c.start())

    fetch(tmeta_ref, cur, lambda c: c.wait())
    w = w_ref[...]
    lane = lax.broadcasted_iota(jnp.int32, w.shape, 1)
    col = lambda j: jnp.sum(jnp.where(lane == j, w, 0.0), axis=-1, keepdims=True)
    w_lo, w_hi, r_lo, r_hi = col(2), col(3), col(4), col(5)
    y = ybuf[cur].astype(BF16)
    pos = lax.broadcasted_iota(jnp.int32, (TM, R_TILE), 1).astype(F32)
    pick = lambda r: jnp.dot(jnp.where(pos == r, 1.0, 0.0).astype(BF16), y, preferred_element_type=F32)
    ff = w_lo * pick(r_lo) + w_hi * pick(r_hi)
    o_ref[...] = x1_ref[...] + gt_ref[pl.ds(grp, 1), :] * _rms(ff, gpost_ref[...])


def _combine(ys, tmeta, slot_w, tile0, x1, mod, layer, g_post, group0, tiles_per_group):
    rows = x1.shape[0]
    n = rows // TM
    row = lambda i: (i, 0)
    return pl.pallas_call(
        functools.partial(_combine_body, group0=group0, tiles_per_group=tiles_per_group),
        grid=(n,),
        in_specs=[pl.BlockSpec((None, N_GROUPS_PAD, E_PAD), lambda i: (tile0 + i, 0, 0), memory_space=pltpu.SMEM),
                  pl.BlockSpec((None, N_GROUPS_PAD, E_PAD), lambda i: (tile0 + jnp.minimum(i + 1, n - 1), 0, 0),
                               memory_space=pltpu.SMEM),
                  pl.BlockSpec((TM, E_PAD), lambda i: (tile0 + i, 0)), pl.BlockSpec((TM, D), row),
                  pl.BlockSpec((None, N_GROUPS_PAD, D), lambda i: (layer, 0, 5)),
                  pl.BlockSpec((1, D), lambda i: (0, 0)),
                  pl.BlockSpec(memory_space=pl.ANY)],
        out_specs=pl.BlockSpec((TM, D), row),
        out_shape=jax.ShapeDtypeStruct((rows, D), F32),
        scratch_shapes=[pltpu.VMEM((2, R_TILE, D), F32), pltpu.SemaphoreType.DMA((2,))],
        compiler_params=_cparams(("arbitrary",)),
        name="moe_combine",
    )(tmeta, tmeta, slot_w, x1, mod, g_post.reshape(1, D), ys)


def kernel(x_prompt, x_sample, cache_k, cache_v, c, c_ctx, g_attn_pre, g_attn_post, g_ffn_pre, g_ffn_post,
           w_ada, b_ada, w_in, lam_params, g_subln, w_fnet, w_out, w_gate, w_up, w_down, w_router,
           e_gate, e_up, e_down):
    assert x_prompt.shape == (N_CTX_B, T_CTX, D) and x_sample.shape == (N_LAT_B, T_LAT, D)
    assert cache_k.shape == (N_LAT_B, DEPTH, T_PAST, N_HEADS, 2, QK_DIM)

    cond = jnp.zeros((N_GROUPS_PAD, D), F32).at[0].set(c_ctx).at[1:1 + N_LAT_B].set(c)
    mod = _ada(cond, w_ada, b_ada)
    wfold_bf = _wfold(w_fnet).astype(BF16)
    rope_tabs = _rope_tables()
    fft_consts = _fft_consts()

    w_in_bf = w_in.astype(BF16)
    w_out_bf = w_out.astype(BF16)
    w_router_pad = jnp.zeros((DEPTH // 2, D, E_PAD), F32).at[:, :, :N_EXPERTS].set(w_router)
    ck_bf = cache_k.reshape(N_LAT_B, DEPTH, T_PAST, ATTN_W).astype(BF16)
    cv_bf = cache_v.reshape(N_LAT_B, DEPTH, T_PAST, ATTN_W).astype(BF16)

    lat_tiles = T_LAT // TM
    streams = [dict(x=x_prompt.reshape(N_CTX_B * T_CTX, D), group0=0, tiles=None, lat=False),
               dict(x=x_sample.reshape(N_LAT_B * T_LAT, D), group0=1, tiles=lat_tiles, lat=True)]
    new_k = new_v = None
    for l in range(DEPTH):
        lam_init = 0.8 - 0.6 * math.exp(-0.3 * l)
        i = l // 2
        routed = []
        for s in streams:
            g0, tiles = s["group0"], s["tiles"]
            if s["lat"]:
                qkv, f = _inproj(s["x"], mod, l, g_attn_pre[l], w_in_bf[l], rope_tabs, g0, tiles)
                attn_o = _attn_lat(qkv, ck_bf[:, l], cv_bf[:, l], lam_params, g_subln, l, lam_init)
                attn_o = attn_o.reshape(N_LAT_B * T_LAT, ATTN_W)
                four_o = _four_lat(f, wfold_bf, l, fft_consts)
            else:
                qkv, f, new_k, new_v = _inproj(s["x"], mod, l, g_attn_pre[l], w_in_bf[l], None, g0, tiles,
                                               cache=None if l == 0 else (new_k, new_v))
                attn_o = _attn_ctx(qkv, lam_params, g_subln, l, lam_init).reshape(N_CTX_B * T_CTX, ATTN_W)
                four_o = _four_ctx(f, wfold_bf, l).reshape(N_CTX_B * T_CTX, FOUR_W)
            if l % 2 == 0:
                x1, hn = _outproj(attn_o, four_o, s["x"], mod, l, g_attn_post[l], g_ffn_pre[l], w_out_bf[l],
                                  None, g0, tiles)
                s["x"] = _ffn_dense(hn, x1, mod, l, g_ffn_post[l], w_gate[i].astype(BF16), w_up[i].astype(BF16),
                                    w_down[i].astype(BF16), g0, tiles)
            else:
                routed.append(_outproj(attn_o, four_o, s["x"], mod, l, g_attn_post[l], g_ffn_pre[l],
                                       w_out_bf[l], w_router_pad[i], g0, tiles))
        if l % 2 == 1:
            slot_w, rho, tmeta, meta = _route([r[2] for r in routed], [r[3] for r in routed])
            tmeta = tmeta.astype(jnp.int32)
            tile_e = meta[0, :N_TILES].astype(jnp.int32)
            tile_n = meta[1, :N_TILES].astype(jnp.int32)
            tile_next = meta[3, :N_TILES].astype(jnp.int32)
            pad_starts = meta[2:3, :N_EXPERTS + 1].astype(jnp.int32)
            bounds = np.cumsum([0] + [r[0].shape[0] for r in routed])
            xs = _dispatch([r[1] for r in routed], tmeta, rho, pad_starts)
            ys = _experts(xs, tile_e, tile_n, tile_next, e_gate[i], e_up[i], e_down[i])
            for s, r, lo in zip(streams, routed, bounds[:-1]):
                s["x"] = _combine(ys, tmeta, slot_w, int(lo) // TM, r[0], mod, l, g_ffn_post[l],
                                   s["group0"], s["tiles"])

    y_prompt = streams[0]["x"].reshape(N_CTX_B, T_CTX, D)
    y_sample = streams[1]["x"].reshape(N_LAT_B, T_LAT, D)
    new_cache_k = new_k.reshape(N_CTX_B, DEPTH, T_CTX, N_HEADS, 2, QK_DIM)
    new_cache_v = new_v.reshape(N_CTX_B, DEPTH, T_CTX, N_HEADS, HEAD_W)
    return (y_prompt, y_sample, new_cache_k, new_cache_v)
```

```python
import functools
import math

import numpy as np
import jax
import jax.numpy as jnp
from jax import lax
from jax.experimental import pallas as pl
from jax.experimental.pallas import tpu as pltpu

F32 = jnp.float32
BF16 = jnp.bfloat16

D = 1024
N_CTX_B = 16
T_CTX = 256
N_LAT_B = 2
T_LAT = 4096
T_PAST = 512
DEPTH = 2
GRID_W = 64
N_HEADS = 4
QK_DIM = 64
HEAD_W = 2 * QK_DIM
ATTN_W = N_HEADS * HEAD_W
F_GROUPS = 4
F_CH = 128
FOUR_W = F_GROUPS * F_CH
IN_W = 3 * ATTN_W + FOUR_W
QKV_W = 3 * ATTN_W
D_FF = 2816
N_EXPERTS = 8
D_FF_E = 1024
N_MOD = 6
EPS = 1e-6
ROPE_THETA = 10000.0
N_GROUPS_PAD = 8
E_PAD = 128

TM = 512
CTX_PER_STEP = 4
TM_OUT = 1024
TOP_K = 2
TMOE = 512
N_TOKENS = N_CTX_B * T_CTX + N_LAT_B * T_LAT
GROUP = 8
R_TILE = TOP_K * TM + N_EXPERTS * GROUP
N_TILES = -(-(TOP_K * N_TOKENS + (N_TOKENS // TM) * N_EXPERTS * (GROUP - 1)) // TMOE) + N_EXPERTS
NP = N_TILES * TMOE
TQ = 512
Q_SUB = 2
TK = 1536
ATTN_UNROLL = 3
FFT_R = 64
FFT_S = 8
VMEM_LIMIT = 56 * 1024 * 1024


def _cparams(sem):
    return pltpu.CompilerParams(dimension_semantics=sem, vmem_limit_bytes=VMEM_LIMIT)


def _rms(x, g):
    ms = jnp.mean(x * x, axis=-1, keepdims=True)
    return x * lax.rsqrt(ms + EPS) * g


def _silu(x):
    return x * (1.0 / (1.0 + jnp.exp(-x)))


def _ada_body(c_ref, w_ref, b_ref, o_ref):
    s = _silu(c_ref[...])
    s_hi = s.astype(BF16)
    s_lo = (s - s_hi.astype(F32)).astype(BF16)
    both = jnp.dot(jnp.concatenate([s_hi, s_lo], axis=0), w_ref[...].astype(BF16), preferred_element_type=F32)
    o_ref[...] = both[:N_GROUPS_PAD] + both[N_GROUPS_PAD:] + b_ref[...]


def _ada(cond, w_ada, b_ada):
    tn = 1536
    return pl.pallas_call(
        _ada_body,
        grid=(DEPTH, N_MOD * D // tn),
        in_specs=[pl.BlockSpec((N_GROUPS_PAD, D), lambda l, j: (0, 0)),
                  pl.BlockSpec((None, D, tn), lambda l, j: (l, 0, j)),
                  pl.BlockSpec((None, 1, tn), lambda l, j: (l, 0, j))],
        out_specs=pl.BlockSpec((None, N_GROUPS_PAD, tn), lambda l, j: (l, 0, j)),
        out_shape=jax.ShapeDtypeStruct((DEPTH, N_GROUPS_PAD, N_MOD * D), F32),
        compiler_params=_cparams(("arbitrary", "arbitrary")),
        name="ada_mod",
    )(cond, w_ada, b_ada.reshape(DEPTH, 1, N_MOD * D))


def _wfold_body(cc_ref, sc_ref, w_ref, o_ref):
    w = w_ref[...]
    o_ref[0:F_CH, :] = jnp.dot(cc_ref[...], w, preferred_element_type=F32,
                               precision=lax.Precision.HIGHEST)
    o_ref[F_CH:2 * F_CH, :] = jnp.dot(sc_ref[...], w, preferred_element_type=F32,
                                      precision=lax.Precision.HIGHEST)


def _wfold(w_fnet):
    ang = 2.0 * np.pi * np.outer(np.arange(F_CH), np.arange(F_CH)) / F_CH
    cc = jnp.asarray(np.cos(ang), F32)
    sc = jnp.asarray(np.sin(ang), F32)
    cspec = pl.BlockSpec((F_CH, F_CH), lambda l, g: (0, 0))
    return pl.pallas_call(
        _wfold_body,
        grid=(DEPTH, F_GROUPS),
        in_specs=[cspec, cspec, pl.BlockSpec((None, None, F_CH, F_CH), lambda l, g: (l, g, 0, 0))],
        out_specs=pl.BlockSpec((None, None, 2 * F_CH, F_CH), lambda l, g: (l, g, 0, 0)),
        out_shape=jax.ShapeDtypeStruct((DEPTH, F_GROUPS, 2 * F_CH, F_CH), F32),
        compiler_params=_cparams(("arbitrary", "arbitrary")),
        name="fnet_fold",
    )(cc, sc, w_fnet)


def _group_of(i, group0, tiles_per_group):
    if tiles_per_group is None:
        return group0
    return group0 + lax.div(i, jnp.int32(tiles_per_group))


def _inproj_body(x_ref, sh_ref, sc_ref, g_ref, w_ref, *rest, rope, cache_layer, group0, tiles_per_group):
    if rope:
        cos_ref, sin_ref, qkv_ref, f_ref = rest
    else:
        qkv_ref, f_ref, k32_ref, v32_ref = rest[-4:]
    grp = _group_of(pl.program_id(0), group0, tiles_per_group)
    sh = sh_ref[pl.ds(grp, 1), :]
    sc = sc_ref[pl.ds(grp, 1), :]
    hn = _rms(x_ref[...], g_ref[...]) * (1.0 + sc) + sh
    proj = jnp.dot(hn.astype(BF16), w_ref[...], preferred_element_type=F32)
    qk = proj[:, :2 * ATTN_W]
    v = proj[:, 2 * ATTN_W:QKV_W]
    if rope:
        cos = jnp.tile(cos_ref[...], (1, 2 * ATTN_W // HEAD_W))
        sin = jnp.tile(sin_ref[...], (1, 2 * ATTN_W // HEAD_W))
        lane = lax.broadcasted_iota(jnp.int32, qk.shape, 1)
        low = (lane % 32) < 16
        rot = jnp.where(low, pltpu.roll(qk, 2 * ATTN_W - 16, 1), pltpu.roll(qk, 16, 1))
        qk = qk * cos + rot * sin
    else:
        for ref, val in ((k32_ref, qk[:, ATTN_W:]), (v32_ref, v)):
            val = val.reshape(TM // T_CTX, T_CTX, ATTN_W)
            if cache_layer == 0:
                ref[:, 0] = val
                ref[:, 1:] = jnp.zeros((TM // T_CTX, DEPTH - 1, T_CTX, ATTN_W), F32)
            else:
                ref[...] = val
    qkv_ref[:, 0:ATTN_W] = (qk[:, :ATTN_W] * (QK_DIM ** -0.5 * math.log2(math.e))).astype(BF16)
    qkv_ref[:, ATTN_W:2 * ATTN_W] = qk[:, ATTN_W:].astype(BF16)
    qkv_ref[:, 2 * ATTN_W:] = v.astype(BF16)
    f_ref[...] = proj[:, QKV_W:]


def _inproj(x, mod, layer, g_pre, w_in_bf, rope_tabs, group0, tiles_per_group, cache=None):
    rows = x.shape[0]
    rope = rope_tabs is not None
    aliases = {}
    row = lambda i: (i, 0)
    const = lambda i: (0, 0)
    in_specs = [pl.BlockSpec((TM, D), row),
                pl.BlockSpec((None, N_GROUPS_PAD, D), lambda i: (layer, 0, 0)),
                pl.BlockSpec((None, N_GROUPS_PAD, D), lambda i: (layer, 0, 1)),
                pl.BlockSpec((1, D), const),
                pl.BlockSpec((D, IN_W), const)]
    args = [x, mod, mod, g_pre.reshape(1, D), w_in_bf]
    out_specs = [pl.BlockSpec((TM, QKV_W), row), pl.BlockSpec((TM, FOUR_W), row)]
    out_shape = [jax.ShapeDtypeStruct((rows, QKV_W), BF16), jax.ShapeDtypeStruct((rows, FOUR_W), F32)]
    if rope:
        tiles_per_seq = T_LAT // TM
        tab = pl.BlockSpec((TM, HEAD_W), lambda i: (i % tiles_per_seq, 0))
        in_specs += [tab, tab]
        args += list(rope_tabs)
    else:
        nb = TM // T_CTX
        if cache is None:
            cache_spec = pl.BlockSpec((nb, DEPTH, T_CTX, ATTN_W), lambda i: (i, 0, 0, 0))
        else:
            cache_spec = pl.BlockSpec((nb, None, T_CTX, ATTN_W), lambda i: (i, layer, 0, 0))
            aliases = {len(args): 2, len(args) + 1: 3}
            in_specs += [pl.BlockSpec(memory_space=pl.ANY)] * 2
            args += list(cache)
        out_specs += [cache_spec] * 2
        out_shape += [jax.ShapeDtypeStruct((N_CTX_B, DEPTH, T_CTX, ATTN_W), F32)] * 2
    return pl.pallas_call(
        functools.partial(_inproj_body, rope=rope, cache_layer=None if rope else (0 if cache is None else layer),
                          group0=group0, tiles_per_group=tiles_per_group),
        grid=(rows // TM,),
        in_specs=in_specs, out_specs=out_specs, out_shape=out_shape,
        input_output_aliases=aliases,
        compiler_params=_cparams(("arbitrary",)),
        name="inproj_lat" if rope else "inproj_ctx",
    )(*args)


def _rope_tables():
    half = QK_DIM // 2
    inv = 1.0 / (ROPE_THETA ** (np.arange(0, half, 2, dtype=np.float64) / half))
    pos = np.arange(T_LAT)
    def tab(p):
        ang = p[:, None].astype(np.float64) * inv[None, :]
        return np.concatenate([ang, ang], axis=-1)
    ang = np.concatenate([tab(pos // GRID_W), tab(pos % GRID_W)], axis=-1)
    sign = np.where((np.arange(QK_DIM) % 32) < 16, -1.0, 1.0)
    cos = np.tile(np.cos(ang), (1, 2))
    sin = np.tile(np.sin(ang) * sign[None, :], (1, 2))
    return jnp.asarray(cos, F32), jnp.asarray(sin, F32)


def _lam(lam_ref, lam_init):
    lp = lam_ref[...]
    return (jnp.exp(jnp.sum(lp[0:1] * lp[1:2], keepdims=True))
            - jnp.exp(jnp.sum(lp[2:3] * lp[3:4], keepdims=True)) + lam_init)


def _stack_maps(q):
    lane = lax.broadcasted_iota(jnp.int32, q.shape, 1)
    zero = jnp.zeros_like(q)
    return jnp.concatenate([jnp.where(lane < QK_DIM, q, zero), jnp.where(lane >= QK_DIM, q, zero)], axis=0)


def _softmax_step(qq, kb, vb, carry):
    m, l, acc = carry
    s = lax.dot_general(qq, kb, (((1,), (1,)), ((), ())), preferred_element_type=F32)
    m_new = jnp.maximum(m, jnp.max(s, axis=-1, keepdims=True))
    alpha = jnp.exp2(m - m_new)
    p = jnp.exp2(s - m_new)
    l = alpha * l + jnp.sum(p, axis=-1, keepdims=True)
    acc = alpha * acc + jnp.dot(p.astype(BF16), vb, preferred_element_type=F32)
    return m_new, l, acc


def _softmax_init(rows):
    return (jnp.full((rows, 1), -jnp.inf, F32), jnp.zeros((rows, 1), F32), jnp.zeros((rows, HEAD_W), F32))


def _diff_out(carry, tq, lam, gs, lam_init):
    _, l, acc = carry
    o = acc / l
    a = o[:tq] - lam * o[tq:]
    return _rms(a, gs) * (1.0 - lam_init)


def _attn_ctx_body(lam_ref, gs_ref, q_ref, k_ref, v_ref, o_ref, *, lam_init):
    lam = _lam(lam_ref, lam_init)
    for b in range(CTX_PER_STEP):
        for h in range(N_HEADS):
            sl = slice(h * HEAD_W, (h + 1) * HEAD_W)
            carry = _softmax_step(_stack_maps(q_ref[b, :, sl]), k_ref[b, :, sl], v_ref[b, :, sl],
                                  _softmax_init(2 * T_CTX))
            o_ref[b, :, sl] = _diff_out(carry, T_CTX, lam, gs_ref[...], lam_init).astype(o_ref.dtype)


def _attn_ctx(qkv, lam_params, g_subln, layer, lam_init):
    qkv3 = qkv.reshape(N_CTX_B, T_CTX, QKV_W)
    blk = lambda part: pl.BlockSpec((CTX_PER_STEP, T_CTX, ATTN_W), lambda b: (b, 0, part))
    return pl.pallas_call(
        functools.partial(_attn_ctx_body, lam_init=lam_init),
        grid=(N_CTX_B // CTX_PER_STEP,),
        in_specs=[pl.BlockSpec((None, 4, QK_DIM), lambda b: (layer, 0, 0)),
                  pl.BlockSpec((None, 1, HEAD_W), lambda b: (layer, 0, 0)),
                  blk(0), blk(1), blk(2)],
        out_specs=pl.BlockSpec((CTX_PER_STEP, T_CTX, ATTN_W), lambda b: (b, 0, 0)),
        out_shape=jax.ShapeDtypeStruct((N_CTX_B, T_CTX, ATTN_W), BF16),
        compiler_params=_cparams(("arbitrary",)),
        name="attn_ctx",
    )(lam_params, g_subln.reshape(DEPTH, 1, HEAD_W), qkv3, qkv3, qkv3)


def _attn_lat_body(lam_ref, gs_ref, q_ref, kc_ref, vc_ref, kl_ref, vl_ref, o_ref, kcat, vcat, *, lam_init):
    @pl.when(pl.program_id(2) == 0)
    def _():
        kcat[0:T_PAST, :] = kc_ref[...]
        kcat[T_PAST:, :] = kl_ref[...]
        vcat[0:T_PAST, :] = vc_ref[...]
        vcat[T_PAST:, :] = vl_ref[...]

    lam = _lam(lam_ref, lam_init)
    for sub in range(Q_SUB):
        rows = slice(sub * TQ, (sub + 1) * TQ)
        qq = _stack_maps(q_ref[rows, :])

        def body(c, carry, qq=qq):
            start = pl.multiple_of(c * TK, TK)
            return _softmax_step(qq, kcat[pl.ds(start, TK), :], vcat[pl.ds(start, TK), :], carry)

        carry = lax.fori_loop(0, (T_PAST + T_LAT) // TK, body, _softmax_init(2 * TQ), unroll=ATTN_UNROLL)
        o_ref[rows, :] = _diff_out(carry, TQ, lam, gs_ref[...], lam_init).astype(o_ref.dtype)


def _attn_lat(qkv, ck, cv, lam_params, g_subln, layer, lam_init):
    qkv3 = qkv.reshape(N_LAT_B, T_LAT, QKV_W)
    full = lambda off: pl.BlockSpec((None, T_LAT, HEAD_W), lambda b, h, i: (b, 0, off + h))
    past = pl.BlockSpec((None, T_PAST, HEAD_W), lambda b, h, i: (b, 0, h))
    return pl.pallas_call(
        functools.partial(_attn_lat_body, lam_init=lam_init),
        grid=(N_LAT_B, N_HEADS, T_LAT // (Q_SUB * TQ)),
        in_specs=[pl.BlockSpec((None, 4, QK_DIM), lambda b, h, i: (layer, 0, 0)),
                  pl.BlockSpec((None, 1, HEAD_W), lambda b, h, i: (layer, 0, 0)),
                  pl.BlockSpec((None, Q_SUB * TQ, HEAD_W), lambda b, h, i: (b, i, h)),
                  past, past, full(N_HEADS), full(2 * N_HEADS)],
        out_specs=pl.BlockSpec((None, Q_SUB * TQ, HEAD_W), lambda b, h, i: (b, i, h)),
        out_shape=jax.ShapeDtypeStruct((N_LAT_B, T_LAT, ATTN_W), BF16),
        scratch_shapes=[pltpu.VMEM((T_PAST + T_LAT, HEAD_W), BF16)] * 2,
        compiler_params=_cparams(("arbitrary", "arbitrary", "arbitrary")),
        name="attn_lat",
    )(lam_params, g_subln.reshape(DEPTH, 1, HEAD_W), qkv3, ck, cv, qkv3, qkv3)


def _fold_groups(ur, ui, wf_ref, scale):
    outs = []
    for g in range(F_GROUPS):
        sl = slice(g * F_CH, (g + 1) * F_CH)
        lhs = jnp.concatenate([ur[:, sl], ui[:, sl]], axis=1).astype(BF16)
        outs.append(jnp.dot(lhs, wf_ref[g], preferred_element_type=F32))
    return jnp.concatenate(outs, axis=1) * scale


def _four_ctx_body(f_ref, dft_ref, wf_ref, o_ref):
    for b in range(CTX_PER_STEP):
        u = jnp.dot(dft_ref[...], f_ref[b].astype(BF16), preferred_element_type=F32)
        o_ref[b] = _fold_groups(u[:T_CTX], u[T_CTX:], wf_ref, 1.0 / math.sqrt(T_CTX * F_CH)).astype(o_ref.dtype)


def _four_ctx(f, wfold_bf, layer):
    ang = 2.0 * np.pi * np.outer(np.arange(T_CTX), np.arange(T_CTX)) / T_CTX
    dft = jnp.asarray(np.concatenate([np.cos(ang), -np.sin(ang)], axis=0), F32).astype(BF16)
    return pl.pallas_call(
        _four_ctx_body,
        grid=(N_CTX_B // CTX_PER_STEP,),
        in_specs=[pl.BlockSpec((CTX_PER_STEP, T_CTX, FOUR_W), lambda b: (b, 0, 0)),
                  pl.BlockSpec((2 * T_CTX, T_CTX), lambda b: (0, 0)),
                  pl.BlockSpec((None, F_GROUPS, 2 * F_CH, F_CH), lambda b: (layer, 0, 0, 0))],
        out_specs=pl.BlockSpec((CTX_PER_STEP, T_CTX, FOUR_W), lambda b: (b, 0, 0)),
        out_shape=jax.ShapeDtypeStruct((N_CTX_B, T_CTX, FOUR_W), BF16),
        compiler_params=_cparams(("arbitrary",)),
        name="fourier_ctx",
    )(f.reshape(N_CTX_B, T_CTX, FOUR_W), dft, wfold_bf)


def _fft_body(x_ref, ma_ref, twc_ref, tws_ref, mc_ref, ms_ref, wf_ref, o_ref, h_ref):
    nb = FFT_R * FFT_S
    nj = FFT_R // FFT_S
    j = pl.program_id(1)

    @pl.when(j < nj)
    def _():
        x = x_ref[...].reshape(nb, FOUR_W).astype(BF16)
        g = jnp.dot(ma_ref[...], x, preferred_element_type=F32)
        gr, gi = g[:nb], g[nb:]
        twc = jnp.tile(twc_ref[...], (1, FOUR_W // 128))
        tws = jnp.tile(tws_ref[...], (1, FOUR_W // 128))
        lo = pl.ds(pl.multiple_of(j * FFT_S, FFT_S), FFT_S)
        h_ref[lo, :, 0:FOUR_W] = (gr * twc + gi * tws).reshape(FFT_S, FFT_R, FOUR_W)
        h_ref[lo, :, FOUR_W:] = (gi * twc - gr * tws).reshape(FFT_S, FFT_R, FOUR_W)

    @pl.when(j >= nj)
    def _():
        hi = pl.ds(pl.multiple_of((j - nj) * FFT_S, FFT_S), FFT_S)
        h = h_ref[:, hi, :].reshape(nb, 2 * FOUR_W).astype(BF16)
        p = jnp.dot(mc_ref[...], h, preferred_element_type=F32)
        q = jnp.dot(ms_ref[...], h, preferred_element_type=F32)
        ur = p[:, :FOUR_W] + q[:, FOUR_W:]
        ui = p[:, FOUR_W:] - q[:, :FOUR_W]
        out = _fold_groups(ur, ui, wf_ref, 1.0 / math.sqrt(T_LAT * F_CH))
        o_ref[...] = out.reshape(FFT_R, FFT_S, FOUR_W)


def _fft_consts():
    r, s = FFT_R, FFT_S
    nb = r * s
    ang = 2.0 * np.pi * np.outer(np.arange(r), np.arange(r)) / r
    c, sn = np.cos(ang), np.sin(ang)
    eye = np.eye(s)
    ma = np.concatenate([np.einsum('pb,ts->tpbs', c, eye).reshape(nb, nb),
                         np.einsum('pb,ts->tpbs', -sn, eye).reshape(nb, nb)], axis=0)
    mbc = np.einsum('pa,ts->ptas', c, eye).reshape(nb, nb)
    mbs = np.einsum('pa,ts->ptas', sn, eye).reshape(nb, nb)
    tw = 2.0 * np.pi * np.outer(np.arange(r), np.arange(r)).reshape(-1) / (r * r)
    twc = np.broadcast_to(np.cos(tw)[:, None], (r * r, 128))
    tws = np.broadcast_to(np.sin(tw)[:, None], (r * r, 128))
    bf = lambda a: jnp.asarray(a, F32).astype(BF16)
    return bf(ma), bf(mbc), bf(mbs), jnp.asarray(twc, F32), jnp.asarray(tws, F32)


def _four_lat(f, wfold_bf, layer, consts):
    ma, mbc, mbs, twc, tws = consts
    r, s = FFT_R, FFT_S
    nb = r * s
    nj = r // s
    f4 = f.reshape(N_LAT_B, r, r, FOUR_W)
    a_step = lambda j: jnp.minimum(j, nj - 1)
    b_step = lambda j: jnp.maximum(j - nj, 0)
    const = lambda b, j: (0, 0)
    out = pl.pallas_call(
        _fft_body,
        grid=(N_LAT_B, 2 * nj),
        in_specs=[pl.BlockSpec((None, r, s, FOUR_W), lambda b, j: (b, 0, a_step(j), 0)),
                  pl.BlockSpec((2 * nb, nb), const),
                  pl.BlockSpec((nb, 128), lambda b, j: (a_step(j), 0)),
                  pl.BlockSpec((nb, 128), lambda b, j: (a_step(j), 0)),
                  pl.BlockSpec((nb, nb), const),
                  pl.BlockSpec((nb, nb), const),
                  pl.BlockSpec((None, F_GROUPS, 2 * F_CH, F_CH), lambda b, j: (layer, 0, 0, 0))],
        out_specs=pl.BlockSpec((None, r, s, FOUR_W), lambda b, j: (b, 0, b_step(j), 0)),
        out_shape=jax.ShapeDtypeStruct((N_LAT_B, r, r, FOUR_W), F32),
        scratch_shapes=[pltpu.VMEM((r, r, 2 * FOUR_W), F32)],
        compiler_params=_cparams(("arbitrary", "arbitrary")),
        name="fft_lat",
    )(f4, ma, twc, tws, mbc, mbs, wfold_bf)
    return out.reshape(N_LAT_B * T_LAT, FOUR_W)


def _outproj_body(a_ref, f_ref, x_ref, gt_ref, sh_ref, sc_ref, gpost_ref, gpre_ref, wo_ref, *rest,
                  moe, group0, tiles_per_group):
    if moe:
        wr_ref, x1_ref, hn_ref, gates_ref, sel_ref = rest
    else:
        x1_ref, hn_ref = rest
    grp = _group_of(pl.program_id(0), group0, tiles_per_group)
    gt, sh, sc = gt_ref[pl.ds(grp, 1), :], sh_ref[pl.ds(grp, 1), :], sc_ref[pl.ds(grp, 1), :]
    for half in range(TM_OUT // TM):
        rows = slice(half * TM, (half + 1) * TM)
        mix_in = jnp.concatenate([a_ref[rows, :], f_ref[rows, :].astype(BF16)], axis=1)
        mixed = jnp.dot(mix_in, wo_ref[...], preferred_element_type=F32)
        x1 = x_ref[rows, :] + gt * _rms(mixed, gpost_ref[...])
        x1_ref[rows, :] = x1
        hn = _rms(x1, gpre_ref[...]) * (1.0 + sc) + sh
        hn_ref[rows, :] = hn.astype(hn_ref.dtype)
        if moe:
            gates, sel = _route_top2(hn, wr_ref[...])
            gates_ref[rows, :] = gates
            sel_ref[rows, :] = sel.astype(sel_ref.dtype)


def _route_top2(hn, wr):
    wr_hi = wr.astype(BF16)
    wr_lo = (wr - wr_hi.astype(F32)).astype(BF16)
    hn_hi = hn.astype(BF16)
    hn_lo = (hn - hn_hi.astype(F32)).astype(BF16)
    hh = jnp.dot(hn_hi, jnp.concatenate([wr_hi, wr_lo], axis=1), preferred_element_type=F32)
    logits = hh[:, :E_PAD] + hh[:, E_PAD:] + jnp.dot(hn_lo, wr_hi, preferred_element_type=F32)
    lane = lax.broadcasted_iota(jnp.int32, logits.shape, 1).astype(F32)
    neg = jnp.float32(-jnp.inf)
    logits = jnp.where(lane < float(N_EXPERTS), logits, neg)
    m1 = jnp.max(logits, axis=-1, keepdims=True)
    i1 = jnp.min(jnp.where(logits == m1, lane, float(E_PAD)), axis=-1, keepdims=True)
    rest_l = jnp.where(lane == i1, neg, logits)
    m2 = jnp.max(rest_l, axis=-1, keepdims=True)
    i2 = jnp.min(jnp.where(rest_l == m2, lane, float(E_PAD)), axis=-1, keepdims=True)
    e2 = jnp.exp(m2 - m1)
    w1 = 1.0 / (1.0 + e2)
    w2 = e2 / (1.0 + e2)
    gates = jnp.where(lane == i1, w1, 0.0) + jnp.where(lane == i2, w2, 0.0)
    sel = jnp.where((lane == i1) | (lane == i2), 1.0, 0.0)
    return gates, sel


def _outproj(attn_o, four_o, x, mod, layer, g_post, g_ffn_pre, w_out_bf, w_router_pad, group0, tiles_per_group):
    rows = x.shape[0]
    moe = w_router_pad is not None
    row = lambda i: (i, 0)
    const = lambda i: (0, 0)
    modspec = lambda k: pl.BlockSpec((None, N_GROUPS_PAD, D), lambda i: (layer, 0, k))
    blocks_per_group = None if tiles_per_group is None else tiles_per_group * TM // TM_OUT
    in_specs = [pl.BlockSpec((TM_OUT, ATTN_W), row), pl.BlockSpec((TM_OUT, FOUR_W), row),
                pl.BlockSpec((TM_OUT, D), row),
                modspec(2), modspec(3), modspec(4),
                pl.BlockSpec((1, D), const), pl.BlockSpec((1, D), const), pl.BlockSpec((D, D), const)]
    args = [attn_o, four_o, x, mod, mod, mod, g_post.reshape(1, D), g_ffn_pre.reshape(1, D), w_out_bf]
    out_specs = [pl.BlockSpec((TM_OUT, D), row), pl.BlockSpec((TM_OUT, D), row)]
    out_shape = [jax.ShapeDtypeStruct((rows, D), F32), jax.ShapeDtypeStruct((rows, D), BF16)]
    if moe:
        in_specs.append(pl.BlockSpec((D, E_PAD), const))
        args.append(w_router_pad)
        out_specs += [pl.BlockSpec((TM_OUT, E_PAD), row)] * 2
        out_shape += [jax.ShapeDtypeStruct((rows, E_PAD), F32), jax.ShapeDtypeStruct((rows, E_PAD), BF16)]
    return pl.pallas_call(
        functools.partial(_outproj_body, moe=moe, group0=group0, tiles_per_group=blocks_per_group),
        grid=(rows // TM_OUT,),
        in_specs=in_specs, out_specs=out_specs, out_shape=out_shape,
        compiler_params=_cparams(("arbitrary",)),
        name="outproj_moe" if moe else "outproj",
    )(*args)


def _swiglu(hn, wg, wu, wd):
    g = jnp.dot(hn, wg, preferred_element_type=F32)
    u = jnp.dot(hn, wu, preferred_element_type=F32)
    return jnp.dot((_silu(g) * u).astype(BF16), wd, preferred_element_type=F32)


def _ffn_dense_body(hn_ref, x1_ref, gt_ref, gpost_ref, wg_ref, wu_ref, wd_ref, o_ref, *, group0, tiles_per_group):
    grp = _group_of(pl.program_id(0), group0, tiles_per_group)
    ff = _swiglu(hn_ref[...], wg_ref[...], wu_ref[...], wd_ref[...])
    o_ref[...] = x1_ref[...] + gt_ref[pl.ds(grp, 1), :] * _rms(ff, gpost_ref[...])


def _ffn_dense(hn, x1, mod, layer, g_post, wg, wu, wd, group0, tiles_per_group):
    rows = x1.shape[0]
    row = lambda i: (i, 0)
    const = lambda i: (0, 0)
    resident = lambda shape: pl.BlockSpec(shape, const, pipeline_mode=pl.Buffered(1))
    return pl.pallas_call(
        functools.partial(_ffn_dense_body, group0=group0, tiles_per_group=tiles_per_group),
        grid=(rows // TM,),
        in_specs=[pl.BlockSpec((TM, D), row), pl.BlockSpec((TM, D), row),
                  pl.BlockSpec((None, N_GROUPS_PAD, D), lambda i: (layer, 0, 5)),
                  pl.BlockSpec((1, D), const),
                  resident((D, D_FF)), resident((D, D_FF)), resident((D_FF, D))],
        out_specs=pl.BlockSpec((TM, D), row),
        out_shape=jax.ShapeDtypeStruct((rows, D), F32),
        compiler_params=_cparams(("arbitrary",)),
        name="ffn_dense",
    )(hn, x1, mod, g_post.reshape(1, D), wg, wu, wd)


def _experts_body(te_ref, tn_ref, tnext_ref, xs_ref, wg_hbm, wu_hbm, wd_hbm, ys_ref, w32, wg_bf, wu_bf, wd_bf, sem):
    i = pl.program_id(0)
    e = te_ref[i]
    fresh = (i == 0) | (e != te_ref[jnp.maximum(i - 1, 0)])

    def fetch(expert, act):
        for j, hbm in enumerate((wg_hbm, wu_hbm, wd_hbm)):
            act(pltpu.make_async_copy(hbm.at[expert], w32.at[j], sem))

    @pl.when(i == 0)
    def _():
        fetch(e, lambda c: c.start())

    @pl.when(fresh)
    def _():
        fetch(e, lambda c: c.wait())
        wg_bf[...] = w32[0].astype(BF16)
        wu_bf[...] = w32[1].astype(BF16)
        wd_bf[...] = w32[2].astype(BF16)
        nxt = tnext_ref[i]

        @pl.when(nxt != e)
        def _():
            fetch(nxt, lambda c: c.start())

    n_real = tn_ref[i]

    @pl.when(n_real > 0)
    def _():
        ys_ref[...] = _swiglu(xs_ref[...].astype(BF16), wg_bf[...], wu_bf[...], wd_bf[...])

    @pl.when(n_real == 0)
    def _():
        ys_ref[...] = jnp.zeros_like(ys_ref)


def _experts(xs, tile_e, tile_n, tile_next, eg, eu, ed):
    assert D == D_FF_E
    tile = lambda i, te, tn, tx: (i, 0)
    return pl.pallas_call(
        _experts_body,
        grid_spec=pltpu.PrefetchScalarGridSpec(
            num_scalar_prefetch=3,
            grid=(N_TILES,),
            in_specs=[pl.BlockSpec((TMOE, D), tile)] + [pl.BlockSpec(memory_space=pl.ANY)] * 3,
            out_specs=pl.BlockSpec((TMOE, D), tile),
            scratch_shapes=[pltpu.VMEM((3, D, D), F32),
                            pltpu.VMEM((D, D_FF_E), BF16), pltpu.VMEM((D, D_FF_E), BF16),
                            pltpu.VMEM((D_FF_E, D), BF16), pltpu.SemaphoreType.DMA]),
        out_shape=jax.ShapeDtypeStruct((NP, D), F32),
        compiler_params=_cparams(("arbitrary",)),
        name="moe_experts",
    )(tile_e, tile_n, tile_next, xs, eg, eu, ed)


def _route_body(*refs, firsts):
    ns = len(firsts)
    gates_refs, sel_refs = refs[:ns], refs[ns:2 * ns]
    ltri_ref, w_ref, rho_ref, tmeta_ref, meta_ref, cnt_ref, off_ref, run_ref = refs[2 * ns:]
    phase, t = pl.program_id(0), pl.program_id(1)

    def this_stream(stream_refs):
        val = stream_refs[0][...]
        for k in range(1, ns):
            val = jnp.where(t >= firsts[k], stream_refs[k][...], val)
        return val

    sel = this_stream(sel_refs)
    group_rows = jnp.ceil(jnp.sum(sel.astype(F32), axis=0, keepdims=True) * (1.0 / GROUP)) * GROUP

    @pl.when((phase == 0) & (t == 0))
    def _():
        cnt_ref[...] = jnp.zeros_like(cnt_ref)

    @pl.when(phase == 0)
    def _():
        cnt_ref[...] += group_rows

    @pl.when((phase == 1) & (t == 0))
    def _():
        off_ref[...] = _segment_meta(cnt_ref[...], meta_ref)
        run_ref[...] = jnp.zeros_like(run_ref)

    @pl.when(phase == 1)
    def _():
        lane1 = lax.broadcasted_iota(jnp.int32, group_rows.shape, 1).astype(F32)
        buf0 = jnp.zeros_like(group_rows)
        running = jnp.zeros((1, 1), F32)
        for e in range(N_EXPERTS):
            pick = lane1 == float(e)
            buf0 = jnp.where(pick, running, buf0)
            running = running + jnp.sum(jnp.where(pick, group_rows, 0.0), axis=-1, keepdims=True)
        slot0 = off_ref[...] + run_ref[...]
        run_ref[...] += group_rows
        row8 = lax.broadcasted_iota(jnp.int32, tmeta_ref.shape, 0)
        tmeta_ref[...] = jnp.where(row8 == 0, slot0, jnp.where(row8 == 1, group_rows,
                                   jnp.where(row8 == 2, buf0, 0.0)))

        earlier = jnp.dot(ltri_ref[...], sel, preferred_element_type=F32)
        rho = earlier + buf0
        chosen = sel > 0
        r_lo = jnp.min(jnp.where(chosen, rho, float(R_TILE)), axis=-1, keepdims=True)
        r_hi = jnp.max(jnp.where(chosen, rho, -1.0), axis=-1, keepdims=True)
        g = this_stream(gates_refs)
        w_lo = jnp.sum(jnp.where(chosen & (rho == r_lo), g, 0.0), axis=-1, keepdims=True)
        w_hi = jnp.sum(jnp.where(chosen & (rho == r_hi), g, 0.0), axis=-1, keepdims=True)
        lane = lax.broadcasted_iota(jnp.int32, rho.shape, 1)
        w_ref[...] = jnp.where(lane == 2, w_lo, jnp.where(lane == 3, w_hi,
                               jnp.where(lane == 4, r_lo, jnp.where(lane == 5, r_hi, 0.0))))
        cols = jnp.where(lane == 0, r_lo, jnp.where(lane == 1, r_hi, 0.0))
        rho_ref[...] = jnp.transpose(cols)[0:N_GROUPS_PAD, :]


def _segment_meta(cnt, meta_ref):
    padded = jnp.ceil(cnt * (1.0 / TMOE)) * TMOE
    lane = lax.broadcasted_iota(jnp.int32, cnt.shape, 1).astype(F32)
    pad_of = lambda e: jnp.sum(jnp.where(lane == float(e), padded, 0.0), axis=-1, keepdims=True)
    following = [None] * N_EXPERTS
    seen = jnp.full((1, 1), -1.0, F32)
    for e in reversed(range(N_EXPERTS)):
        following[e] = seen
        seen = jnp.where(pad_of(e) > 0.0, float(e), seen)
    start = lane * TMOE
    off = jnp.zeros_like(cnt)
    tile_e = jnp.zeros_like(cnt)
    tile_n = jnp.zeros_like(cnt)
    tile_next = jnp.zeros_like(cnt)
    running = jnp.zeros((1, 1), F32)
    last_e = jnp.zeros((1, 1), F32)
    for e in range(N_EXPERTS):
        pick = lane == float(e)
        cnt_e = jnp.sum(jnp.where(pick, cnt, 0.0), axis=-1, keepdims=True)
        pad_e = pad_of(e)
        off = jnp.where(pick, running, off)
        in_seg = (start >= running) & (start < running + pad_e)
        tile_e = jnp.where(in_seg, float(e), tile_e)
        tile_n = jnp.where(in_seg, jnp.clip(cnt_e - (start - running), 0.0, float(TMOE)), tile_n)
        tile_next = jnp.where(in_seg, jnp.where(following[e] >= 0.0, following[e], float(e)), tile_next)
        last_e = jnp.where(pad_e > 0.0, float(e), last_e)
        running = running + pad_e
    tile_e = jnp.where(start >= running, last_e, tile_e)
    tile_next = jnp.where(start >= running, last_e, tile_next)
    row = lax.broadcasted_iota(jnp.int32, meta_ref.shape, 0)
    last_tile = jnp.where(lane < float(N_EXPERTS), off + padded - TMOE, running)
    last_tile = jnp.where((lane < float(N_EXPERTS)) & (padded == 0.0), -1.0, last_tile)
    meta_ref[...] = jnp.where(row == 0, tile_e, jnp.where(row == 1, tile_n,
                              jnp.where(row == 2, last_tile, jnp.where(row == 3, tile_next, 0.0))))
    return off


def _route(gates, sels):
    tiles = [g.shape[0] // TM for g in gates]
    firsts = [int(f) for f in np.cumsum([0] + tiles)]
    n, rows = firsts[-1], firsts[-1] * TM
    ltri = jnp.asarray(np.tril(np.ones((TM, TM)), -1), F32).astype(BF16)
    stream_spec = lambda lo, cnt: pl.BlockSpec((TM, E_PAD), lambda p, t: (jnp.clip(t - lo, 0, cnt - 1), 0))
    stream_specs = [stream_spec(lo, cnt) for lo, cnt in zip(firsts[:-1], tiles)]
    per_tile = lambda p, t: (t * p, 0, 0)
    return pl.pallas_call(
        functools.partial(_route_body, firsts=tuple(firsts[:-1])),
        grid=(2, n),
        in_specs=stream_specs + stream_specs + [pl.BlockSpec((TM, TM), lambda p, t: (0, 0))],
        out_specs=[pl.BlockSpec((TM, E_PAD), lambda p, t: (t * p, 0)),
                   pl.BlockSpec((None, N_GROUPS_PAD, TM), per_tile),
                   pl.BlockSpec((None, N_GROUPS_PAD, E_PAD), per_tile),
                   pl.BlockSpec((N_GROUPS_PAD, E_PAD), lambda p, t: (0, 0))],
        out_shape=[jax.ShapeDtypeStruct((rows, E_PAD), F32),
                   jax.ShapeDtypeStruct((n, N_GROUPS_PAD, TM), F32),
                   jax.ShapeDtypeStruct((n, N_GROUPS_PAD, E_PAD), F32),
                   jax.ShapeDtypeStruct((N_GROUPS_PAD, E_PAD), F32)],
        scratch_shapes=[pltpu.VMEM((1, E_PAD), F32)] * 3,
        compiler_params=_cparams(("arbitrary", "arbitrary")),
        name="moe_route",
    )(*gates, *sels, ltri)


def _group_copies(tmeta_ref, slot_window, buf_window, sem, to_slots, act):
    for e in range(N_EXPERTS):
        slot0, n, buf0 = tmeta_ref[0, e], tmeta_ref[1, e], tmeta_ref[2, e]
        done = jnp.int32(0)
        size = TM
        while size >= GROUP:
            take = (n - done) >= size
            s_at = pl.ds(pl.multiple_of(slot0 + done, GROUP), size)
            b_at = pl.ds(pl.multiple_of(buf0 + done, GROUP), size)
            src, dst = (buf_window(b_at), slot_window(s_at)) if to_slots else (slot_window(s_at), buf_window(b_at))

            @pl.when(take)
            def _(src=src, dst=dst):
                act(pltpu.make_async_copy(src, dst, sem))

            done = done + jnp.where(take, size, 0)
            size //= 2


def _dispatch_body(tmeta_ref, prev_tmeta_ref, pad_ref, rho_ref, *rest, firsts):
    ns = len(firsts)
    hn_refs, (xs_ref, pbuf, zeros, sems) = rest[:ns], rest[ns:]
    i = pl.program_id(0)
    cur = lax.rem(i, 2)
    sem = sems.at[0]

    @pl.when(i == 0)
    def _():
        zeros[...] = jnp.zeros_like(zeros)
        fills = []
        for e in range(N_EXPERTS):
            s = pad_ref[0, e]
            fills.append((s >= 0, pl.multiple_of(jnp.maximum(s, 0), TMOE)))
        for j in range(N_TILES - TOP_K * N_TOKENS // TMOE):
            s = NP - (j + 1) * TMOE
            fills.append((s >= pad_ref[0, N_EXPERTS], s))
        fill = lambda s: pltpu.make_async_copy(zeros, xs_ref.at[pl.ds(s, TMOE), :], sem)
        for live, s in fills:
            @pl.when(live)
            def _(s=s):
                fill(s).start()
        for live, s in fills:
            @pl.when(live)
            def _(s=s):
                fill(s).wait()

    hn = hn_refs[0][...]
    for k in range(1, ns):
        hn = jnp.where(i >= firsts[k], hn_refs[k][...], hn)
    row = lax.broadcasted_iota(jnp.int32, (R_TILE, TM), 0).astype(F32)
    onehot = jnp.where((row == rho_ref[0:1, :]) | (row == rho_ref[1:2, :]), 1.0, 0.0).astype(BF16)
    pbuf[cur] = jnp.dot(onehot, hn, preferred_element_type=F32)

    def store(meta_ref, half, act):
        _group_copies(meta_ref, lambda at: xs_ref.at[at, :], lambda at: pbuf.at[half, at, :], sems.at[half],
                      True, act)

    store(tmeta_ref, cur, lambda c: c.start())

    @pl.when(i > 0)
    def _():
        store(prev_tmeta_ref, 1 - cur, lambda c: c.wait())

    @pl.when(i + 1 == pl.num_programs(0))
    def _():
        store(tmeta_ref, cur, lambda c: c.wait())


def _dispatch(hns, tmeta, rho, pad_starts):
    tiles = [h.shape[0] // TM for h in hns]
    firsts = [int(f) for f in np.cumsum([0] + tiles)]
    n = firsts[-1]
    stream_spec = lambda lo, cnt: pl.BlockSpec((TM, D), lambda i: (jnp.clip(i - lo, 0, cnt - 1), 0))
    return pl.pallas_call(
        functools.partial(_dispatch_body, firsts=tuple(firsts[:-1])),
        grid=(n,),
        in_specs=[pl.BlockSpec((None, N_GROUPS_PAD, E_PAD), lambda i: (i, 0, 0), memory_space=pltpu.SMEM),
                  pl.BlockSpec((None, N_GROUPS_PAD, E_PAD), lambda i: (jnp.maximum(i - 1, 0), 0, 0),
                               memory_space=pltpu.SMEM),
                  pl.BlockSpec(memory_space=pltpu.SMEM),
                  pl.BlockSpec((None, N_GROUPS_PAD, TM), lambda i: (i, 0, 0))]
                 + [stream_spec(lo, cnt) for lo, cnt in zip(firsts[:-1], tiles)],
        out_specs=pl.BlockSpec(memory_space=pl.ANY),
        out_shape=jax.ShapeDtypeStruct((NP, D), F32),
        scratch_shapes=[pltpu.VMEM((2, R_TILE, D), F32), pltpu.VMEM((TMOE, D), F32),
                        pltpu.SemaphoreType.DMA((2,))],
        compiler_params=_cparams(("arbitrary",)),
        name="moe_dispatch",
    )(tmeta, tmeta, pad_starts, rho, *hns)


def _combine_body(tmeta_ref, next_tmeta_ref, w_ref, x1_ref, gt_ref, gpost_ref, ys_ref, o_ref, ybuf, sems,
                   *, group0, tiles_per_group):
    i = pl.program_id(0)
    grp = _group_of(i, group0, tiles_per_group)
    cur = lax.rem(i, 2)

    def fetch(meta_ref, half, act):
        _group_copies(meta_ref, lambda at: ys_ref.at[at, :], lambda at: ybuf.at[half, at, :], sems.at[half],
                      False, act)

    @pl.when(i == 0)
    def _():
        ybuf[...] = jnp.zeros_like(ybuf)
        fetch(tmeta_ref, 0, lambda c: c.start())

    @pl.when(i + 1 < pl.num_programs(0))
    def _():
        fetch(next_tmeta_ref, 1 - cur, lambda c: c.start())

    fetch(tmeta_ref, cur, lambda c: c.wait())
    w = w_ref[...]
    lane = lax.broadcasted_iota(jnp.int32, w.shape, 1)
    col = lambda j: jnp.sum(jnp.where(lane == j, w, 0.0), axis=-1, keepdims=True)
    w_lo, w_hi, r_lo, r_hi = col(2), col(3), col(4), col(5)
    y = ybuf[cur].astype(BF16)
    pos = lax.broadcasted_iota(jnp.int32, (TM, R_TILE), 1).astype(F32)
    pick = lambda r: jnp.dot(jnp.where(pos == r, 1.0, 0.0).astype(BF16), y, preferred_element_type=F32)
    ff = w_lo * pick(r_lo) + w_hi * pick(r_hi)
    o_ref[...] = x1_ref[...] + gt_ref[pl.ds(grp, 1), :] * _rms(ff, gpost_ref[...])


def _combine(ys, tmeta, slot_w, tile0, x1, mod, layer, g_post, group0, tiles_per_group):
    rows = x1.shape[0]
    n = rows // TM
    row = lambda i: (i, 0)
    return pl.pallas_call(
        functools.partial(_combine_body, group0=group0, tiles_per_group=tiles_per_group),
        grid=(n,),
        in_specs=[pl.BlockSpec((None, N_GROUPS_PAD, E_PAD), lambda i: (tile0 + i, 0, 0), memory_space=pltpu.SMEM),
                  pl.BlockSpec((None, N_GROUPS_PAD, E_PAD), lambda i: (tile0 + jnp.minimum(i + 1, n - 1), 0, 0),
                               memory_space=pltpu.SMEM),
                  pl.BlockSpec((TM, E_PAD), lambda i: (tile0 + i, 0)), pl.BlockSpec((TM, D), row),
                  pl.BlockSpec((None, N_GROUPS_PAD, D), lambda i: (layer, 0, 5)),
                  pl.BlockSpec((1, D), lambda i: (0, 0)),
                  pl.BlockSpec(memory_space=pl.ANY)],
        out_specs=pl.BlockSpec((TM, D), row),
        out_shape=jax.ShapeDtypeStruct((rows, D), F32),
        scratch_shapes=[pltpu.VMEM((2, R_TILE, D), F32), pltpu.SemaphoreType.DMA((2,))],
        compiler_params=_cparams(("arbitrary",)),
        name="moe_combine",
    )(tmeta, tmeta, slot_w, x1, mod, g_post.reshape(1, D), ys)


def kernel(x_prompt, x_sample, cache_k, cache_v, c, c_ctx, g_attn_pre, g_attn_post, g_ffn_pre, g_ffn_post,
           w_ada, b_ada, w_in, lam_params, g_subln, w_fnet, w_out, w_gate, w_up, w_down, w_router,
           e_gate, e_up, e_down):
    assert x_prompt.shape == (N_CTX_B, T_CTX, D) and x_sample.shape == (N_LAT_B, T_LAT, D)
    assert cache_k.shape == (N_LAT_B, DEPTH, T_PAST, N_HEADS, 2, QK_DIM)

    cond = jnp.zeros((N_GROUPS_PAD, D), F32).at[0].set(c_ctx).at[1:1 + N_LAT_B].set(c)
    mod = _ada(cond, w_ada, b_ada)
    wfold_bf = _wfold(w_fnet).astype(BF16)
    rope_tabs = _rope_tables()
    fft_consts = _fft_consts()

    w_in_bf = w_in.astype(BF16)
    w_out_bf = w_out.astype(BF16)
    w_router_pad = jnp.zeros((DEPTH // 2, D, E_PAD), F32).at[:, :, :N_EXPERTS].set(w_router)
    ck_bf = cache_k.reshape(N_LAT_B, DEPTH, T_PAST, ATTN_W).astype(BF16)
    cv_bf = cache_v.reshape(N_LAT_B, DEPTH, T_PAST, ATTN_W).astype(BF16)

    lat_tiles = T_LAT // TM
    streams = [dict(x=x_prompt.reshape(N_CTX_B * T_CTX, D), group0=0, tiles=None, lat=False),
               dict(x=x_sample.reshape(N_LAT_B * T_LAT, D), group0=1, tiles=lat_tiles, lat=True)]
    new_k = new_v = None
    for l in range(DEPTH):
        lam_init = 0.8 - 0.6 * math.exp(-0.3 * l)
        i = l // 2
        routed = []
        for s in streams:
            g0, tiles = s["group0"], s["tiles"]
            if s["lat"]:
                qkv, f = _inproj(s["x"], mod, l, g_attn_pre[l], w_in_bf[l], rope_tabs, g0, tiles)
                attn_o = _attn_lat(qkv, ck_bf[:, l], cv_bf[:, l], lam_params, g_subln, l, lam_init)
                attn_o = attn_o.reshape(N_LAT_B * T_LAT, ATTN_W)
                four_o = _four_lat(f, wfold_bf, l, fft_consts)
            else:
                qkv, f, new_k, new_v = _inproj(s["x"], mod, l, g_attn_pre[l], w_in_bf[l], None, g0, tiles,
                                               cache=None if l == 0 else (new_k, new_v))
                attn_o = _attn_ctx(qkv, lam_params, g_subln, l, lam_init).reshape(N_CTX_B * T_CTX, ATTN_W)
                four_o = _four_ctx(f, wfold_bf, l).reshape(N_CTX_B * T_CTX, FOUR_W)
            if l % 2 == 0:
                x1, hn = _outproj(attn_o, four_o, s["x"], mod, l, g_attn_post[l], g_ffn_pre[l], w_out_bf[l],
                                  None, g0, tiles)
                s["x"] = _ffn_dense(hn, x1, mod, l, g_ffn_post[l], w_gate[i].astype(BF16), w_up[i].astype(BF16),
                                    w_down[i].astype(BF16), g0, tiles)
            else:
                routed.append(_outproj(attn_o, four_o, s["x"], mod, l, g_attn_post[l], g_ffn_pre[l],
                                       w_out_bf[l], w_router_pad[i], g0, tiles))
        if l % 2 == 1:
            slot_w, rho, tmeta, meta = _route([r[2] for r in routed], [r[3] for r in routed])
            tmeta = tmeta.astype(jnp.int32)
            tile_e = meta[0, :N_TILES].astype(jnp.int32)
            tile_n = meta[1, :N_TILES].astype(jnp.int32)
            tile_next = meta[3, :N_TILES].astype(jnp.int32)
            pad_starts = meta[2:3, :N_EXPERTS + 1].astype(jnp.int32)
            bounds = np.cumsum([0] + [r[0].shape[0] for r in routed])
            xs = _dispatch([r[1] for r in routed], tmeta, rho, pad_starts)
            ys = _experts(xs, tile_e, tile_n, tile_next, e_gate[i], e_up[i], e_down[i])
            for s, r, lo in zip(streams, routed, bounds[:-1]):
                s["x"] = _combine(ys, tmeta, slot_w, int(lo) // TM, r[0], mod, l, g_ffn_post[l],
                                   s["group0"], s["tiles"])

    y_prompt = streams[0]["x"].reshape(N_CTX_B, T_CTX, D)
    y_sample = streams[1]["x"].reshape(N_LAT_B, T_LAT, D)
    new_cache_k = new_k.reshape(N_CTX_B, DEPTH, T_CTX, N_HEADS, 2, QK_DIM)
    new_cache_v = new_v.reshape(N_CTX_B, DEPTH, T_CTX, N_HEADS, HEAD_W)
    return (y_prompt, y_sample, new_cache_k, new_cache_v)
```

```python
import functools
import math

import numpy as np
import jax
import jax.numpy as jnp
from jax import lax
from jax.experimental import pallas as pl
from jax.experimental.pallas import tpu as pltpu

F32 = jnp.float32
BF16 = jnp.bfloat16

D = 1024
N_CTX_B = 16
T_CTX = 256
N_LAT_B = 2
T_LAT = 4096
T_PAST = 512
DEPTH = 2
GRID_W = 64
N_HEADS = 4
QK_DIM = 64
HEAD_W = 2 * QK_DIM
ATTN_W = N_HEADS * HEAD_W
F_GROUPS = 4
F_CH = 128
FOUR_W = F_GROUPS * F_CH
IN_W = 3 * ATTN_W + FOUR_W
QKV_W = 3 * ATTN_W
D_FF = 2816
N_EXPERTS = 8
D_FF_E = 1024
N_MOD = 6
EPS = 1e-6
ROPE_THETA = 10000.0
N_GROUPS_PAD = 8
E_PAD = 128

TM = 512
CTX_PER_STEP = 4
TM_OUT = 1024
TOP_K = 2
TMOE = 512
N_TOKENS = N_CTX_B * T_CTX + N_LAT_B * T_LAT
GROUP = 8
R_TILE = TOP_K * TM + N_EXPERTS * GROUP
N_TILES = -(-(TOP_K * N_TOKENS + (N_TOKENS // TM) * N_EXPERTS * (GROUP - 1)) // TMOE) + N_EXPERTS
NP = N_TILES * TMOE
TQ = 512
Q_SUB = 2
TK = 1536
ATTN_UNROLL = 3
FFT_R = 64
FFT_S = 8
VMEM_LIMIT = 56 * 1024 * 1024


def _cparams(sem):
    return pltpu.CompilerParams(dimension_semantics=sem, vmem_limit_bytes=VMEM_LIMIT)


def _rms(x, g):
    ms = jnp.mean(x * x, axis=-1, keepdims=True)
    return x * lax.rsqrt(ms + EPS) * g


def _silu(x):
    return x * (1.0 / (1.0 + jnp.exp(-x)))


def _ada_body(c_ref, w_ref, b_ref, o_ref):
    s = _silu(c_ref[...])
    s_hi = s.astype(BF16)
    s_lo = (s - s_hi.astype(F32)).astype(BF16)
    both = jnp.dot(jnp.concatenate([s_hi, s_lo], axis=0), w_ref[...].astype(BF16), preferred_element_type=F32)
    o_ref[...] = both[:N_GROUPS_PAD] + both[N_GROUPS_PAD:] + b_ref[...]


def _ada(cond, w_ada, b_ada):
    tn = 1536
    return pl.pallas_call(
        _ada_body,
        grid=(DEPTH, N_MOD * D // tn),
        in_specs=[pl.BlockSpec((N_GROUPS_PAD, D), lambda l, j: (0, 0)),
                  pl.BlockSpec((None, D, tn), lambda l, j: (l, 0, j)),
                  pl.BlockSpec((None, 1, tn), lambda l, j: (l, 0, j))],
        out_specs=pl.BlockSpec((None, N_GROUPS_PAD, tn), lambda l, j: (l, 0, j)),
        out_shape=jax.ShapeDtypeStruct((DEPTH, N_GROUPS_PAD, N_MOD * D), F32),
        compiler_params=_cparams(("arbitrary", "arbitrary")),
        name="ada_mod",
    )(cond, w_ada, b_ada.reshape(DEPTH, 1, N_MOD * D))


def _wfold_body(cc_ref, sc_ref, w_ref, o_ref):
    w = w_ref[...]
    o_ref[0:F_CH, :] = jnp.dot(cc_ref[...], w, preferred_element_type=F32,
                               precision=lax.Precision.HIGHEST)
    o_ref[F_CH:2 * F_CH, :] = jnp.dot(sc_ref[...], w, preferred_element_type=F32,
                                      precision=lax.Precision.HIGHEST)


def _wfold(w_fnet):
    ang = 2.0 * np.pi * np.outer(np.arange(F_CH), np.arange(F_CH)) / F_CH
    cc = jnp.asarray(np.cos(ang), F32)
    sc = jnp.asarray(np.sin(ang), F32)
    cspec = pl.BlockSpec((F_CH, F_CH), lambda l, g: (0, 0))
    return pl.pallas_call(
        _wfold_body,
        grid=(DEPTH, F_GROUPS),
        in_specs=[cspec, cspec, pl.BlockSpec((None, None, F_CH, F_CH), lambda l, g: (l, g, 0, 0))],
        out_specs=pl.BlockSpec((None, None, 2 * F_CH, F_CH), lambda l, g: (l, g, 0, 0)),
        out_shape=jax.ShapeDtypeStruct((DEPTH, F_GROUPS, 2 * F_CH, F_CH), F32),
        compiler_params=_cparams(("arbitrary", "arbitrary")),
        name="fnet_fold",
    )(cc, sc, w_fnet)


def _group_of(i, group0, tiles_per_group):
    if tiles_per_group is None:
        return group0
    return group0 + lax.div(i, jnp.int32(tiles_per_group))


def _inproj_body(x_ref, sh_ref, sc_ref, g_ref, w_ref, *rest, rope, cache_layer, group0, tiles_per_group):
    if rope:
        cos_ref, sin_ref, qkv_ref, f_ref = rest
    else:
        qkv_ref, f_ref, k32_ref, v32_ref = rest[-4:]
    grp = _group_of(pl.program_id(0), group0, tiles_per_group)
    sh = sh_ref[pl.ds(grp, 1), :]
    sc = sc_ref[pl.ds(grp, 1), :]
    hn = _rms(x_ref[...], g_ref[...]) * (1.0 + sc) + sh
    proj = jnp.dot(hn.astype(BF16), w_ref[...], preferred_element_type=F32)
    qk = proj[:, :2 * ATTN_W]
    v = proj[:, 2 * ATTN_W:QKV_W]
    if rope:
        cos = jnp.tile(cos_ref[...], (1, 2 * ATTN_W // HEAD_W))
        sin = jnp.tile(sin_ref[...], (1, 2 * ATTN_W // HEAD_W))
        lane = lax.broadcasted_iota(jnp.int32, qk.shape, 1)
        low = (lane % 32) < 16
        rot = jnp.where(low, pltpu.roll(qk, 2 * ATTN_W - 16, 1), pltpu.roll(qk, 16, 1))
        qk = qk * cos + rot * sin
    else:
        for ref, val in ((k32_ref, qk[:, ATTN_W:]), (v32_ref, v)):
            val = val.reshape(TM // T_CTX, T_CTX, ATTN_W)
            if cache_layer == 0:
                ref[:, 0] = val
                ref[:, 1:] = jnp.zeros((TM // T_CTX, DEPTH - 1, T_CTX, ATTN_W), F32)
            else:
                ref[...] = val
    qkv_ref[:, 0:ATTN_W] = (qk[:, :ATTN_W] * (QK_DIM ** -0.5 * math.log2(math.e))).astype(BF16)
    qkv_ref[:, ATTN_W:2 * ATTN_W] = qk[:, ATTN_W:].astype(BF16)
    qkv_ref[:, 2 * ATTN_W:] = v.astype(BF16)
    f_ref[...] = proj[:, QKV_W:]


def _inproj(x, mod, layer, g_pre, w_in_bf, rope_tabs, group0, tiles_per_group, cache=None):
    rows = x.shape[0]
    rope = rope_tabs is not None
    aliases = {}
    row = lambda i: (i, 0)
    const = lambda i: (0, 0)
    in_specs = [pl.BlockSpec((TM, D), row),
                pl.BlockSpec((None, N_GROUPS_PAD, D), lambda i: (layer, 0, 0)),
                pl.BlockSpec((None, N_GROUPS_PAD, D), lambda i: (layer, 0, 1)),
                pl.BlockSpec((1, D), const),
                pl.BlockSpec((D, IN_W), const)]
    args = [x, mod, mod, g_pre.reshape(1, D), w_in_bf]
    out_specs = [pl.BlockSpec((TM, QKV_W), row), pl.BlockSpec((TM, FOUR_W), row)]
    out_shape = [jax.ShapeDtypeStruct((rows, QKV_W), BF16), jax.ShapeDtypeStruct((rows, FOUR_W), F32)]
    if rope:
        tiles_per_seq = T_LAT // TM
        tab = pl.BlockSpec((TM, HEAD_W), lambda i: (i % tiles_per_seq, 0))
        in_specs += [tab, tab]
        args += list(rope_tabs)
    else:
        nb = TM // T_CTX
        if cache is None:
            cache_spec = pl.BlockSpec((nb, DEPTH, T_CTX, ATTN_W), lambda i: (i, 0, 0, 0))
        else:
            cache_spec = pl.BlockSpec((nb, None, T_CTX, ATTN_W), lambda i: (i, layer, 0, 0))
            aliases = {len(args): 2, len(args) + 1: 3}
            in_specs += [pl.BlockSpec(memory_space=pl.ANY)] * 2
            args += list(cache)
        out_specs += [cache_spec] * 2
        out_shape += [jax.ShapeDtypeStruct((N_CTX_B, DEPTH, T_CTX, ATTN_W), F32)] * 2
    return pl.pallas_call(
        functools.partial(_inproj_body, rope=rope, cache_layer=None if rope else (0 if cache is None else layer),
                          group0=group0, tiles_per_group=tiles_per_group),
        grid=(rows // TM,),
        in_specs=in_specs, out_specs=out_specs, out_shape=out_shape,
        input_output_aliases=aliases,
        compiler_params=_cparams(("arbitrary",)),
        name="inproj_lat" if rope else "inproj_ctx",
    )(*args)


def _rope_tables():
    half = QK_DIM // 2
    inv = 1.0 / (ROPE_THETA ** (np.arange(0, half, 2, dtype=np.float64) / half))
    pos = np.arange(T_LAT)
    def tab(p):
        ang = p[:, None].astype(np.float64) * inv[None, :]
        return np.concatenate([ang, ang], axis=-1)
    ang = np.concatenate([tab(pos // GRID_W), tab(pos % GRID_W)], axis=-1)
    sign = np.where((np.arange(QK_DIM) % 32) < 16, -1.0, 1.0)
    cos = np.tile(np.cos(ang), (1, 2))
    sin = np.tile(np.sin(ang) * sign[None, :], (1, 2))
    return jnp.asarray(cos, F32), jnp.asarray(sin, F32)


def _lam(lam_ref, lam_init):
    lp = lam_ref[...]
    return (jnp.exp(jnp.sum(lp[0:1] * lp[1:2], keepdims=True))
            - jnp.exp(jnp.sum(lp[2:3] * lp[3:4], keepdims=True)) + lam_init)


def _stack_maps(q):
    lane = lax.broadcasted_iota(jnp.int32, q.shape, 1)
    zero = jnp.zeros_like(q)
    return jnp.concatenate([jnp.where(lane < QK_DIM, q, zero), jnp.where(lane >= QK_DIM, q, zero)], axis=0)


def _softmax_step(qq, kb, vb, carry):
    m, l, acc = carry
    s = lax.dot_general(qq, kb, (((1,), (1,)), ((), ())), preferred_element_type=F32)
    m_new = jnp.maximum(m, jnp.max(s, axis=-1, keepdims=True))
    alpha = jnp.exp2(m - m_new)
    p = jnp.exp2(s - m_new)
    l = alpha * l + jnp.sum(p, axis=-1, keepdims=True)
    acc = alpha * acc + jnp.dot(p.astype(BF16), vb, preferred_element_type=F32)
    return m_new, l, acc


def _softmax_init(rows):
    return (jnp.full((rows, 1), -jnp.inf, F32), jnp.zeros((rows, 1), F32), jnp.zeros((rows, HEAD_W), F32))


def _diff_out(carry, tq, lam, gs, lam_init):
    _, l, acc = carry
    o = acc / l
    a = o[:tq] - lam * o[tq:]
    return _rms(a, gs) * (1.0 - lam_init)


def _attn_ctx_body(lam_ref, gs_ref, q_ref, k_ref, v_ref, o_ref, *, lam_init):
    lam = _lam(lam_ref, lam_init)
    for b in range(CTX_PER_STEP):
        for h in range(N_HEADS):
            sl = slice(h * HEAD_W, (h + 1) * HEAD_W)
            carry = _softmax_step(_stack_maps(q_ref[b, :, sl]), k_ref[b, :, sl], v_ref[b, :, sl],
                                  _softmax_init(2 * T_CTX))
            o_ref[b, :, sl] = _diff_out(carry, T_CTX, lam, gs_ref[...], lam_init).astype(o_ref.dtype)


def _attn_ctx(qkv, lam_params, g_subln, layer, lam_init):
    qkv3 = qkv.reshape(N_CTX_B, T_CTX, QKV_W)
    blk = lambda part: pl.BlockSpec((CTX_PER_STEP, T_CTX, ATTN_W), lambda b: (b, 0, part))
    return pl.pallas_call(
        functools.partial(_attn_ctx_body, lam_init=lam_init),
        grid=(N_CTX_B // CTX_PER_STEP,),
        in_specs=[pl.BlockSpec((None, 4, QK_DIM), lambda b: (layer, 0, 0)),
                  pl.BlockSpec((None, 1, HEAD_W), lambda b: (layer, 0, 0)),
                  blk(0), blk(1), blk(2)],
        out_specs=pl.BlockSpec((CTX_PER_STEP, T_CTX, ATTN_W), lambda b: (b, 0, 0)),
        out_shape=jax.ShapeDtypeStruct((N_CTX_B, T_CTX, ATTN_W), BF16),
        compiler_params=_cparams(("arbitrary",)),
        name="attn_ctx",
    )(lam_params, g_subln.reshape(DEPTH, 1, HEAD_W), qkv3, qkv3, qkv3)


def _attn_lat_body(lam_ref, gs_ref, q_ref, kc_ref, vc_ref, kl_ref, vl_ref, o_ref, kcat, vcat, *, lam_init):
    @pl.when(pl.program_id(2) == 0)
    def _():
        kcat[0:T_PAST, :] = kc_ref[...]
        kcat[T_PAST:, :] = kl_ref[...]
        vcat[0:T_PAST, :] = vc_ref[...]
        vcat[T_PAST:, :] = vl_ref[...]

    lam = _lam(lam_ref, lam_init)
    for sub in range(Q_SUB):
        rows = slice(sub * TQ, (sub + 1) * TQ)
        qq = _stack_maps(q_ref[rows, :])

        def body(c, carry, qq=qq):
            start = pl.multiple_of(c * TK, TK)
            return _softmax_step(qq, kcat[pl.ds(start, TK), :], vcat[pl.ds(start, TK), :], carry)

        carry = lax.fori_loop(0, (T_PAST + T_LAT) // TK, body, _softmax_init(2 * TQ), unroll=ATTN_UNROLL)
        o_ref[rows, :] = _diff_out(carry, TQ, lam, gs_ref[...], lam_init).astype(o_ref.dtype)


def _attn_lat(qkv, ck, cv, lam_params, g_subln, layer, lam_init):
    qkv3 = qkv.reshape(N_LAT_B, T_LAT, QKV_W)
    full = lambda off: pl.BlockSpec((None, T_LAT, HEAD_W), lambda b, h, i: (b, 0, off + h))
    past = pl.BlockSpec((None, T_PAST, HEAD_W), lambda b, h, i: (b, 0, h))
    return pl.pallas_call(
        functools.partial(_attn_lat_body, lam_init=lam_init),
        grid=(N_LAT_B, N_HEADS, T_LAT // (Q_SUB * TQ)),
        in_specs=[pl.BlockSpec((None, 4, QK_DIM), lambda b, h, i: (layer, 0, 0)),
                  pl.BlockSpec((None, 1, HEAD_W), lambda b, h, i: (layer, 0, 0)),
                  pl.BlockSpec((None, Q_SUB * TQ, HEAD_W), lambda b, h, i: (b, i, h)),
                  past, past, full(N_HEADS), full(2 * N_HEADS)],
        out_specs=pl.BlockSpec((None, Q_SUB * TQ, HEAD_W), lambda b, h, i: (b, i, h)),
        out_shape=jax.ShapeDtypeStruct((N_LAT_B, T_LAT, ATTN_W), BF16),
        scratch_shapes=[pltpu.VMEM((T_PAST + T_LAT, HEAD_W), BF16)] * 2,
        compiler_params=_cparams(("arbitrary", "arbitrary", "arbitrary")),
        name="attn_lat",
    )(lam_params, g_subln.reshape(DEPTH, 1, HEAD_W), qkv3, ck, cv, qkv3, qkv3)


def _fold_groups(ur, ui, wf_ref, scale):
    outs = []
    for g in range(F_GROUPS):
        sl = slice(g * F_CH, (g + 1) * F_CH)
        lhs = jnp.concatenate([ur[:, sl], ui[:, sl]], axis=1).astype(BF16)
        outs.append(jnp.dot(lhs, wf_ref[g], preferred_element_type=F32))
    return jnp.concatenate(outs, axis=1) * scale


def _four_ctx_body(f_ref, dft_ref, wf_ref, o_ref):
    for b in range(CTX_PER_STEP):
        u = jnp.dot(dft_ref[...], f_ref[b].astype(BF16), preferred_element_type=F32)
        o_ref[b] = _fold_groups(u[:T_CTX], u[T_CTX:], wf_ref, 1.0 / math.sqrt(T_CTX * F_CH)).astype(o_ref.dtype)


def _four_ctx(f, wfold_bf, layer):
    ang = 2.0 * np.pi * np.outer(np.arange(T_CTX), np.arange(T_CTX)) / T_CTX
    dft = jnp.asarray(np.concatenate([np.cos(ang), -np.sin(ang)], axis=0), F32).astype(BF16)
    return pl.pallas_call(
        _four_ctx_body,
        grid=(N_CTX_B // CTX_PER_STEP,),
        in_specs=[pl.BlockSpec((CTX_PER_STEP, T_CTX, FOUR_W), lambda b: (b, 0, 0)),
                  pl.BlockSpec((2 * T_CTX, T_CTX), lambda b: (0, 0)),
                  pl.BlockSpec((None, F_GROUPS, 2 * F_CH, F_CH), lambda b: (layer, 0, 0, 0))],
        out_specs=pl.BlockSpec((CTX_PER_STEP, T_CTX, FOUR_W), lambda b: (b, 0, 0)),
        out_shape=jax.ShapeDtypeStruct((N_CTX_B, T_CTX, FOUR_W), BF16),
        compiler_params=_cparams(("arbitrary",)),
        name="fourier_ctx",
    )(f.reshape(N_CTX_B, T_CTX, FOUR_W), dft, wfold_bf)


def _fft_body(x_ref, ma_ref, twc_ref, tws_ref, mc_ref, ms_ref, wf_ref, o_ref, h_ref):
    nb = FFT_R * FFT_S
    nj = FFT_R // FFT_S
    j = pl.program_id(1)

    @pl.when(j < nj)
    def _():
        x = x_ref[...].reshape(nb, FOUR_W).astype(BF16)
        g = jnp.dot(ma_ref[...], x, preferred_element_type=F32)
        gr, gi = g[:nb], g[nb:]
        twc = jnp.tile(twc_ref[...], (1, FOUR_W // 128))
        tws = jnp.tile(tws_ref[...], (1, FOUR_W // 128))
        lo = pl.ds(pl.multiple_of(j * FFT_S, FFT_S), FFT_S)
        h_ref[lo, :, 0:FOUR_W] = (gr * twc + gi * tws).reshape(FFT_S, FFT_R, FOUR_W)
        h_ref[lo, :, FOUR_W:] = (gi * twc - gr * tws).reshape(FFT_S, FFT_R, FOUR_W)

    @pl.when(j >= nj)
    def _():
        hi = pl.ds(pl.multiple_of((j - nj) * FFT_S, FFT_S), FFT_S)
        h = h_ref[:, hi, :].reshape(nb, 2 * FOUR_W).astype(BF16)
        p = jnp.dot(mc_ref[...], h, preferred_element_type=F32)
        q = jnp.dot(ms_ref[...], h, preferred_element_type=F32)
        ur = p[:, :FOUR_W] + q[:, FOUR_W:]
        ui = p[:, FOUR_W:] - q[:, :FOUR_W]
        out = _fold_groups(ur, ui, wf_ref, 1.0 / math.sqrt(T_LAT * F_CH))
        o_ref[...] = out.reshape(FFT_R, FFT_S, FOUR_W)


def _fft_consts():
    r, s = FFT_R, FFT_S
    nb = r * s
    ang = 2.0 * np.pi * np.outer(np.arange(r), np.arange(r)) / r
    c, sn = np.cos(ang), np.sin(ang)
    eye = np.eye(s)
    ma = np.concatenate([np.einsum('pb,ts->tpbs', c, eye).reshape(nb, nb),
                         np.einsum('pb,ts->tpbs', -sn, eye).reshape(nb, nb)], axis=0)
    mbc = np.einsum('pa,ts->ptas', c, eye).reshape(nb, nb)
    mbs = np.einsum('pa,ts->ptas', sn, eye).reshape(nb, nb)
    tw = 2.0 * np.pi * np.outer(np.arange(r), np.arange(r)).reshape(-1) / (r * r)
    twc = np.broadcast_to(np.cos(tw)[:, None], (r * r, 128))
    tws = np.broadcast_to(np.sin(tw)[:, None], (r * r, 128))
    bf = lambda a: jnp.asarray(a, F32).astype(BF16)
    return bf(ma), bf(mbc), bf(mbs), jnp.asarray(twc, F32), jnp.asarray(tws, F32)


def _four_lat(f, wfold_bf, layer, consts):
    ma, mbc, mbs, twc, tws = consts
    r, s = FFT_R, FFT_S
    nb = r * s
    nj = r // s
    f4 = f.reshape(N_LAT_B, r, r, FOUR_W)
    a_step = lambda j: jnp.minimum(j, nj - 1)
    b_step = lambda j: jnp.maximum(j - nj, 0)
    const = lambda b, j: (0, 0)
    out = pl.pallas_call(
        _fft_body,
        grid=(N_LAT_B, 2 * nj),
        in_specs=[pl.BlockSpec((None, r, s, FOUR_W), lambda b, j: (b, 0, a_step(j), 0)),
                  pl.BlockSpec((2 * nb, nb), const),
                  pl.BlockSpec((nb, 128), lambda b, j: (a_step(j), 0)),
                  pl.BlockSpec((nb, 128), lambda b, j: (a_step(j), 0)),
                  pl.BlockSpec((nb, nb), const),
                  pl.BlockSpec((nb, nb), const),
                  pl.BlockSpec((None, F_GROUPS, 2 * F_CH, F_CH), lambda b, j: (layer, 0, 0, 0))],
        out_specs=pl.BlockSpec((None, r, s, FOUR_W), lambda b, j: (b, 0, b_step(j), 0)),
        out_shape=jax.ShapeDtypeStruct((N_LAT_B, r, r, FOUR_W), F32),
        scratch_shapes=[pltpu.VMEM((r, r, 2 * FOUR_W), F32)],
        compiler_params=_cparams(("arbitrary", "arbitrary")),
        name="fft_lat",
    )(f4, ma, twc, tws, mbc, mbs, wfold_bf)
    return out.reshape(N_LAT_B * T_LAT, FOUR_W)


def _outproj_body(a_ref, f_ref, x_ref, gt_ref, sh_ref, sc_ref, gpost_ref, gpre_ref, wo_ref, *rest,
                  moe, group0, tiles_per_group):
    if moe:
        wr_ref, x1_ref, hn_ref, gates_ref, sel_ref, len_ref = rest
    else:
        x1_ref, hn_ref = rest
    grp = _group_of(pl.program_id(0), group0, tiles_per_group)
    gt, sh, sc = gt_ref[pl.ds(grp, 1), :], sh_ref[pl.ds(grp, 1), :], sc_ref[pl.ds(grp, 1), :]
    for half in range(TM_OUT // TM):
        rows = slice(half * TM, (half + 1) * TM)
        mix_in = jnp.concatenate([a_ref[rows, :], f_ref[rows, :].astype(BF16)], axis=1)
        mixed = jnp.dot(mix_in, wo_ref[...], preferred_element_type=F32)
        x1 = x_ref[rows, :] + gt * _rms(mixed, gpost_ref[...])
        x1_ref[rows, :] = x1
        hn = _rms(x1, gpre_ref[...]) * (1.0 + sc) + sh
        hn_ref[rows, :] = hn.astype(hn_ref.dtype)
        if moe:
            gates, sel = _route_top2(hn, wr_ref[...])
            gates_ref[rows, :] = gates
            sel_ref[rows, :] = sel.astype(sel_ref.dtype)
            group_rows = jnp.ceil(jnp.sum(sel, axis=0, keepdims=True) * (1.0 / GROUP)) * GROUP
            len_ref[half] = jnp.broadcast_to(group_rows, (N_GROUPS_PAD, E_PAD))


def _route_top2(hn, wr):
    wr_hi = wr.astype(BF16)
    wr_lo = (wr - wr_hi.astype(F32)).astype(BF16)
    hn_hi = hn.astype(BF16)
    hn_lo = (hn - hn_hi.astype(F32)).astype(BF16)
    hh = jnp.dot(hn_hi, jnp.concatenate([wr_hi, wr_lo], axis=1), preferred_element_type=F32)
    logits = hh[:, :E_PAD] + hh[:, E_PAD:] + jnp.dot(hn_lo, wr_hi, preferred_element_type=F32)
    lane = lax.broadcasted_iota(jnp.int32, logits.shape, 1).astype(F32)
    neg = jnp.float32(-jnp.inf)
    logits = jnp.where(lane < float(N_EXPERTS), logits, neg)
    m1 = jnp.max(logits, axis=-1, keepdims=True)
    i1 = jnp.min(jnp.where(logits == m1, lane, float(E_PAD)), axis=-1, keepdims=True)
    rest_l = jnp.where(lane == i1, neg, logits)
    m2 = jnp.max(rest_l, axis=-1, keepdims=True)
    i2 = jnp.min(jnp.where(rest_l == m2, lane, float(E_PAD)), axis=-1, keepdims=True)
    e2 = jnp.exp(m2 - m1)
    w1 = 1.0 / (1.0 + e2)
    w2 = e2 / (1.0 + e2)
    gates = jnp.where(lane == i1, w1, 0.0) + jnp.where(lane == i2, w2, 0.0)
    sel = jnp.where((lane == i1) | (lane == i2), 1.0, 0.0)
    return gates, sel


def _outproj(attn_o, four_o, x, mod, layer, g_post, g_ffn_pre, w_out_bf, w_router_pad, group0, tiles_per_group):
    rows = x.shape[0]
    moe = w_router_pad is not None
    row = lambda i: (i, 0)
    const = lambda i: (0, 0)
    modspec = lambda k: pl.BlockSpec((None, N_GROUPS_PAD, D), lambda i: (layer, 0, k))
    blocks_per_group = None if tiles_per_group is None else tiles_per_group * TM // TM_OUT
    in_specs = [pl.BlockSpec((TM_OUT, ATTN_W), row), pl.BlockSpec((TM_OUT, FOUR_W), row),
                pl.BlockSpec((TM_OUT, D), row),
                modspec(2), modspec(3), modspec(4),
                pl.BlockSpec((1, D), const), pl.BlockSpec((1, D), const), pl.BlockSpec((D, D), const)]
    args = [attn_o, four_o, x, mod, mod, mod, g_post.reshape(1, D), g_ffn_pre.reshape(1, D), w_out_bf]
    out_specs = [pl.BlockSpec((TM_OUT, D), row), pl.BlockSpec((TM_OUT, D), row)]
    out_shape = [jax.ShapeDtypeStruct((rows, D), F32), jax.ShapeDtypeStruct((rows, D), BF16)]
    if moe:
        in_specs.append(pl.BlockSpec((D, E_PAD), const))
        args.append(w_router_pad)
        out_specs += [pl.BlockSpec((TM_OUT, E_PAD), row)] * 2
        out_specs.append(pl.BlockSpec((TM_OUT // TM, N_GROUPS_PAD, E_PAD), lambda i: (i, 0, 0)))
        out_shape += [jax.ShapeDtypeStruct((rows, E_PAD), F32), jax.ShapeDtypeStruct((rows, E_PAD), BF16),
                      jax.ShapeDtypeStruct((rows // TM, N_GROUPS_PAD, E_PAD), F32)]
    return pl.pallas_call(
        functools.partial(_outproj_body, moe=moe, group0=group0, tiles_per_group=blocks_per_group),
        grid=(rows // TM_OUT,),
        in_specs=in_specs, out_specs=out_specs, out_shape=out_shape,
        compiler_params=_cparams(("arbitrary",)),
        name="outproj_moe" if moe else "outproj",
    )(*args)


def _swiglu(hn, wg, wu, wd):
    g = jnp.dot(hn, wg, preferred_element_type=F32)
    u = jnp.dot(hn, wu, preferred_element_type=F32)
    return jnp.dot((_silu(g) * u).astype(BF16), wd, preferred_element_type=F32)


def _ffn_dense_body(hn_ref, x1_ref, gt_ref, gpost_ref, wg_ref, wu_ref, wd_ref, o_ref, *, group0, tiles_per_group):
    grp = _group_of(pl.program_id(0), group0, tiles_per_group)
    ff = _swiglu(hn_ref[...], wg_ref[...], wu_ref[...], wd_ref[...])
    o_ref[...] = x1_ref[...] + gt_ref[pl.ds(grp, 1), :] * _rms(ff, gpost_ref[...])


def _ffn_dense(hn, x1, mod, layer, g_post, wg, wu, wd, group0, tiles_per_group):
    rows = x1.shape[0]
    row = lambda i: (i, 0)
    const = lambda i: (0, 0)
    resident = lambda shape: pl.BlockSpec(shape, const, pipeline_mode=pl.Buffered(1))
    return pl.pallas_call(
        functools.partial(_ffn_dense_body, group0=group0, tiles_per_group=tiles_per_group),
        grid=(rows // TM,),
        in_specs=[pl.BlockSpec((TM, D), row), pl.BlockSpec((TM, D), row),
                  pl.BlockSpec((None, N_GROUPS_PAD, D), lambda i: (layer, 0, 5)),
                  pl.BlockSpec((1, D), const),
                  resident((D, D_FF)), resident((D, D_FF)), resident((D_FF, D))],
        out_specs=pl.BlockSpec((TM, D), row),
        out_shape=jax.ShapeDtypeStruct((rows, D), F32),
        compiler_params=_cparams(("arbitrary",)),
        name="ffn_dense",
    )(hn, x1, mod, g_post.reshape(1, D), wg, wu, wd)


def _experts_body(te_ref, tn_ref, tnext_ref, xs_ref, wg_hbm, wu_hbm, wd_hbm, ys_ref, w32, wg_bf, wu_bf, wd_bf, sem):
    i = pl.program_id(0)
    e = te_ref[i]
    fresh = (i == 0) | (e != te_ref[jnp.maximum(i - 1, 0)])

    def fetch(expert, act):
        for j, hbm in enumerate((wg_hbm, wu_hbm, wd_hbm)):
            act(pltpu.make_async_copy(hbm.at[expert], w32.at[j], sem))

    @pl.when(i == 0)
    def _():
        fetch(e, lambda c: c.start())

    @pl.when(fresh)
    def _():
        fetch(e, lambda c: c.wait())
        wg_bf[...] = w32[0].astype(BF16)
        wu_bf[...] = w32[1].astype(BF16)
        wd_bf[...] = w32[2].astype(BF16)
        nxt = tnext_ref[i]

        @pl.when(nxt != e)
        def _():
            fetch(nxt, lambda c: c.start())

    n_real = tn_ref[i]

    @pl.when(n_real > 0)
    def _():
        ys_ref[...] = _swiglu(xs_ref[...].astype(BF16), wg_bf[...], wu_bf[...], wd_bf[...])

    @pl.when(n_real == 0)
    def _():
        ys_ref[...] = jnp.zeros_like(ys_ref)


def _experts(xs, tile_e, tile_n, tile_next, eg, eu, ed):
    assert D == D_FF_E
    tile = lambda i, te, tn, tx: (i, 0)
    return pl.pallas_call(
        _experts_body,
        grid_spec=pltpu.PrefetchScalarGridSpec(
            num_scalar_prefetch=3,
            grid=(N_TILES,),
            in_specs=[pl.BlockSpec((TMOE, D), tile)] + [pl.BlockSpec(memory_space=pl.ANY)] * 3,
            out_specs=pl.BlockSpec((TMOE, D), tile),
            scratch_shapes=[pltpu.VMEM((3, D, D), F32),
                            pltpu.VMEM((D, D_FF_E), BF16), pltpu.VMEM((D, D_FF_E), BF16),
                            pltpu.VMEM((D_FF_E, D), BF16), pltpu.SemaphoreType.DMA]),
        out_shape=jax.ShapeDtypeStruct((NP, D), F32),
        compiler_params=_cparams(("arbitrary",)),
        name="moe_experts",
    )(tile_e, tile_n, tile_next, xs, eg, eu, ed)


def _route_body(*refs, firsts):
    ns = len(firsts)
    gates_refs, sel_refs, len_refs = refs[:ns], refs[ns:2 * ns], refs[2 * ns:3 * ns]
    ltri_ref, w_ref, rho_ref, tmeta_ref, meta_ref, off_ref, run_ref = refs[3 * ns:]
    t = pl.program_id(0)

    def this_stream(stream_refs):
        val = stream_refs[0][...]
        for k in range(1, ns):
            val = jnp.where(t >= firsts[k], stream_refs[k][...], val)
        return val

    sel = this_stream(sel_refs)
    group_rows = jnp.ceil(jnp.sum(sel.astype(F32), axis=0, keepdims=True) * (1.0 / GROUP)) * GROUP

    @pl.when(t == 0)
    def _():
        cnt = sum(jnp.sum(r[...].reshape(-1, E_PAD), axis=0, keepdims=True) for r in len_refs) * (1.0 / N_GROUPS_PAD)
        off_ref[...] = _segment_meta(cnt, meta_ref)
        run_ref[...] = jnp.zeros_like(run_ref)

    lane1 = lax.broadcasted_iota(jnp.int32, group_rows.shape, 1).astype(F32)
    buf0 = jnp.zeros_like(group_rows)
    running = jnp.zeros((1, 1), F32)
    for e in range(N_EXPERTS):
        pick = lane1 == float(e)
        buf0 = jnp.where(pick, running, buf0)
        running = running + jnp.sum(jnp.where(pick, group_rows, 0.0), axis=-1, keepdims=True)
    slot0 = off_ref[...] + run_ref[...]
    run_ref[...] += group_rows
    row8 = lax.broadcasted_iota(jnp.int32, tmeta_ref.shape, 0)
    tmeta_ref[...] = jnp.where(row8 == 0, slot0, jnp.where(row8 == 1, group_rows,
                               jnp.where(row8 == 2, buf0, 0.0)))

    earlier = jnp.dot(ltri_ref[...], sel, preferred_element_type=F32)
    rho = earlier + buf0
    chosen = sel > 0
    r_lo = jnp.min(jnp.where(chosen, rho, float(R_TILE)), axis=-1, keepdims=True)
    r_hi = jnp.max(jnp.where(chosen, rho, -1.0), axis=-1, keepdims=True)
    g = this_stream(gates_refs)
    w_lo = jnp.sum(jnp.where(chosen & (rho == r_lo), g, 0.0), axis=-1, keepdims=True)
    w_hi = jnp.sum(jnp.where(chosen & (rho == r_hi), g, 0.0), axis=-1, keepdims=True)
    lane = lax.broadcasted_iota(jnp.int32, rho.shape, 1)
    w_ref[...] = jnp.where(lane == 2, w_lo, jnp.where(lane == 3, w_hi,
                           jnp.where(lane == 4, r_lo, jnp.where(lane == 5, r_hi, 0.0))))
    cols = jnp.where(lane == 0, r_lo, jnp.where(lane == 1, r_hi, 0.0))
    rho_ref[...] = jnp.transpose(cols)[0:N_GROUPS_PAD, :]


def _segment_meta(cnt, meta_ref):
    padded = jnp.ceil(cnt * (1.0 / TMOE)) * TMOE
    lane = lax.broadcasted_iota(jnp.int32, cnt.shape, 1).astype(F32)
    pad_of = lambda e: jnp.sum(jnp.where(lane == float(e), padded, 0.0), axis=-1, keepdims=True)
    following = [None] * N_EXPERTS
    seen = jnp.full((1, 1), -1.0, F32)
    for e in reversed(range(N_EXPERTS)):
        following[e] = seen
        seen = jnp.where(pad_of(e) > 0.0, float(e), seen)
    start = lane * TMOE
    off = jnp.zeros_like(cnt)
    tile_e = jnp.zeros_like(cnt)
    tile_n = jnp.zeros_like(cnt)
    tile_next = jnp.zeros_like(cnt)
    running = jnp.zeros((1, 1), F32)
    last_e = jnp.zeros((1, 1), F32)
    for e in range(N_EXPERTS):
        pick = lane == float(e)
        cnt_e = jnp.sum(jnp.where(pick, cnt, 0.0), axis=-1, keepdims=True)
        pad_e = pad_of(e)
        off = jnp.where(pick, running, off)
        in_seg = (start >= running) & (start < running + pad_e)
        tile_e = jnp.where(in_seg, float(e), tile_e)
        tile_n = jnp.where(in_seg, jnp.clip(cnt_e - (start - running), 0.0, float(TMOE)), tile_n)
        tile_next = jnp.where(in_seg, jnp.where(following[e] >= 0.0, following[e], float(e)), tile_next)
        last_e = jnp.where(pad_e > 0.0, float(e), last_e)
        running = running + pad_e
    tile_e = jnp.where(start >= running, last_e, tile_e)
    tile_next = jnp.where(start >= running, last_e, tile_next)
    row = lax.broadcasted_iota(jnp.int32, meta_ref.shape, 0)
    last_tile = jnp.where(lane < float(N_EXPERTS), off + padded - TMOE, running)
    last_tile = jnp.where((lane < float(N_EXPERTS)) & (padded == 0.0), -1.0, last_tile)
    meta_ref[...] = jnp.where(row == 0, tile_e, jnp.where(row == 1, tile_n,
                              jnp.where(row == 2, last_tile, jnp.where(row == 3, tile_next, 0.0))))
    return off


def _route(gates, sels, lens):
    tiles = [g.shape[0] // TM for g in gates]
    firsts = [int(f) for f in np.cumsum([0] + tiles)]
    n, rows = firsts[-1], firsts[-1] * TM
    ltri = jnp.asarray(np.tril(np.ones((TM, TM)), -1), F32).astype(BF16)
    stream_spec = lambda lo, cnt: pl.BlockSpec((TM, E_PAD), lambda t: (jnp.clip(t - lo, 0, cnt - 1), 0))
    stream_specs = [stream_spec(lo, cnt) for lo, cnt in zip(firsts[:-1], tiles)]
    whole = [pl.BlockSpec(ln.shape, lambda t: (0, 0, 0)) for ln in lens]
    per_tile = lambda t: (t, 0, 0)
    return pl.pallas_call(
        functools.partial(_route_body, firsts=tuple(firsts[:-1])),
        grid=(n,),
        in_specs=stream_specs + stream_specs + whole + [pl.BlockSpec((TM, TM), lambda t: (0, 0))],
        out_specs=[pl.BlockSpec((TM, E_PAD), lambda t: (t, 0)),
                   pl.BlockSpec((None, N_GROUPS_PAD, TM), per_tile),
                   pl.BlockSpec((None, N_GROUPS_PAD, E_PAD), per_tile),
                   pl.BlockSpec((N_GROUPS_PAD, E_PAD), lambda t: (0, 0))],
        out_shape=[jax.ShapeDtypeStruct((rows, E_PAD), F32),
                   jax.ShapeDtypeStruct((n, N_GROUPS_PAD, TM), F32),
                   jax.ShapeDtypeStruct((n, N_GROUPS_PAD, E_PAD), F32),
                   jax.ShapeDtypeStruct((N_GROUPS_PAD, E_PAD), F32)],
        scratch_shapes=[pltpu.VMEM((1, E_PAD), F32)] * 2,
        compiler_params=_cparams(("arbitrary",)),
        name="moe_route",
    )(*gates, *sels, *lens, ltri)


def _group_copies(tmeta_ref, slot_window, buf_window, sem, to_slots, act):
    for e in range(N_EXPERTS):
        slot0, n, buf0 = tmeta_ref[0, e], tmeta_ref[1, e], tmeta_ref[2, e]
        done = jnp.int32(0)
        size = TM
        while size >= GROUP:
            take = (n - done) >= size
            s_at = pl.ds(pl.multiple_of(slot0 + done, GROUP), size)
            b_at = pl.ds(pl.multiple_of(buf0 + done, GROUP), size)
            src, dst = (buf_window(b_at), slot_window(s_at)) if to_slots else (slot_window(s_at), buf_window(b_at))

            @pl.when(take)
            def _(src=src, dst=dst):
                act(pltpu.make_async_copy(src, dst, sem))

            done = done + jnp.where(take, size, 0)
            size //= 2


def _dispatch_body(tmeta_ref, prev_tmeta_ref, pad_ref, rho_ref, *rest, firsts):
    ns = len(firsts)
    hn_refs, (xs_ref, pbuf, zeros, sems) = rest[:ns], rest[ns:]
    i = pl.program_id(0)
    cur = lax.rem(i, 2)
    sem = sems.at[0]

    @pl.when(i == 0)
    def _():
        zeros[...] = jnp.zeros_like(zeros)
        fills = []
        for e in range(N_EXPERTS):
            s = pad_ref[0, e]
            fills.append((s >= 0, pl.multiple_of(jnp.maximum(s, 0), TMOE)))
        for j in range(N_TILES - TOP_K * N_TOKENS // TMOE):
            s = NP - (j + 1) * TMOE
            fills.append((s >= pad_ref[0, N_EXPERTS], s))
        fill = lambda s: pltpu.make_async_copy(zeros, xs_ref.at[pl.ds(s, TMOE), :], sem)
        for live, s in fills:
            @pl.when(live)
            def _(s=s):
                fill(s).start()
        for live, s in fills:
            @pl.when(live)
            def _(s=s):
                fill(s).wait()

    hn = hn_refs[0][...]
    for k in range(1, ns):
        hn = jnp.where(i >= firsts[k], hn_refs[k][...], hn)
    row = lax.broadcasted_iota(jnp.int32, (R_TILE, TM), 0).astype(F32)
    onehot = jnp.where((row == rho_ref[0:1, :]) | (row == rho_ref[1:2, :]), 1.0, 0.0).astype(BF16)
    pbuf[cur] = jnp.dot(onehot, hn, preferred_element_type=F32)

    def store(meta_ref, half, act):
        _group_copies(meta_ref, lambda at: xs_ref.at[at, :], lambda at: pbuf.at[half, at, :], sems.at[half],
                      True, act)

    store(tmeta_ref, cur, lambda c: c.start())

    @pl.when(i > 0)
    def _():
        store(prev_tmeta_ref, 1 - cur, lambda c: c.wait())

    @pl.when(i + 1 == pl.num_programs(0))
    def _():
        store(tmeta_ref, cur, lambda c: c.wait())


def _dispatch(hns, tmeta, rho, pad_starts):
    tiles = [h.shape[0] // TM for h in hns]
    firsts = [int(f) for f in np.cumsum([0] + tiles)]
    n = firsts[-1]
    stream_spec = lambda lo, cnt: pl.BlockSpec((TM, D), lambda i: (jnp.clip(i - lo, 0, cnt - 1), 0))
    return pl.pallas_call(
        functools.partial(_dispatch_body, firsts=tuple(firsts[:-1])),
        grid=(n,),
        in_specs=[pl.BlockSpec((None, N_GROUPS_PAD, E_PAD), lambda i: (i, 0, 0), memory_space=pltpu.SMEM),
                  pl.BlockSpec((None, N_GROUPS_PAD, E_PAD), lambda i: (jnp.maximum(i - 1, 0), 0, 0),
                               memory_space=pltpu.SMEM),
                  pl.BlockSpec(memory_space=pltpu.SMEM),
                  pl.BlockSpec((None, N_GROUPS_PAD, TM), lambda i: (i, 0, 0))]
                 + [stream_spec(lo, cnt) for lo, cnt in zip(firsts[:-1], tiles)],
        out_specs=pl.BlockSpec(memory_space=pl.ANY),
        out_shape=jax.ShapeDtypeStruct((NP, D), F32),
        scratch_shapes=[pltpu.VMEM((2, R_TILE, D), F32), pltpu.VMEM((TMOE, D), F32),
                        pltpu.SemaphoreType.DMA((2,))],
        compiler_params=_cparams(("arbitrary",)),
        name="moe_dispatch",
    )(tmeta, tmeta, pad_starts, rho, *hns)


def _combine_body(tmeta_ref, next_tmeta_ref, w_ref, x1_ref, gt_ref, gpost_ref, ys_ref, o_ref, ybuf, sems,
                   *, group0, tiles_per_group):
    i = pl.program_id(0)
    grp = _group_of(i, group0, tiles_per_group)
    cur = lax.rem(i, 2)

    def fetch(meta_ref, half, act):
        _group_copies(meta_ref, lambda at: ys_ref.at[at, :], lambda at: ybuf.at[half, at, :], sems.at[half],
                      False, act)

    @pl.when(i == 0)
    def _():
        ybuf[...] = jnp.zeros_like(ybuf)
        fetch(tmeta_ref, 0, lambda c: c.start())

    @pl.when(i + 1 < pl.num_programs(0))
    def _():
        fetch(next_tmeta_ref, 1 - cur, lambda c: c.start())

    fetch(tmeta_ref, cur, lambda c: c.wait())
    w = w_ref[...]
    lane = lax.broadcasted_iota(jnp.int32, w.shape, 1)
    col = lambda j: jnp.sum(jnp.where(lane == j, w, 0.0), axis=-1, keepdims=True)
    w_lo, w_hi, r_lo, r_hi = col(2), col(3), col(4), col(5)
    y = ybuf[cur].astype(BF16)
    pos = lax.broadcasted_iota(jnp.int32, (TM, R_TILE), 1).astype(F32)
    pick = lambda r: jnp.dot(jnp.where(pos == r, 1.0, 0.0).astype(BF16), y, preferred_element_type=F32)
    ff = w_lo * pick(r_lo) + w_hi * pick(r_hi)
    o_ref[...] = x1_ref[...] + gt_ref[pl.ds(grp, 1), :] * _rms(ff, gpost_ref[...])


def _combine(ys, tmeta, slot_w, tile0, x1, mod, layer, g_post, group0, tiles_per_group):
    rows = x1.shape[0]
    n = rows // TM
    row = lambda i: (i, 0)
    return pl.pallas_call(
        functools.partial(_combine_body, group0=group0, tiles_per_group=tiles_per_group),
        grid=(n,),
        in_specs=[pl.BlockSpec((None, N_GROUPS_PAD, E_PAD), lambda i: (tile0 + i, 0, 0), memory_space=pltpu.SMEM),
                  pl.BlockSpec((None, N_GROUPS_PAD, E_PAD), lambda i: (tile0 + jnp.minimum(i + 1, n - 1), 0, 0),
                               memory_space=pltpu.SMEM),
                  pl.BlockSpec((TM, E_PAD), lambda i: (tile0 + i, 0)), pl.BlockSpec((TM, D), row),
                  pl.BlockSpec((None, N_GROUPS_PAD, D), lambda i: (layer, 0, 5)),
                  pl.BlockSpec((1, D), lambda i: (0, 0)),
                  pl.BlockSpec(memory_space=pl.ANY)],
        out_specs=pl.BlockSpec((TM, D), row),
        out_shape=jax.ShapeDtypeStruct((rows, D), F32),
        scratch_shapes=[pltpu.VMEM((2, R_TILE, D), F32), pltpu.SemaphoreType.DMA((2,))],
        compiler_params=_cparams(("arbitrary",)),
        name="moe_combine",
    )(tmeta, tmeta, slot_w, x1, mod, g_post.reshape(1, D), ys)


def kernel(x_prompt, x_sample, cache_k, cache_v, c, c_ctx, g_attn_pre, g_attn_post, g_ffn_pre, g_ffn_post,
           w_ada, b_ada, w_in, lam_params, g_subln, w_fnet, w_out, w_gate, w_up, w_down, w_router,
           e_gate, e_up, e_down):
    assert x_prompt.shape == (N_CTX_B, T_CTX, D) and x_sample.shape == (N_LAT_B, T_LAT, D)
    assert cache_k.shape == (N_LAT_B, DEPTH, T_PAST, N_HEADS, 2, QK_DIM)

    cond = jnp.zeros((N_GROUPS_PAD, D), F32).at[0].set(c_ctx).at[1:1 + N_LAT_B].set(c)
    mod = _ada(cond, w_ada, b_ada)
    wfold_bf = _wfold(w_fnet).astype(BF16)
    rope_tabs = _rope_tables()
    fft_consts = _fft_consts()

    w_in_bf = w_in.astype(BF16)
    w_out_bf = w_out.astype(BF16)
    w_router_pad = jnp.zeros((DEPTH // 2, D, E_PAD), F32).at[:, :, :N_EXPERTS].set(w_router)
    ck_bf = cache_k.reshape(N_LAT_B, DEPTH, T_PAST, ATTN_W).astype(BF16)
    cv_bf = cache_v.reshape(N_LAT_B, DEPTH, T_PAST, ATTN_W).astype(BF16)

    lat_tiles = T_LAT // TM
    streams = [dict(x=x_prompt.reshape(N_CTX_B * T_CTX, D), group0=0, tiles=None, lat=False),
               dict(x=x_sample.reshape(N_LAT_B * T_LAT, D), group0=1, tiles=lat_tiles, lat=True)]
    new_k = new_v = None
    for l in range(DEPTH):
        lam_init = 0.8 - 0.6 * math.exp(-0.3 * l)
        i = l // 2
        routed = []
        for s in streams:
            g0, tiles = s["group0"], s["tiles"]
            if s["lat"]:
                qkv, f = _inproj(s["x"], mod, l, g_attn_pre[l], w_in_bf[l], rope_tabs, g0, tiles)
                attn_o = _attn_lat(qkv, ck_bf[:, l], cv_bf[:, l], lam_params, g_subln, l, lam_init)
                attn_o = attn_o.reshape(N_LAT_B * T_LAT, ATTN_W)
                four_o = _four_lat(f, wfold_bf, l, fft_consts)
            else:
                qkv, f, new_k, new_v = _inproj(s["x"], mod, l, g_attn_pre[l], w_in_bf[l], None, g0, tiles,
                                               cache=None if l == 0 else (new_k, new_v))
                attn_o = _attn_ctx(qkv, lam_params, g_subln, l, lam_init).reshape(N_CTX_B * T_CTX, ATTN_W)
                four_o = _four_ctx(f, wfold_bf, l).reshape(N_CTX_B * T_CTX, FOUR_W)
            if l % 2 == 0:
                x1, hn = _outproj(attn_o, four_o, s["x"], mod, l, g_attn_post[l], g_ffn_pre[l], w_out_bf[l],
                                  None, g0, tiles)
                s["x"] = _ffn_dense(hn, x1, mod, l, g_ffn_post[l], w_gate[i].astype(BF16), w_up[i].astype(BF16),
                                    w_down[i].astype(BF16), g0, tiles)
            else:
                routed.append(_outproj(attn_o, four_o, s["x"], mod, l, g_attn_post[l], g_ffn_pre[l],
                                       w_out_bf[l], w_router_pad[i], g0, tiles))
        if l % 2 == 1:
            slot_w, rho, tmeta, meta = _route(*[[r[k] for r in routed] for k in (2, 3, 4)])
            tmeta = tmeta.astype(jnp.int32)
            tile_e = meta[0, :N_TILES].astype(jnp.int32)
            tile_n = meta[1, :N_TILES].astype(jnp.int32)
            tile_next = meta[3, :N_TILES].astype(jnp.int32)
            pad_starts = meta[2:3, :N_EXPERTS + 1].astype(jnp.int32)
            bounds = np.cumsum([0] + [r[0].shape[0] for r in routed])
            xs = _dispatch([r[1] for r in routed], tmeta, rho, pad_starts)
            ys = _experts(xs, tile_e, tile_n, tile_next, e_gate[i], e_up[i], e_down[i])
            for s, r, lo in zip(streams, routed, bounds[:-1]):
                s["x"] = _combine(ys, tmeta, slot_w, int(lo) // TM, r[0], mod, l, g_ffn_post[l],
                                   s["group0"], s["tiles"])

    y_prompt = streams[0]["x"].reshape(N_CTX_B, T_CTX, D)
    y_sample = streams[1]["x"].reshape(N_LAT_B, T_LAT, D)
    new_cache_k = new_k.reshape(N_CTX_B, DEPTH, T_CTX, N_HEADS, 2, QK_DIM)
    new_cache_v = new_v.reshape(N_CTX_B, DEPTH, T_CTX, N_HEADS, HEAD_W)
    return (y_prompt, y_sample, new_cache_k, new_cache_v)
```

```python
import functools
import math

import numpy as np
import jax
import jax.numpy as jnp
from jax import lax
from jax.experimental import pallas as pl
from jax.experimental.pallas import tpu as pltpu

F32 = jnp.float32
BF16 = jnp.bfloat16

D = 1024
N_CTX_B = 16
T_CTX = 256
N_LAT_B = 2
T_LAT = 4096
T_PAST = 512
DEPTH = 2
GRID_W = 64
N_HEADS = 4
QK_DIM = 64
HEAD_W = 2 * QK_DIM
ATTN_W = N_HEADS * HEAD_W
F_GROUPS = 4
F_CH = 128
FOUR_W = F_GROUPS * F_CH
IN_W = 3 * ATTN_W + FOUR_W
QKV_W = 3 * ATTN_W
D_FF = 2816
N_EXPERTS = 8
D_FF_E = 1024
N_MOD = 6
EPS = 1e-6
ROPE_THETA = 10000.0
N_GROUPS_PAD = 8
LANE = 128
E_PAD = LANE
ROPE_AXIS_W = QK_DIM // 2
ROPE_HALF = ROPE_AXIS_W // 2

TM = 512
CTX_PER_STEP = 4
TM_OUT = 1024
TOP_K = 2
TMOE = 512
N_TOKENS = N_CTX_B * T_CTX + N_LAT_B * T_LAT
GROUP = 8
R_TILE = TOP_K * TM + N_EXPERTS * GROUP
N_TILES = -(-(TOP_K * N_TOKENS + (N_TOKENS // TM) * N_EXPERTS * (GROUP - 1)) // TMOE) + N_EXPERTS
NP = N_TILES * TMOE
TQ = 512
Q_SUB = 2
TK = 1536
ATTN_UNROLL = 3
FFT_R = 64
FFT_S = 8
VMEM_LIMIT = 56 * 1024 * 1024


def _cparams(sem):
    return pltpu.CompilerParams(dimension_semantics=sem, vmem_limit_bytes=VMEM_LIMIT)


def _rms(x, g):
    ms = jnp.mean(x * x, axis=-1, keepdims=True)
    return x * lax.rsqrt(ms + EPS) * g


def _silu(x):
    return x * (1.0 / (1.0 + jnp.exp(-x)))


def _ada_body(c_ref, w_ref, b_ref, o_ref):
    s = _silu(c_ref[...])
    s_hi = s.astype(BF16)
    s_lo = (s - s_hi.astype(F32)).astype(BF16)
    both = jnp.dot(jnp.concatenate([s_hi, s_lo], axis=0), w_ref[...].astype(BF16), preferred_element_type=F32)
    o_ref[...] = both[:N_GROUPS_PAD] + both[N_GROUPS_PAD:] + b_ref[...]


def _ada(cond, w_ada, b_ada):
    tn = 1536
    return pl.pallas_call(
        _ada_body,
        grid=(DEPTH, N_MOD * D // tn),
        in_specs=[pl.BlockSpec((N_GROUPS_PAD, D), lambda l, j: (0, 0)),
                  pl.BlockSpec((None, D, tn), lambda l, j: (l, 0, j)),
                  pl.BlockSpec((None, 1, tn), lambda l, j: (l, 0, j))],
        out_specs=pl.BlockSpec((None, N_GROUPS_PAD, tn), lambda l, j: (l, 0, j)),
        out_shape=jax.ShapeDtypeStruct((DEPTH, N_GROUPS_PAD, N_MOD * D), F32),
        compiler_params=_cparams(("arbitrary", "arbitrary")),
        name="ada_mod",
    )(cond, w_ada, b_ada.reshape(DEPTH, 1, N_MOD * D))


def _wfold_body(cc_ref, sc_ref, w_ref, o_ref):
    w = w_ref[...]
    o_ref[0:F_CH, :] = jnp.dot(cc_ref[...], w, preferred_element_type=F32,
                               precision=lax.Precision.HIGHEST)
    o_ref[F_CH:2 * F_CH, :] = jnp.dot(sc_ref[...], w, preferred_element_type=F32,
                                      precision=lax.Precision.HIGHEST)


def _wfold(w_fnet):
    ang = 2.0 * np.pi * np.outer(np.arange(F_CH), np.arange(F_CH)) / F_CH
    cc = jnp.asarray(np.cos(ang), F32)
    sc = jnp.asarray(np.sin(ang), F32)
    cspec = pl.BlockSpec((F_CH, F_CH), lambda l, g: (0, 0))
    return pl.pallas_call(
        _wfold_body,
        grid=(DEPTH, F_GROUPS),
        in_specs=[cspec, cspec, pl.BlockSpec((None, None, F_CH, F_CH), lambda l, g: (l, g, 0, 0))],
        out_specs=pl.BlockSpec((None, None, 2 * F_CH, F_CH), lambda l, g: (l, g, 0, 0)),
        out_shape=jax.ShapeDtypeStruct((DEPTH, F_GROUPS, 2 * F_CH, F_CH), F32),
        compiler_params=_cparams(("arbitrary", "arbitrary")),
        name="fnet_fold",
    )(cc, sc, w_fnet)


def _group_of(i, group0, tiles_per_group):
    if tiles_per_group is None:
        return group0
    return group0 + lax.div(i, jnp.int32(tiles_per_group))


def _inproj_body(x_ref, sh_ref, sc_ref, g_ref, w_ref, *rest, rope, cache_layer, group0, tiles_per_group):
    if rope:
        cos_ref, sin_ref, qkv_ref, f_ref = rest
    else:
        qkv_ref, f_ref, k32_ref, v32_ref = rest[-4:]
    grp = _group_of(pl.program_id(0), group0, tiles_per_group)
    sh = sh_ref[pl.ds(grp, 1), :]
    sc = sc_ref[pl.ds(grp, 1), :]
    hn = _rms(x_ref[...], g_ref[...]) * (1.0 + sc) + sh
    proj = jnp.dot(hn.astype(BF16), w_ref[...], preferred_element_type=F32)
    qk = proj[:, :2 * ATTN_W]
    v = proj[:, 2 * ATTN_W:QKV_W]
    if rope:
        cos = jnp.tile(cos_ref[...], (1, 2 * ATTN_W // HEAD_W))
        sin = jnp.tile(sin_ref[...], (1, 2 * ATTN_W // HEAD_W))
        lane = lax.broadcasted_iota(jnp.int32, qk.shape, 1)
        low = (lane % ROPE_AXIS_W) < ROPE_HALF
        rot = jnp.where(low, pltpu.roll(qk, 2 * ATTN_W - ROPE_HALF, 1), pltpu.roll(qk, ROPE_HALF, 1))
        qk = qk * cos + rot * sin
    else:
        for ref, val in ((k32_ref, qk[:, ATTN_W:]), (v32_ref, v)):
            val = val.reshape(TM // T_CTX, T_CTX, ATTN_W)
            if cache_layer == 0:
                ref[:, 0] = val
                ref[:, 1:] = jnp.zeros((TM // T_CTX, DEPTH - 1, T_CTX, ATTN_W), F32)
            else:
                ref[...] = val
    qkv_ref[:, 0:ATTN_W] = (qk[:, :ATTN_W] * (QK_DIM ** -0.5 * math.log2(math.e))).astype(BF16)
    qkv_ref[:, ATTN_W:2 * ATTN_W] = qk[:, ATTN_W:].astype(BF16)
    qkv_ref[:, 2 * ATTN_W:] = v.astype(BF16)
    f_ref[...] = proj[:, QKV_W:]


def _inproj(x, mod, layer, g_pre, w_in_bf, rope_tabs, group0, tiles_per_group, cache=None):
    rows = x.shape[0]
    rope = rope_tabs is not None
    aliases = {}
    row = lambda i: (i, 0)
    const = lambda i: (0, 0)
    in_specs = [pl.BlockSpec((TM, D), row),
                pl.BlockSpec((None, N_GROUPS_PAD, D), lambda i: (layer, 0, 0)),
                pl.BlockSpec((None, N_GROUPS_PAD, D), lambda i: (layer, 0, 1)),
                pl.BlockSpec((1, D), const),
                pl.BlockSpec((D, IN_W), const)]
    args = [x, mod, mod, g_pre.reshape(1, D), w_in_bf]
    out_specs = [pl.BlockSpec((TM, QKV_W), row), pl.BlockSpec((TM, FOUR_W), row)]
    out_shape = [jax.ShapeDtypeStruct((rows, QKV_W), BF16), jax.ShapeDtypeStruct((rows, FOUR_W), F32)]
    if rope:
        tiles_per_seq = T_LAT // TM
        tab = pl.BlockSpec((TM, HEAD_W), lambda i: (i % tiles_per_seq, 0))
        in_specs += [tab, tab]
        args += list(rope_tabs)
    else:
        nb = TM // T_CTX
        if cache is None:
            cache_spec = pl.BlockSpec((nb, DEPTH, T_CTX, ATTN_W), lambda i: (i, 0, 0, 0))
        else:
            cache_spec = pl.BlockSpec((nb, None, T_CTX, ATTN_W), lambda i: (i, layer, 0, 0))
            aliases = {len(args): 2, len(args) + 1: 3}
            in_specs += [pl.BlockSpec(memory_space=pl.ANY)] * 2
            args += list(cache)
        out_specs += [cache_spec] * 2
        out_shape += [jax.ShapeDtypeStruct((N_CTX_B, DEPTH, T_CTX, ATTN_W), F32)] * 2
    return pl.pallas_call(
        functools.partial(_inproj_body, rope=rope, cache_layer=None if rope else (0 if cache is None else layer),
                          group0=group0, tiles_per_group=tiles_per_group),
        grid=(rows // TM,),
        in_specs=in_specs, out_specs=out_specs, out_shape=out_shape,
        input_output_aliases=aliases,
        compiler_params=_cparams(("arbitrary",)),
        name="inproj_lat" if rope else "inproj_ctx",
    )(*args)


def _rope_tables():
    half = QK_DIM // 2
    inv = 1.0 / (ROPE_THETA ** (np.arange(0, half, 2, dtype=np.float64) / half))
    pos = np.arange(T_LAT)
    def tab(p):
        ang = p[:, None].astype(np.float64) * inv[None, :]
        return np.concatenate([ang, ang], axis=-1)
    ang = np.concatenate([tab(pos // GRID_W), tab(pos % GRID_W)], axis=-1)
    sign = np.where((np.arange(QK_DIM) % ROPE_AXIS_W) < ROPE_HALF, -1.0, 1.0)
    cos = np.tile(np.cos(ang), (1, 2))
    sin = np.tile(np.sin(ang) * sign[None, :], (1, 2))
    return jnp.asarray(cos, F32), jnp.asarray(sin, F32)


def _lam(lam_ref, lam_init):
    lp = lam_ref[...]
    return (jnp.exp(jnp.sum(lp[0:1] * lp[1:2], keepdims=True))
            - jnp.exp(jnp.sum(lp[2:3] * lp[3:4], keepdims=True)) + lam_init)


def _stack_maps(q):
    lane = lax.broadcasted_iota(jnp.int32, q.shape, 1)
    zero = jnp.zeros_like(q)
    return jnp.concatenate([jnp.where(lane < QK_DIM, q, zero), jnp.where(lane >= QK_DIM, q, zero)], axis=0)


def _softmax_step(qq, kb, vb, carry):
    m, l, acc = carry
    s = lax.dot_general(qq, kb, (((1,), (1,)), ((), ())), preferred_element_type=F32)
    m_new = jnp.maximum(m, jnp.max(s, axis=-1, keepdims=True))
    alpha = jnp.exp2(m - m_new)
    p = jnp.exp2(s - m_new)
    l = alpha * l + jnp.sum(p, axis=-1, keepdims=True)
    acc = alpha * acc + jnp.dot(p.astype(BF16), vb, preferred_element_type=F32)
    return m_new, l, acc


def _softmax_init(rows):
    return (jnp.full((rows, 1), -jnp.inf, F32), jnp.zeros((rows, 1), F32), jnp.zeros((rows, HEAD_W), F32))


def _diff_out(carry, tq, lam, gs, lam_init):
    _, l, acc = carry
    o = acc / l
    a = o[:tq] - lam * o[tq:]
    return _rms(a, gs) * (1.0 - lam_init)


def _attn_ctx_body(lam_ref, gs_ref, q_ref, k_ref, v_ref, o_ref, *, lam_init):
    lam = _lam(lam_ref, lam_init)
    for b in range(CTX_PER_STEP):
        for h in range(N_HEADS):
            sl = slice(h * HEAD_W, (h + 1) * HEAD_W)
            carry = _softmax_step(_stack_maps(q_ref[b, :, sl]), k_ref[b, :, sl], v_ref[b, :, sl],
                                  _softmax_init(2 * T_CTX))
            o_ref[b, :, sl] = _diff_out(carry, T_CTX, lam, gs_ref[...], lam_init).astype(o_ref.dtype)


def _attn_ctx(qkv, lam_params, g_subln, layer, lam_init):
    qkv3 = qkv.reshape(N_CTX_B, T_CTX, QKV_W)
    blk = lambda part: pl.BlockSpec((CTX_PER_STEP, T_CTX, ATTN_W), lambda b: (b, 0, part))
    return pl.pallas_call(
        functools.partial(_attn_ctx_body, lam_init=lam_init),
        grid=(N_CTX_B // CTX_PER_STEP,),
        in_specs=[pl.BlockSpec((None, 4, QK_DIM), lambda b: (layer, 0, 0)),
                  pl.BlockSpec((None, 1, HEAD_W), lambda b: (layer, 0, 0)),
                  blk(0), blk(1), blk(2)],
        out_specs=pl.BlockSpec((CTX_PER_STEP, T_CTX, ATTN_W), lambda b: (b, 0, 0)),
        out_shape=jax.ShapeDtypeStruct((N_CTX_B, T_CTX, ATTN_W), BF16),
        compiler_params=_cparams(("arbitrary",)),
        name="attn_ctx",
    )(lam_params, g_subln.reshape(DEPTH, 1, HEAD_W), qkv3, qkv3, qkv3)


def _attn_lat_body(lam_ref, gs_ref, q_ref, kc_ref, vc_ref, kl_ref, vl_ref, o_ref, kcat, vcat, *, lam_init):
    @pl.when(pl.program_id(2) == 0)
    def _():
        kcat[0:T_PAST, :] = kc_ref[...]
        kcat[T_PAST:, :] = kl_ref[...]
        vcat[0:T_PAST, :] = vc_ref[...]
        vcat[T_PAST:, :] = vl_ref[...]

    lam = _lam(lam_ref, lam_init)
    for sub in range(Q_SUB):
        rows = slice(sub * TQ, (sub + 1) * TQ)
        qq = _stack_maps(q_ref[rows, :])

        def body(c, carry, qq=qq):
            start = pl.multiple_of(c * TK, TK)
            return _softmax_step(qq, kcat[pl.ds(start, TK), :], vcat[pl.ds(start, TK), :], carry)

        carry = lax.fori_loop(0, (T_PAST + T_LAT) // TK, body, _softmax_init(2 * TQ), unroll=ATTN_UNROLL)
        o_ref[rows, :] = _diff_out(carry, TQ, lam, gs_ref[...], lam_init).astype(o_ref.dtype)


def _attn_lat(qkv, ck, cv, lam_params, g_subln, layer, lam_init):
    qkv3 = qkv.reshape(N_LAT_B, T_LAT, QKV_W)
    full = lambda off: pl.BlockSpec((None, T_LAT, HEAD_W), lambda b, h, i: (b, 0, off + h))
    past = pl.BlockSpec((None, T_PAST, HEAD_W), lambda b, h, i: (b, 0, h))
    return pl.pallas_call(
        functools.partial(_attn_lat_body, lam_init=lam_init),
        grid=(N_LAT_B, N_HEADS, T_LAT // (Q_SUB * TQ)),
        in_specs=[pl.BlockSpec((None, 4, QK_DIM), lambda b, h, i: (layer, 0, 0)),
                  pl.BlockSpec((None, 1, HEAD_W), lambda b, h, i: (layer, 0, 0)),
                  pl.BlockSpec((None, Q_SUB * TQ, HEAD_W), lambda b, h, i: (b, i, h)),
                  past, past, full(N_HEADS), full(2 * N_HEADS)],
        out_specs=pl.BlockSpec((None, Q_SUB * TQ, HEAD_W), lambda b, h, i: (b, i, h)),
        out_shape=jax.ShapeDtypeStruct((N_LAT_B, T_LAT, ATTN_W), BF16),
        scratch_shapes=[pltpu.VMEM((T_PAST + T_LAT, HEAD_W), BF16)] * 2,
        compiler_params=_cparams(("arbitrary", "arbitrary", "arbitrary")),
        name="attn_lat",
    )(lam_params, g_subln.reshape(DEPTH, 1, HEAD_W), qkv3, ck, cv, qkv3, qkv3)


def _fold_groups(ur, ui, wf_ref, scale):
    outs = []
    for g in range(F_GROUPS):
        sl = slice(g * F_CH, (g + 1) * F_CH)
        lhs = jnp.concatenate([ur[:, sl], ui[:, sl]], axis=1).astype(BF16)
        outs.append(jnp.dot(lhs, wf_ref[g], preferred_element_type=F32))
    return jnp.concatenate(outs, axis=1) * scale


def _four_ctx_body(f_ref, dft_ref, wf_ref, o_ref):
    for b in range(CTX_PER_STEP):
        u = jnp.dot(dft_ref[...], f_ref[b].astype(BF16), preferred_element_type=F32)
        o_ref[b] = _fold_groups(u[:T_CTX], u[T_CTX:], wf_ref, 1.0 / math.sqrt(T_CTX * F_CH)).astype(o_ref.dtype)


def _four_ctx(f, wfold_bf, layer):
    ang = 2.0 * np.pi * np.outer(np.arange(T_CTX), np.arange(T_CTX)) / T_CTX
    dft = jnp.asarray(np.concatenate([np.cos(ang), -np.sin(ang)], axis=0), F32).astype(BF16)
    return pl.pallas_call(
        _four_ctx_body,
        grid=(N_CTX_B // CTX_PER_STEP,),
        in_specs=[pl.BlockSpec((CTX_PER_STEP, T_CTX, FOUR_W), lambda b: (b, 0, 0)),
                  pl.BlockSpec((2 * T_CTX, T_CTX), lambda b: (0, 0)),
                  pl.BlockSpec((None, F_GROUPS, 2 * F_CH, F_CH), lambda b: (layer, 0, 0, 0))],
        out_specs=pl.BlockSpec((CTX_PER_STEP, T_CTX, FOUR_W), lambda b: (b, 0, 0)),
        out_shape=jax.ShapeDtypeStruct((N_CTX_B, T_CTX, FOUR_W), BF16),
        compiler_params=_cparams(("arbitrary",)),
        name="fourier_ctx",
    )(f.reshape(N_CTX_B, T_CTX, FOUR_W), dft, wfold_bf)


def _fft_body(x_ref, ma_ref, twc_ref, tws_ref, mc_ref, ms_ref, wf_ref, o_ref, h_ref):
    nb = FFT_R * FFT_S
    nj = FFT_R // FFT_S
    j = pl.program_id(1)

    @pl.when(j < nj)
    def _():
        x = x_ref[...].reshape(nb, FOUR_W).astype(BF16)
        g = jnp.dot(ma_ref[...], x, preferred_element_type=F32)
        gr, gi = g[:nb], g[nb:]
        twc = jnp.tile(twc_ref[...], (1, FOUR_W // LANE))
        tws = jnp.tile(tws_ref[...], (1, FOUR_W // LANE))
        lo = pl.ds(pl.multiple_of(j * FFT_S, FFT_S), FFT_S)
        h_ref[lo, :, 0:FOUR_W] = (gr * twc + gi * tws).reshape(FFT_S, FFT_R, FOUR_W)
        h_ref[lo, :, FOUR_W:] = (gi * twc - gr * tws).reshape(FFT_S, FFT_R, FOUR_W)

    @pl.when(j >= nj)
    def _():
        hi = pl.ds(pl.multiple_of((j - nj) * FFT_S, FFT_S), FFT_S)
        h = h_ref[:, hi, :].reshape(nb, 2 * FOUR_W).astype(BF16)
        p = jnp.dot(mc_ref[...], h, preferred_element_type=F32)
        q = jnp.dot(ms_ref[...], h, preferred_element_type=F32)
        ur = p[:, :FOUR_W] + q[:, FOUR_W:]
        ui = p[:, FOUR_W:] - q[:, :FOUR_W]
        out = _fold_groups(ur, ui, wf_ref, 1.0 / math.sqrt(T_LAT * F_CH))
        o_ref[...] = out.reshape(FFT_R, FFT_S, FOUR_W)


def _fft_consts():
    r, s = FFT_R, FFT_S
    nb = r * s
    ang = 2.0 * np.pi * np.outer(np.arange(r), np.arange(r)) / r
    c, sn = np.cos(ang), np.sin(ang)
    eye = np.eye(s)
    ma = np.concatenate([np.einsum('pb,ts->tpbs', c, eye).reshape(nb, nb),
                         np.einsum('pb,ts->tpbs', -sn, eye).reshape(nb, nb)], axis=0)
    mbc = np.einsum('pa,ts->ptas', c, eye).reshape(nb, nb)
    mbs = np.einsum('pa,ts->ptas', sn, eye).reshape(nb, nb)
    tw = 2.0 * np.pi * np.outer(np.arange(r), np.arange(r)).reshape(-1) / (r * r)
    twc = np.broadcast_to(np.cos(tw)[:, None], (r * r, LANE))
    tws = np.broadcast_to(np.sin(tw)[:, None], (r * r, LANE))
    bf = lambda a: jnp.asarray(a, F32).astype(BF16)
    return bf(ma), bf(mbc), bf(mbs), jnp.asarray(twc, F32), jnp.asarray(tws, F32)


def _four_lat(f, wfold_bf, layer, consts):
    ma, mbc, mbs, twc, tws = consts
    r, s = FFT_R, FFT_S
    nb = r * s
    nj = r // s
    f4 = f.reshape(N_LAT_B, r, r, FOUR_W)
    a_step = lambda j: jnp.minimum(j, nj - 1)
    b_step = lambda j: jnp.maximum(j - nj, 0)
    const = lambda b, j: (0, 0)
    out = pl.pallas_call(
        _fft_body,
        grid=(N_LAT_B, 2 * nj),
        in_specs=[pl.BlockSpec((None, r, s, FOUR_W), lambda b, j: (b, 0, a_step(j), 0)),
                  pl.BlockSpec((2 * nb, nb), const),
                  pl.BlockSpec((nb, LANE), lambda b, j: (a_step(j), 0)),
                  pl.BlockSpec((nb, LANE), lambda b, j: (a_step(j), 0)),
                  pl.BlockSpec((nb, nb), const),
                  pl.BlockSpec((nb, nb), const),
                  pl.BlockSpec((None, F_GROUPS, 2 * F_CH, F_CH), lambda b, j: (layer, 0, 0, 0))],
        out_specs=pl.BlockSpec((None, r, s, FOUR_W), lambda b, j: (b, 0, b_step(j), 0)),
        out_shape=jax.ShapeDtypeStruct((N_LAT_B, r, r, FOUR_W), F32),
        scratch_shapes=[pltpu.VMEM((r, r, 2 * FOUR_W), F32)],
        compiler_params=_cparams(("arbitrary", "arbitrary")),
        name="fft_lat",
    )(f4, ma, twc, tws, mbc, mbs, wfold_bf)
    return out.reshape(N_LAT_B * T_LAT, FOUR_W)


def _outproj_body(a_ref, f_ref, x_ref, gt_ref, sh_ref, sc_ref, gpost_ref, gpre_ref, wo_ref, *rest,
                  moe, group0, tiles_per_group):
    if moe:
        wr_ref, x1_ref, hn_ref, gates_ref, sel_ref, len_ref = rest
    else:
        x1_ref, hn_ref = rest
    grp = _group_of(pl.program_id(0), group0, tiles_per_group)
    gt, sh, sc = gt_ref[pl.ds(grp, 1), :], sh_ref[pl.ds(grp, 1), :], sc_ref[pl.ds(grp, 1), :]
    for half in range(TM_OUT // TM):
        rows = slice(half * TM, (half + 1) * TM)
        mix_in = jnp.concatenate([a_ref[rows, :], f_ref[rows, :].astype(BF16)], axis=1)
        mixed = jnp.dot(mix_in, wo_ref[...], preferred_element_type=F32)
        x1 = x_ref[rows, :] + gt * _rms(mixed, gpost_ref[...])
        x1_ref[rows, :] = x1
        hn = _rms(x1, gpre_ref[...]) * (1.0 + sc) + sh
        hn_ref[rows, :] = hn.astype(hn_ref.dtype)
        if moe:
            gates, sel = _route_top2(hn, wr_ref[...])
            gates_ref[rows, :] = gates
            sel_ref[rows, :] = sel.astype(sel_ref.dtype)
            group_rows = jnp.ceil(jnp.sum(sel, axis=0, keepdims=True) * (1.0 / GROUP)) * GROUP
            len_ref[half] = jnp.broadcast_to(group_rows, (N_GROUPS_PAD, E_PAD))


def _route_top2(hn, wr):
    wr_hi = wr.astype(BF16)
    wr_lo = (wr - wr_hi.astype(F32)).astype(BF16)
    hn_hi = hn.astype(BF16)
    hn_lo = (hn - hn_hi.astype(F32)).astype(BF16)
    hh = jnp.dot(hn_hi, jnp.concatenate([wr_hi, wr_lo], axis=1), preferred_element_type=F32)
    logits = hh[:, :E_PAD] + hh[:, E_PAD:] + jnp.dot(hn_lo, wr_hi, preferred_element_type=F32)
    lane = lax.broadcasted_iota(jnp.int32, logits.shape, 1).astype(F32)
    neg = jnp.float32(-jnp.inf)
    logits = jnp.where(lane < float(N_EXPERTS), logits, neg)
    m1 = jnp.max(logits, axis=-1, keepdims=True)
    i1 = jnp.min(jnp.where(logits == m1, lane, float(E_PAD)), axis=-1, keepdims=True)
    rest_l = jnp.where(lane == i1, neg, logits)
    m2 = jnp.max(rest_l, axis=-1, keepdims=True)
    i2 = jnp.min(jnp.where(rest_l == m2, lane, float(E_PAD)), axis=-1, keepdims=True)
    e2 = jnp.exp(m2 - m1)
    w1 = 1.0 / (1.0 + e2)
    w2 = e2 / (1.0 + e2)
    gates = jnp.where(lane == i1, w1, 0.0) + jnp.where(lane == i2, w2, 0.0)
    sel = jnp.where((lane == i1) | (lane == i2), 1.0, 0.0)
    return gates, sel


def _outproj(attn_o, four_o, x, mod, layer, g_post, g_ffn_pre, w_out_bf, w_router_pad, group0, tiles_per_group):
    rows = x.shape[0]
    moe = w_router_pad is not None
    row = lambda i: (i, 0)
    const = lambda i: (0, 0)
    modspec = lambda k: pl.BlockSpec((None, N_GROUPS_PAD, D), lambda i: (layer, 0, k))
    blocks_per_group = None if tiles_per_group is None else tiles_per_group * TM // TM_OUT
    in_specs = [pl.BlockSpec((TM_OUT, ATTN_W), row), pl.BlockSpec((TM_OUT, FOUR_W), row),
                pl.BlockSpec((TM_OUT, D), row),
                modspec(2), modspec(3), modspec(4),
                pl.BlockSpec((1, D), const), pl.BlockSpec((1, D), const), pl.BlockSpec((D, D), const)]
    args = [attn_o, four_o, x, mod, mod, mod, g_post.reshape(1, D), g_ffn_pre.reshape(1, D), w_out_bf]
    out_specs = [pl.BlockSpec((TM_OUT, D), row), pl.BlockSpec((TM_OUT, D), row)]
    out_shape = [jax.ShapeDtypeStruct((rows, D), F32), jax.ShapeDtypeStruct((rows, D), BF16)]
    if moe:
        in_specs.append(pl.BlockSpec((D, E_PAD), const))
        args.append(w_router_pad)
        out_specs += [pl.BlockSpec((TM_OUT, E_PAD), row)] * 2
        out_specs.append(pl.BlockSpec((TM_OUT // TM, N_GROUPS_PAD, E_PAD), lambda i: (i, 0, 0)))
        out_shape += [jax.ShapeDtypeStruct((rows, E_PAD), F32), jax.ShapeDtypeStruct((rows, E_PAD), BF16),
                      jax.ShapeDtypeStruct((rows // TM, N_GROUPS_PAD, E_PAD), F32)]
    return pl.pallas_call(
        functools.partial(_outproj_body, moe=moe, group0=group0, tiles_per_group=blocks_per_group),
        grid=(rows // TM_OUT,),
        in_specs=in_specs, out_specs=out_specs, out_shape=out_shape,
        compiler_params=_cparams(("arbitrary",)),
        name="outproj_moe" if moe else "outproj",
    )(*args)


def _swiglu(hn, wg, wu, wd):
    g = jnp.dot(hn, wg, preferred_element_type=F32)
    u = jnp.dot(hn, wu, preferred_element_type=F32)
    return jnp.dot((_silu(g) * u).astype(BF16), wd, preferred_element_type=F32)


def _ffn_dense_body(hn_ref, x1_ref, gt_ref, gpost_ref, wg_ref, wu_ref, wd_ref, o_ref, *, group0, tiles_per_group):
    grp = _group_of(pl.program_id(0), group0, tiles_per_group)
    ff = _swiglu(hn_ref[...], wg_ref[...], wu_ref[...], wd_ref[...])
    o_ref[...] = x1_ref[...] + gt_ref[pl.ds(grp, 1), :] * _rms(ff, gpost_ref[...])


def _ffn_dense(hn, x1, mod, layer, g_post, wg, wu, wd, group0, tiles_per_group):
    rows = x1.shape[0]
    row = lambda i: (i, 0)
    const = lambda i: (0, 0)
    resident = lambda shape: pl.BlockSpec(shape, const, pipeline_mode=pl.Buffered(1))
    return pl.pallas_call(
        functools.partial(_ffn_dense_body, group0=group0, tiles_per_group=tiles_per_group),
        grid=(rows // TM,),
        in_specs=[pl.BlockSpec((TM, D), row), pl.BlockSpec((TM, D), row),
                  pl.BlockSpec((None, N_GROUPS_PAD, D), lambda i: (layer, 0, 5)),
                  pl.BlockSpec((1, D), const),
                  resident((D, D_FF)), resident((D, D_FF)), resident((D_FF, D))],
        out_specs=pl.BlockSpec((TM, D), row),
        out_shape=jax.ShapeDtypeStruct((rows, D), F32),
        compiler_params=_cparams(("arbitrary",)),
        name="ffn_dense",
    )(hn, x1, mod, g_post.reshape(1, D), wg, wu, wd)


def _experts_body(te_ref, tn_ref, tnext_ref, xs_ref, wg_hbm, wu_hbm, wd_hbm, ys_ref, w32, wg_bf, wu_bf, wd_bf, sem):
    i = pl.program_id(0)
    e = te_ref[i]
    fresh = (i == 0) | (e != te_ref[jnp.maximum(i - 1, 0)])

    def fetch(expert, act):
        for j, hbm in enumerate((wg_hbm, wu_hbm, wd_hbm)):
            act(pltpu.make_async_copy(hbm.at[expert], w32.at[j], sem))

    @pl.when(i == 0)
    def _():
        fetch(e, lambda c: c.start())

    @pl.when(fresh)
    def _():
        fetch(e, lambda c: c.wait())
        wg_bf[...] = w32[0].astype(BF16)
        wu_bf[...] = w32[1].astype(BF16)
        wd_bf[...] = w32[2].astype(BF16)
        nxt = tnext_ref[i]

        @pl.when(nxt != e)
        def _():
            fetch(nxt, lambda c: c.start())

    n_real = tn_ref[i]

    @pl.when(n_real > 0)
    def _():
        ys_ref[...] = _swiglu(xs_ref[...].astype(BF16), wg_bf[...], wu_bf[...], wd_bf[...])

    @pl.when(n_real == 0)
    def _():
        ys_ref[...] = jnp.zeros_like(ys_ref)


def _experts(xs, tile_e, tile_n, tile_next, eg, eu, ed):
    assert D == D_FF_E
    tile = lambda i, te, tn, tx: (i, 0)
    return pl.pallas_call(
        _experts_body,
        grid_spec=pltpu.PrefetchScalarGridSpec(
            num_scalar_prefetch=3,
            grid=(N_TILES,),
            in_specs=[pl.BlockSpec((TMOE, D), tile)] + [pl.BlockSpec(memory_space=pl.ANY)] * 3,
            out_specs=pl.BlockSpec((TMOE, D), tile),
            scratch_shapes=[pltpu.VMEM((3, D, D), F32),
                            pltpu.VMEM((D, D_FF_E), BF16), pltpu.VMEM((D, D_FF_E), BF16),
                            pltpu.VMEM((D_FF_E, D), BF16), pltpu.SemaphoreType.DMA]),
        out_shape=jax.ShapeDtypeStruct((NP, D), F32),
        compiler_params=_cparams(("arbitrary",)),
        name="moe_experts",
    )(tile_e, tile_n, tile_next, xs, eg, eu, ed)


def _route_body(*refs, firsts):
    ns = len(firsts)
    gates_refs, sel_refs, len_refs = refs[:ns], refs[ns:2 * ns], refs[2 * ns:3 * ns]
    ltri_ref, w_ref, rho_ref, tmeta_ref, meta_ref, off_ref, run_ref = refs[3 * ns:]
    t = pl.program_id(0)

    def this_stream(stream_refs):
        val = stream_refs[0][...]
        for k in range(1, ns):
            val = jnp.where(t >= firsts[k], stream_refs[k][...], val)
        return val

    sel = this_stream(sel_refs)
    group_rows = jnp.ceil(jnp.sum(sel.astype(F32), axis=0, keepdims=True) * (1.0 / GROUP)) * GROUP

    @pl.when(t == 0)
    def _():
        cnt = sum(jnp.sum(r[...].reshape(-1, E_PAD), axis=0, keepdims=True) for r in len_refs) * (1.0 / N_GROUPS_PAD)
        off_ref[...] = _segment_meta(cnt, meta_ref)
        run_ref[...] = jnp.zeros_like(run_ref)

    lane1 = lax.broadcasted_iota(jnp.int32, group_rows.shape, 1).astype(F32)
    buf0 = jnp.zeros_like(group_rows)
    running = jnp.zeros((1, 1), F32)
    for e in range(N_EXPERTS):
        pick = lane1 == float(e)
        buf0 = jnp.where(pick, running, buf0)
        running = running + jnp.sum(jnp.where(pick, group_rows, 0.0), axis=-1, keepdims=True)
    slot0 = off_ref[...] + run_ref[...]
    run_ref[...] += group_rows
    row8 = lax.broadcasted_iota(jnp.int32, tmeta_ref.shape, 0)
    tmeta_ref[...] = jnp.where(row8 == 0, slot0, jnp.where(row8 == 1, group_rows,
                               jnp.where(row8 == 2, buf0, 0.0)))

    earlier = jnp.dot(ltri_ref[...], sel, preferred_element_type=F32)
    rho = earlier + buf0
    chosen = sel > 0
    r_lo = jnp.min(jnp.where(chosen, rho, float(R_TILE)), axis=-1, keepdims=True)
    r_hi = jnp.max(jnp.where(chosen, rho, -1.0), axis=-1, keepdims=True)
    g = this_stream(gates_refs)
    w_lo = jnp.sum(jnp.where(chosen & (rho == r_lo), g, 0.0), axis=-1, keepdims=True)
    w_hi = jnp.sum(jnp.where(chosen & (rho == r_hi), g, 0.0), axis=-1, keepdims=True)
    lane = lax.broadcasted_iota(jnp.int32, rho.shape, 1)
    w_ref[...] = jnp.where(lane == 2, w_lo, jnp.where(lane == 3, w_hi,
                           jnp.where(lane == 4, r_lo, jnp.where(lane == 5, r_hi, 0.0))))
    cols = jnp.where(lane == 0, r_lo, jnp.where(lane == 1, r_hi, 0.0))
    rho_ref[...] = jnp.transpose(cols)[0:N_GROUPS_PAD, :]


def _segment_meta(cnt, meta_ref):
    padded = jnp.ceil(cnt * (1.0 / TMOE)) * TMOE
    lane = lax.broadcasted_iota(jnp.int32, cnt.shape, 1).astype(F32)
    pad_of = lambda e: jnp.sum(jnp.where(lane == float(e), padded, 0.0), axis=-1, keepdims=True)
    following = [None] * N_EXPERTS
    seen = jnp.full((1, 1), -1.0, F32)
    for e in reversed(range(N_EXPERTS)):
        following[e] = seen
        seen = jnp.where(pad_of(e) > 0.0, float(e), seen)
    start = lane * TMOE
    off = jnp.zeros_like(cnt)
    tile_e = jnp.zeros_like(cnt)
    tile_n = jnp.zeros_like(cnt)
    tile_next = jnp.zeros_like(cnt)
    running = jnp.zeros((1, 1), F32)
    last_e = jnp.zeros((1, 1), F32)
    for e in range(N_EXPERTS):
        pick = lane == float(e)
        cnt_e = jnp.sum(jnp.where(pick, cnt, 0.0), axis=-1, keepdims=True)
        pad_e = pad_of(e)
        off = jnp.where(pick, running, off)
        in_seg = (start >= running) & (start < running + pad_e)
        tile_e = jnp.where(in_seg, float(e), tile_e)
        tile_n = jnp.where(in_seg, jnp.clip(cnt_e - (start - running), 0.0, float(TMOE)), tile_n)
        tile_next = jnp.where(in_seg, jnp.where(following[e] >= 0.0, following[e], float(e)), tile_next)
        last_e = jnp.where(pad_e > 0.0, float(e), last_e)
        running = running + pad_e
    tile_e = jnp.where(start >= running, last_e, tile_e)
    tile_next = jnp.where(start >= running, last_e, tile_next)
    row = lax.broadcasted_iota(jnp.int32, meta_ref.shape, 0)
    last_tile = jnp.where(lane < float(N_EXPERTS), off + padded - TMOE, running)
    last_tile = jnp.where((lane < float(N_EXPERTS)) & (padded == 0.0), -1.0, last_tile)
    meta_ref[...] = jnp.where(row == 0, tile_e, jnp.where(row == 1, tile_n,
                              jnp.where(row == 2, last_tile, jnp.where(row == 3, tile_next, 0.0))))
    return off


def _route(gates, sels, lens):
    tiles = [g.shape[0] // TM for g in gates]
    firsts = [int(f) for f in np.cumsum([0] + tiles)]
    n, rows = firsts[-1], firsts[-1] * TM
    ltri = jnp.asarray(np.tril(np.ones((TM, TM)), -1), F32).astype(BF16)
    stream_spec = lambda lo, cnt: pl.BlockSpec((TM, E_PAD), lambda t: (jnp.clip(t - lo, 0, cnt - 1), 0))
    stream_specs = [stream_spec(lo, cnt) for lo, cnt in zip(firsts[:-1], tiles)]
    whole = [pl.BlockSpec(ln.shape, lambda t: (0, 0, 0)) for ln in lens]
    per_tile = lambda t: (t, 0, 0)
    return pl.pallas_call(
        functools.partial(_route_body, firsts=tuple(firsts[:-1])),
        grid=(n,),
        in_specs=stream_specs + stream_specs + whole + [pl.BlockSpec((TM, TM), lambda t: (0, 0))],
        out_specs=[pl.BlockSpec((TM, E_PAD), lambda t: (t, 0)),
                   pl.BlockSpec((None, N_GROUPS_PAD, TM), per_tile),
                   pl.BlockSpec((None, N_GROUPS_PAD, E_PAD), per_tile),
                   pl.BlockSpec((N_GROUPS_PAD, E_PAD), lambda t: (0, 0))],
        out_shape=[jax.ShapeDtypeStruct((rows, E_PAD), F32),
                   jax.ShapeDtypeStruct((n, N_GROUPS_PAD, TM), F32),
                   jax.ShapeDtypeStruct((n, N_GROUPS_PAD, E_PAD), F32),
                   jax.ShapeDtypeStruct((N_GROUPS_PAD, E_PAD), F32)],
        scratch_shapes=[pltpu.VMEM((1, E_PAD), F32)] * 2,
        compiler_params=_cparams(("arbitrary",)),
        name="moe_route",
    )(*gates, *sels, *lens, ltri)


def _group_copies(tmeta_ref, slot_window, buf_window, sem, to_slots, act):
    for e in range(N_EXPERTS):
        slot0, n, buf0 = tmeta_ref[0, e], tmeta_ref[1, e], tmeta_ref[2, e]
        done = jnp.int32(0)
        size = TM
        while size >= GROUP:
            take = (n - done) >= size
            s_at = pl.ds(pl.multiple_of(slot0 + done, GROUP), size)
            b_at = pl.ds(pl.multiple_of(buf0 + done, GROUP), size)
            src, dst = (buf_window(b_at), slot_window(s_at)) if to_slots else (slot_window(s_at), buf_window(b_at))

            @pl.when(take)
            def _(src=src, dst=dst):
                act(pltpu.make_async_copy(src, dst, sem))

            done = done + jnp.where(take, size, 0)
            size //= 2


def _dispatch_body(tmeta_ref, prev_tmeta_ref, pad_ref, rho_ref, *rest, firsts):
    ns = len(firsts)
    hn_refs, (xs_ref, pbuf, zeros, sems) = rest[:ns], rest[ns:]
    i = pl.program_id(0)
    cur = lax.rem(i, 2)
    sem = sems.at[0]

    def zero_fills(act):
        fills = []
        for e in range(N_EXPERTS):
            s = pad_ref[0, e]
            fills.append((s >= 0, pl.multiple_of(jnp.maximum(s, 0), TMOE)))
        for j in range(N_TILES - TOP_K * N_TOKENS // TMOE):
            s = NP - (j + 1) * TMOE
            fills.append((s >= pad_ref[0, N_EXPERTS], s))
        for live, s in fills:
            @pl.when(live)
            def _(s=s):
                act(pltpu.make_async_copy(zeros, xs_ref.at[pl.ds(s, TMOE), :], sem))

    @pl.when(i == 0)
    def _():
        zeros[...] = jnp.zeros_like(zeros)
        zero_fills(lambda c: c.start())

    hn = hn_refs[0][...]
    for k in range(1, ns):
        hn = jnp.where(i >= firsts[k], hn_refs[k][...], hn)
    row = lax.broadcasted_iota(jnp.int32, (R_TILE, TM), 0).astype(F32)
    onehot = jnp.where((row == rho_ref[0:1, :]) | (row == rho_ref[1:2, :]), 1.0, 0.0).astype(BF16)
    pbuf[cur] = jnp.dot(onehot, hn, preferred_element_type=F32)

    @pl.when(i == 0)
    def _():
        zero_fills(lambda c: c.wait())

    def store(meta_ref, half, act):
        _group_copies(meta_ref, lambda at: xs_ref.at[at, :], lambda at: pbuf.at[half, at, :], sems.at[half],
                      True, act)

    store(tmeta_ref, cur, lambda c: c.start())

    @pl.when(i > 0)
    def _():
        store(prev_tmeta_ref, 1 - cur, lambda c: c.wait())

    @pl.when(i + 1 == pl.num_programs(0))
    def _():
        store(tmeta_ref, cur, lambda c: c.wait())


def _dispatch(hns, tmeta, rho, pad_starts):
    tiles = [h.shape[0] // TM for h in hns]
    firsts = [int(f) for f in np.cumsum([0] + tiles)]
    n = firsts[-1]
    stream_spec = lambda lo, cnt: pl.BlockSpec((TM, D), lambda i: (jnp.clip(i - lo, 0, cnt - 1), 0))
    return pl.pallas_call(
        functools.partial(_dispatch_body, firsts=tuple(firsts[:-1])),
        grid=(n,),
        in_specs=[pl.BlockSpec((None, N_GROUPS_PAD, E_PAD), lambda i: (i, 0, 0), memory_space=pltpu.SMEM),
                  pl.BlockSpec((None, N_GROUPS_PAD, E_PAD), lambda i: (jnp.maximum(i - 1, 0), 0, 0),
                               memory_space=pltpu.SMEM),
                  pl.BlockSpec(memory_space=pltpu.SMEM),
                  pl.BlockSpec((None, N_GROUPS_PAD, TM), lambda i: (i, 0, 0))]
                 + [stream_spec(lo, cnt) for lo, cnt in zip(firsts[:-1], tiles)],
        out_specs=pl.BlockSpec(memory_space=pl.ANY),
        out_shape=jax.ShapeDtypeStruct((NP, D), F32),
        scratch_shapes=[pltpu.VMEM((2, R_TILE, D), F32), pltpu.VMEM((TMOE, D), F32),
                        pltpu.SemaphoreType.DMA((2,))],
        compiler_params=_cparams(("arbitrary",)),
        name="moe_dispatch",
    )(tmeta, tmeta, pad_starts, rho, *hns)


def _combine_body(tmeta_ref, next_tmeta_ref, w_ref, x1_ref, gt_ref, gpost_ref, ys_ref, o_ref, ybuf, sems,
                   *, group0, tiles_per_group):
    i = pl.program_id(0)
    grp = _group_of(i, group0, tiles_per_group)
    cur = lax.rem(i, 2)

    def fetch(meta_ref, half, act):
        _group_copies(meta_ref, lambda at: ys_ref.at[at, :], lambda at: ybuf.at[half, at, :], sems.at[half],
                      False, act)

    @pl.when(i == 0)
    def _():
        ybuf[...] = jnp.zeros_like(ybuf)
        fetch(tmeta_ref, 0, lambda c: c.start())

    @pl.when(i + 1 < pl.num_programs(0))
    def _():
        fetch(next_tmeta_ref, 1 - cur, lambda c: c.start())

    fetch(tmeta_ref, cur, lambda c: c.wait())
    w = w_ref[...]
    lane = lax.broadcasted_iota(jnp.int32, w.shape, 1)
    col = lambda j: jnp.sum(jnp.where(lane == j, w, 0.0), axis=-1, keepdims=True)
    w_lo, w_hi, r_lo, r_hi = col(2), col(3), col(4), col(5)
    y = ybuf[cur].astype(BF16)
    pos = lax.broadcasted_iota(jnp.int32, (TM, R_TILE), 1).astype(F32)
    pick = lambda r: jnp.dot(jnp.where(pos == r, 1.0, 0.0).astype(BF16), y, preferred_element_type=F32)
    ff = w_lo * pick(r_lo) + w_hi * pick(r_hi)
    o_ref[...] = x1_ref[...] + gt_ref[pl.ds(grp, 1), :] * _rms(ff, gpost_ref[...])


def _combine(ys, tmeta, slot_w, tile0, x1, mod, layer, g_post, group0, tiles_per_group):
    rows = x1.shape[0]
    n = rows // TM
    row = lambda i: (i, 0)
    return pl.pallas_call(
        functools.partial(_combine_body, group0=group0, tiles_per_group=tiles_per_group),
        grid=(n,),
        in_specs=[pl.BlockSpec((None, N_GROUPS_PAD, E_PAD), lambda i: (tile0 + i, 0, 0), memory_space=pltpu.SMEM),
                  pl.BlockSpec((None, N_GROUPS_PAD, E_PAD), lambda i: (tile0 + jnp.minimum(i + 1, n - 1), 0, 0),
                               memory_space=pltpu.SMEM),
                  pl.BlockSpec((TM, E_PAD), lambda i: (tile0 + i, 0)), pl.BlockSpec((TM, D), row),
                  pl.BlockSpec((None, N_GROUPS_PAD, D), lambda i: (layer, 0, 5)),
                  pl.BlockSpec((1, D), lambda i: (0, 0)),
                  pl.BlockSpec(memory_space=pl.ANY)],
        out_specs=pl.BlockSpec((TM, D), row),
        out_shape=jax.ShapeDtypeStruct((rows, D), F32),
        scratch_shapes=[pltpu.VMEM((2, R_TILE, D), F32), pltpu.SemaphoreType.DMA((2,))],
        compiler_params=_cparams(("arbitrary",)),
        name="moe_combine",
    )(tmeta, tmeta, slot_w, x1, mod, g_post.reshape(1, D), ys)


def kernel(x_prompt, x_sample, cache_k, cache_v, c, c_ctx, g_attn_pre, g_attn_post, g_ffn_pre, g_ffn_post,
           w_ada, b_ada, w_in, lam_params, g_subln, w_fnet, w_out, w_gate, w_up, w_down, w_router,
           e_gate, e_up, e_down):
    assert x_prompt.shape == (N_CTX_B, T_CTX, D) and x_sample.shape == (N_LAT_B, T_LAT, D)
    assert cache_k.shape == (N_LAT_B, DEPTH, T_PAST, N_HEADS, 2, QK_DIM)

    cond = jnp.zeros((N_GROUPS_PAD, D), F32).at[0].set(c_ctx).at[1:1 + N_LAT_B].set(c)
    mod = _ada(cond, w_ada, b_ada)
    wfold_bf = _wfold(w_fnet).astype(BF16)
    rope_tabs = _rope_tables()
    fft_consts = _fft_consts()

    w_in_bf = w_in.astype(BF16)
    w_out_bf = w_out.astype(BF16)
    w_router_pad = jnp.zeros((DEPTH // 2, D, E_PAD), F32).at[:, :, :N_EXPERTS].set(w_router)
    ck_bf = cache_k.reshape(N_LAT_B, DEPTH, T_PAST, ATTN_W).astype(BF16)
    cv_bf = cache_v.reshape(N_LAT_B, DEPTH, T_PAST, ATTN_W).astype(BF16)

    lat_tiles = T_LAT // TM
    streams = [dict(x=x_prompt.reshape(N_CTX_B * T_CTX, D), group0=0, tiles=None, lat=False),
               dict(x=x_sample.reshape(N_LAT_B * T_LAT, D), group0=1, tiles=lat_tiles, lat=True)]
    new_k = new_v = None
    for l in range(DEPTH):
        lam_init = 0.8 - 0.6 * math.exp(-0.3 * l)
        i = l // 2
        routed = []
        for s in streams:
            g0, tiles = s["group0"], s["tiles"]
            if s["lat"]:
                qkv, f = _inproj(s["x"], mod, l, g_attn_pre[l], w_in_bf[l], rope_tabs, g0, tiles)
                attn_o = _attn_lat(qkv, ck_bf[:, l], cv_bf[:, l], lam_params, g_subln, l, lam_init)
                attn_o = attn_o.reshape(N_LAT_B * T_LAT, ATTN_W)
                four_o = _four_lat(f, wfold_bf, l, fft_consts)
            else:
                qkv, f, new_k, new_v = _inproj(s["x"], mod, l, g_attn_pre[l], w_in_bf[l], None, g0, tiles,
                                               cache=None if l == 0 else (new_k, new_v))
                attn_o = _attn_ctx(qkv, lam_params, g_subln, l, lam_init).reshape(N_CTX_B * T_CTX, ATTN_W)
                four_o = _four_ctx(f, wfold_bf, l).reshape(N_CTX_B * T_CTX, FOUR_W)
            if l % 2 == 0:
                x1, hn = _outproj(attn_o, four_o, s["x"], mod, l, g_attn_post[l], g_ffn_pre[l], w_out_bf[l],
                                  None, g0, tiles)
                s["x"] = _ffn_dense(hn, x1, mod, l, g_ffn_post[l], w_gate[i].astype(BF16), w_up[i].astype(BF16),
                                    w_down[i].astype(BF16), g0, tiles)
            else:
                routed.append(_outproj(attn_o, four_o, s["x"], mod, l, g_attn_post[l], g_ffn_pre[l],
                                       w_out_bf[l], w_router_pad[i], g0, tiles))
        if l % 2 == 1:
            slot_w, rho, tmeta, meta = _route(*[[r[k] for r in routed] for k in (2, 3, 4)])
            tmeta = tmeta.astype(jnp.int32)
            tile_e = meta[0, :N_TILES].astype(jnp.int32)
            tile_n = meta[1, :N_TILES].astype(jnp.int32)
            tile_next = meta[3, :N_TILES].astype(jnp.int32)
            pad_starts = meta[2:3, :N_EXPERTS + 1].astype(jnp.int32)
            bounds = np.cumsum([0] + [r[0].shape[0] for r in routed])
            xs = _dispatch([r[1] for r in routed], tmeta, rho, pad_starts)
            ys = _experts(xs, tile_e, tile_n, tile_next, e_gate[i], e_up[i], e_down[i])
            for s, r, lo in zip(streams, routed, bounds[:-1]):
                s["x"] = _combine(ys, tmeta, slot_w, int(lo) // TM, r[0], mod, l, g_ffn_post[l],
                                   s["group0"], s["tiles"])

    y_prompt = streams[0]["x"].reshape(N_CTX_B, T_CTX, D)
    y_sample = streams[1]["x"].reshape(N_LAT_B, T_LAT, D)
    new_cache_k = new_k.reshape(N_CTX_B, DEPTH, T_CTX, N_HEADS, 2, QK_DIM)
    new_cache_v = new_v.reshape(N_CTX_B, DEPTH, T_CTX, N_HEADS, HEAD_W)
    return (y_prompt, y_sample, new_cache_k, new_cache_v)
```

```python
import functools
import math

import numpy as np
import jax
import jax.numpy as jnp
from jax import lax
from jax.experimental import pallas as pl
from jax.experimental.pallas import tpu as pltpu

F32 = jnp.float32
BF16 = jnp.bfloat16

D = 1024
N_CTX_B = 16
T_CTX = 256
N_LAT_B = 2
T_LAT = 4096
T_PAST = 512
DEPTH = 2
GRID_W = 64
N_HEADS = 4
QK_DIM = 64
HEAD_W = 2 * QK_DIM
ATTN_W = N_HEADS * HEAD_W
F_GROUPS = 4
F_CH = 128
FOUR_W = F_GROUPS * F_CH
IN_W = 3 * ATTN_W + FOUR_W
QKV_W = 3 * ATTN_W
D_FF = 2816
N_EXPERTS = 8
D_FF_E = 1024
N_MOD = 6
EPS = 1e-6
ROPE_THETA = 10000.0
N_GROUPS_PAD = 8
LANE = 128
E_PAD = LANE
ROPE_AXIS_W = QK_DIM // 2
ROPE_HALF = ROPE_AXIS_W // 2

TM = 512
CTX_PER_STEP = 4
TM_OUT = 1024
TOP_K = 2
TMOE = 512
N_TOKENS = N_CTX_B * T_CTX + N_LAT_B * T_LAT
GROUP = 8
R_TILE = TOP_K * TM + N_EXPERTS * GROUP
N_TILES = -(-(TOP_K * N_TOKENS + (N_TOKENS // TM) * N_EXPERTS * (GROUP - 1)) // TMOE) + N_EXPERTS
NP = N_TILES * TMOE
TQ = 512
Q_SUB = 2
TK = 1536
ATTN_UNROLL = 3
FFT_R = 64
FFT_S = 8
VMEM_LIMIT = 56 * 1024 * 1024


def _cparams(sem):
    return pltpu.CompilerParams(dimension_semantics=sem, vmem_limit_bytes=VMEM_LIMIT)


def _rms(x, g):
    ms = jnp.mean(x * x, axis=-1, keepdims=True)
    return x * lax.rsqrt(ms + EPS) * g


def _silu(x):
    return x * (1.0 / (1.0 + jnp.exp(-x)))


def _ada_body(c_ref, w_ref, b_ref, o_ref):
    s = _silu(c_ref[...])
    s_hi = s.astype(BF16)
    s_lo = (s - s_hi.astype(F32)).astype(BF16)
    both = jnp.dot(jnp.concatenate([s_hi, s_lo], axis=0), w_ref[...].astype(BF16), preferred_element_type=F32)
    o_ref[...] = both[:N_GROUPS_PAD] + both[N_GROUPS_PAD:] + b_ref[...]


def _ada(cond, w_ada, b_ada):
    tn = 1536
    return pl.pallas_call(
        _ada_body,
        grid=(DEPTH, N_MOD * D // tn),
        in_specs=[pl.BlockSpec((N_GROUPS_PAD, D), lambda l, j: (0, 0)),
                  pl.BlockSpec((None, D, tn), lambda l, j: (l, 0, j)),
                  pl.BlockSpec((None, 1, tn), lambda l, j: (l, 0, j))],
        out_specs=pl.BlockSpec((None, N_GROUPS_PAD, tn), lambda l, j: (l, 0, j)),
        out_shape=jax.ShapeDtypeStruct((DEPTH, N_GROUPS_PAD, N_MOD * D), F32),
        compiler_params=_cparams(("arbitrary", "arbitrary")),
        name="ada_mod",
    )(cond, w_ada, b_ada.reshape(DEPTH, 1, N_MOD * D))


def _wfold_body(cc_ref, sc_ref, w_ref, o_ref):
    w = w_ref[...]
    o_ref[0:F_CH, :] = jnp.dot(cc_ref[...], w, preferred_element_type=F32,
                               precision=lax.Precision.HIGHEST)
    o_ref[F_CH:2 * F_CH, :] = jnp.dot(sc_ref[...], w, preferred_element_type=F32,
                                      precision=lax.Precision.HIGHEST)


def _wfold(w_fnet):
    ang = 2.0 * np.pi * np.outer(np.arange(F_CH), np.arange(F_CH)) / F_CH
    cc = jnp.asarray(np.cos(ang), F32)
    sc = jnp.asarray(np.sin(ang), F32)
    cspec = pl.BlockSpec((F_CH, F_CH), lambda l, g: (0, 0))
    return pl.pallas_call(
        _wfold_body,
        grid=(DEPTH, F_GROUPS),
        in_specs=[cspec, cspec, pl.BlockSpec((None, None, F_CH, F_CH), lambda l, g: (l, g, 0, 0))],
        out_specs=pl.BlockSpec((None, None, 2 * F_CH, F_CH), lambda l, g: (l, g, 0, 0)),
        out_shape=jax.ShapeDtypeStruct((DEPTH, F_GROUPS, 2 * F_CH, F_CH), F32),
        compiler_params=_cparams(("arbitrary", "arbitrary")),
        name="fnet_fold",
    )(cc, sc, w_fnet)


def _group_of(i, group0, tiles_per_group):
    if tiles_per_group is None:
        return group0
    return group0 + lax.div(i, jnp.int32(tiles_per_group))


def _inproj_body(x_ref, sh_ref, sc_ref, g_ref, w_ref, *rest, rope, cache_layer, group0, tiles_per_group):
    if rope:
        cos_ref, sin_ref, qkv_ref, f_ref = rest
    else:
        qkv_ref, f_ref, k32_ref, v32_ref = rest[-4:]
    grp = _group_of(pl.program_id(0), group0, tiles_per_group)
    sh = sh_ref[pl.ds(grp, 1), :]
    sc = sc_ref[pl.ds(grp, 1), :]
    hn = _rms(x_ref[...], g_ref[...]) * (1.0 + sc) + sh
    proj = jnp.dot(hn.astype(BF16), w_ref[...], preferred_element_type=F32)
    qk = proj[:, :2 * ATTN_W]
    v = proj[:, 2 * ATTN_W:QKV_W]
    if rope:
        cos = jnp.tile(cos_ref[...], (1, 2 * ATTN_W // HEAD_W))
        sin = jnp.tile(sin_ref[...], (1, 2 * ATTN_W // HEAD_W))
        lane = lax.broadcasted_iota(jnp.int32, qk.shape, 1)
        low = (lane % ROPE_AXIS_W) < ROPE_HALF
        rot = jnp.where(low, pltpu.roll(qk, 2 * ATTN_W - ROPE_HALF, 1), pltpu.roll(qk, ROPE_HALF, 1))
        qk = qk * cos + rot * sin
    else:
        for ref, val in ((k32_ref, qk[:, ATTN_W:]), (v32_ref, v)):
            val = val.reshape(TM // T_CTX, T_CTX, ATTN_W)
            if cache_layer == 0:
                ref[:, 0] = val
                ref[:, 1:] = jnp.zeros((TM // T_CTX, DEPTH - 1, T_CTX, ATTN_W), F32)
            else:
                ref[...] = val
    qkv_ref[:, 0:ATTN_W] = (qk[:, :ATTN_W] * (QK_DIM ** -0.5 * math.log2(math.e))).astype(BF16)
    qkv_ref[:, ATTN_W:2 * ATTN_W] = qk[:, ATTN_W:].astype(BF16)
    qkv_ref[:, 2 * ATTN_W:] = v.astype(BF16)
    f_ref[...] = proj[:, QKV_W:]


def _inproj(x, mod, layer, g_pre, w_in_bf, rope_tabs, group0, tiles_per_group, cache=None):
    rows = x.shape[0]
    rope = rope_tabs is not None
    aliases = {}
    row = lambda i: (i, 0)
    const = lambda i: (0, 0)
    in_specs = [pl.BlockSpec((TM, D), row),
                pl.BlockSpec((None, N_GROUPS_PAD, D), lambda i: (layer, 0, 0)),
                pl.BlockSpec((None, N_GROUPS_PAD, D), lambda i: (layer, 0, 1)),
                pl.BlockSpec((1, D), const),
                pl.BlockSpec((D, IN_W), const)]
    args = [x, mod, mod, g_pre.reshape(1, D), w_in_bf]
    out_specs = [pl.BlockSpec((TM, QKV_W), row), pl.BlockSpec((TM, FOUR_W), row)]
    out_shape = [jax.ShapeDtypeStruct((rows, QKV_W), BF16), jax.ShapeDtypeStruct((rows, FOUR_W), F32)]
    if rope:
        tiles_per_seq = T_LAT // TM
        tab = pl.BlockSpec((TM, HEAD_W), lambda i: (i % tiles_per_seq, 0))
        in_specs += [tab, tab]
        args += list(rope_tabs)
    else:
        nb = TM // T_CTX
        if cache is None:
            cache_spec = pl.BlockSpec((nb, DEPTH, T_CTX, ATTN_W), lambda i: (i, 0, 0, 0))
        else:
            cache_spec = pl.BlockSpec((nb, None, T_CTX, ATTN_W), lambda i: (i, layer, 0, 0))
            aliases = {len(args): 2, len(args) + 1: 3}
            in_specs += [pl.BlockSpec(memory_space=pl.ANY)] * 2
            args += list(cache)
        out_specs += [cache_spec] * 2
        out_shape += [jax.ShapeDtypeStruct((N_CTX_B, DEPTH, T_CTX, ATTN_W), F32)] * 2
    return pl.pallas_call(
        functools.partial(_inproj_body, rope=rope, cache_layer=None if rope else (0 if cache is None else layer),
                          group0=group0, tiles_per_group=tiles_per_group),
        grid=(rows // TM,),
        in_specs=in_specs, out_specs=out_specs, out_shape=out_shape,
        input_output_aliases=aliases,
        compiler_params=_cparams(("arbitrary",)),
        name="inproj_lat" if rope else "inproj_ctx",
    )(*args)


def _rope_tables():
    half = QK_DIM // 2
    inv = 1.0 / (ROPE_THETA ** (np.arange(0, half, 2, dtype=np.float64) / half))
    pos = np.arange(T_LAT)
    def tab(p):
        ang = p[:, None].astype(np.float64) * inv[None, :]
        return np.concatenate([ang, ang], axis=-1)
    ang = np.concatenate([tab(pos // GRID_W), tab(pos % GRID_W)], axis=-1)
    sign = np.where((np.arange(QK_DIM) % ROPE_AXIS_W) < ROPE_HALF, -1.0, 1.0)
    cos = np.tile(np.cos(ang), (1, 2))
    sin = np.tile(np.sin(ang) * sign[None, :], (1, 2))
    return jnp.asarray(cos, F32), jnp.asarray(sin, F32)


def _lam(lam_ref, lam_init):
    lp = lam_ref[...]
    return (jnp.exp(jnp.sum(lp[0:1] * lp[1:2], keepdims=True))
            - jnp.exp(jnp.sum(lp[2:3] * lp[3:4], keepdims=True)) + lam_init)


def _stack_maps(q):
    lane = lax.broadcasted_iota(jnp.int32, q.shape, 1)
    zero = jnp.zeros_like(q)
    return jnp.concatenate([jnp.where(lane < QK_DIM, q, zero), jnp.where(lane >= QK_DIM, q, zero)], axis=0)


def _softmax_step(qq, kb, vb, carry):
    m, l, acc = carry
    s = lax.dot_general(qq, kb, (((1,), (1,)), ((), ())), preferred_element_type=F32)
    m_new = jnp.maximum(m, jnp.max(s, axis=-1, keepdims=True))
    alpha = jnp.exp2(m - m_new)
    p = jnp.exp2(s - m_new)
    l = alpha * l + jnp.sum(p, axis=-1, keepdims=True)
    acc = alpha * acc + jnp.dot(p.astype(BF16), vb, preferred_element_type=F32)
    return m_new, l, acc


def _softmax_init(rows):
    return (jnp.full((rows, 1), -jnp.inf, F32), jnp.zeros((rows, 1), F32), jnp.zeros((rows, HEAD_W), F32))


def _diff_out(carry, tq, lam, gs, lam_init):
    _, l, acc = carry
    o = acc / l
    a = o[:tq] - lam * o[tq:]
    return _rms(a, gs) * (1.0 - lam_init)


def _attn_ctx_body(lam_ref, gs_ref, q_ref, k_ref, v_ref, o_ref, *, lam_init):
    lam = _lam(lam_ref, lam_init)
    for b in range(CTX_PER_STEP):
        for h in range(N_HEADS):
            sl = slice(h * HEAD_W, (h + 1) * HEAD_W)
            carry = _softmax_step(_stack_maps(q_ref[b, :, sl]), k_ref[b, :, sl], v_ref[b, :, sl],
                                  _softmax_init(2 * T_CTX))
            o_ref[b, :, sl] = _diff_out(carry, T_CTX, lam, gs_ref[...], lam_init).astype(o_ref.dtype)


def _attn_ctx(qkv, lam_params, g_subln, layer, lam_init):
    qkv3 = qkv.reshape(N_CTX_B, T_CTX, QKV_W)
    blk = lambda part: pl.BlockSpec((CTX_PER_STEP, T_CTX, ATTN_W), lambda b: (b, 0, part))
    return pl.pallas_call(
        functools.partial(_attn_ctx_body, lam_init=lam_init),
        grid=(N_CTX_B // CTX_PER_STEP,),
        in_specs=[pl.BlockSpec((None, 4, QK_DIM), lambda b: (layer, 0, 0)),
                  pl.BlockSpec((None, 1, HEAD_W), lambda b: (layer, 0, 0)),
                  blk(0), blk(1), blk(2)],
        out_specs=pl.BlockSpec((CTX_PER_STEP, T_CTX, ATTN_W), lambda b: (b, 0, 0)),
        out_shape=jax.ShapeDtypeStruct((N_CTX_B, T_CTX, ATTN_W), BF16),
        compiler_params=_cparams(("arbitrary",)),
        name="attn_ctx",
    )(lam_params, g_subln.reshape(DEPTH, 1, HEAD_W), qkv3, qkv3, qkv3)


def _attn_lat_body(lam_ref, gs_ref, q_ref, kc_ref, vc_ref, kl_ref, vl_ref, o_ref, kcat, vcat, *, lam_init):
    @pl.when(pl.program_id(2) == 0)
    def _():
        kcat[0:T_PAST, :] = kc_ref[...]
        kcat[T_PAST:, :] = kl_ref[...]
        vcat[0:T_PAST, :] = vc_ref[...]
        vcat[T_PAST:, :] = vl_ref[...]

    lam = _lam(lam_ref, lam_init)
    for sub in range(Q_SUB):
        rows = slice(sub * TQ, (sub + 1) * TQ)
        qq = _stack_maps(q_ref[rows, :])

        def body(c, carry, qq=qq):
            start = pl.multiple_of(c * TK, TK)
            return _softmax_step(qq, kcat[pl.ds(start, TK), :], vcat[pl.ds(start, TK), :], carry)

        carry = lax.fori_loop(0, (T_PAST + T_LAT) // TK, body, _softmax_init(2 * TQ), unroll=ATTN_UNROLL)
        o_ref[rows, :] = _diff_out(carry, TQ, lam, gs_ref[...], lam_init).astype(o_ref.dtype)


def _attn_lat(qkv, ck, cv, lam_params, g_subln, layer, lam_init):
    qkv3 = qkv.reshape(N_LAT_B, T_LAT, QKV_W)
    full = lambda off: pl.BlockSpec((None, T_LAT, HEAD_W), lambda b, h, i: (b, 0, off + h))
    past = pl.BlockSpec((None, T_PAST, HEAD_W), lambda b, h, i: (b, 0, h))
    return pl.pallas_call(
        functools.partial(_attn_lat_body, lam_init=lam_init),
        grid=(N_LAT_B, N_HEADS, T_LAT // (Q_SUB * TQ)),
        in_specs=[pl.BlockSpec((None, 4, QK_DIM), lambda b, h, i: (layer, 0, 0)),
                  pl.BlockSpec((None, 1, HEAD_W), lambda b, h, i: (layer, 0, 0)),
                  pl.BlockSpec((None, Q_SUB * TQ, HEAD_W), lambda b, h, i: (b, i, h)),
                  past, past, full(N_HEADS), full(2 * N_HEADS)],
        out_specs=pl.BlockSpec((None, Q_SUB * TQ, HEAD_W), lambda b, h, i: (b, i, h)),
        out_shape=jax.ShapeDtypeStruct((N_LAT_B, T_LAT, ATTN_W), BF16),
        scratch_shapes=[pltpu.VMEM((T_PAST + T_LAT, HEAD_W), BF16)] * 2,
        compiler_params=_cparams(("arbitrary", "arbitrary", "arbitrary")),
        name="attn_lat",
    )(lam_params, g_subln.reshape(DEPTH, 1, HEAD_W), qkv3, ck, cv, qkv3, qkv3)


def _fold_groups(ur, ui, wf_ref, scale):
    outs = []
    for g in range(F_GROUPS):
        sl = slice(g * F_CH, (g + 1) * F_CH)
        lhs = jnp.concatenate([ur[:, sl], ui[:, sl]], axis=1).astype(BF16)
        outs.append(jnp.dot(lhs, wf_ref[g], preferred_element_type=F32))
    return jnp.concatenate(outs, axis=1) * scale


def _four_ctx_body(f_ref, dft_ref, wf_ref, o_ref):
    for b in range(CTX_PER_STEP):
        u = jnp.dot(dft_ref[...], f_ref[b].astype(BF16), preferred_element_type=F32)
        o_ref[b] = _fold_groups(u[:T_CTX], u[T_CTX:], wf_ref, 1.0 / math.sqrt(T_CTX * F_CH)).astype(o_ref.dtype)


def _four_ctx(f, wfold_bf, layer):
    ang = 2.0 * np.pi * np.outer(np.arange(T_CTX), np.arange(T_CTX)) / T_CTX
    dft = jnp.asarray(np.concatenate([np.cos(ang), -np.sin(ang)], axis=0), F32).astype(BF16)
    return pl.pallas_call(
        _four_ctx_body,
        grid=(N_CTX_B // CTX_PER_STEP,),
        in_specs=[pl.BlockSpec((CTX_PER_STEP, T_CTX, FOUR_W), lambda b: (b, 0, 0)),
                  pl.BlockSpec((2 * T_CTX, T_CTX), lambda b: (0, 0)),
                  pl.BlockSpec((None, F_GROUPS, 2 * F_CH, F_CH), lambda b: (layer, 0, 0, 0))],
        out_specs=pl.BlockSpec((CTX_PER_STEP, T_CTX, FOUR_W), lambda b: (b, 0, 0)),
        out_shape=jax.ShapeDtypeStruct((N_CTX_B, T_CTX, FOUR_W), BF16),
        compiler_params=_cparams(("arbitrary",)),
        name="fourier_ctx",
    )(f.reshape(N_CTX_B, T_CTX, FOUR_W), dft, wfold_bf)


def _fft_body(x_ref, ma_ref, twc_ref, tws_ref, mc_ref, ms_ref, wf_ref, o_ref, h_ref):
    nb = FFT_R * FFT_S
    nj = FFT_R // FFT_S
    j = pl.program_id(1)

    @pl.when(j < nj)
    def _():
        x = x_ref[...].reshape(nb, FOUR_W).astype(BF16)
        g = jnp.dot(ma_ref[...], x, preferred_element_type=F32)
        gr, gi = g[:nb], g[nb:]
        twc = jnp.tile(twc_ref[...], (1, FOUR_W // LANE))
        tws = jnp.tile(tws_ref[...], (1, FOUR_W // LANE))
        lo = pl.ds(pl.multiple_of(j * FFT_S, FFT_S), FFT_S)
        h_ref[lo, :, 0:FOUR_W] = (gr * twc + gi * tws).reshape(FFT_S, FFT_R, FOUR_W)
        h_ref[lo, :, FOUR_W:] = (gi * twc - gr * tws).reshape(FFT_S, FFT_R, FOUR_W)

    @pl.when(j >= nj)
    def _():
        hi = pl.ds(pl.multiple_of((j - nj) * FFT_S, FFT_S), FFT_S)
        h = h_ref[:, hi, :].reshape(nb, 2 * FOUR_W).astype(BF16)
        p = jnp.dot(mc_ref[...], h, preferred_element_type=F32)
        q = jnp.dot(ms_ref[...], h, preferred_element_type=F32)
        ur = p[:, :FOUR_W] + q[:, FOUR_W:]
        ui = p[:, FOUR_W:] - q[:, :FOUR_W]
        out = _fold_groups(ur, ui, wf_ref, 1.0 / math.sqrt(T_LAT * F_CH))
        o_ref[...] = out.reshape(FFT_R, FFT_S, FOUR_W)


def _fft_consts():
    r, s = FFT_R, FFT_S
    nb = r * s
    ang = 2.0 * np.pi * np.outer(np.arange(r), np.arange(r)) / r
    c, sn = np.cos(ang), np.sin(ang)
    eye = np.eye(s)
    ma = np.concatenate([np.einsum('pb,ts->tpbs', c, eye).reshape(nb, nb),
                         np.einsum('pb,ts->tpbs', -sn, eye).reshape(nb, nb)], axis=0)
    mbc = np.einsum('pa,ts->ptas', c, eye).reshape(nb, nb)
    mbs = np.einsum('pa,ts->ptas', sn, eye).reshape(nb, nb)
    tw = 2.0 * np.pi * np.outer(np.arange(r), np.arange(r)).reshape(-1) / (r * r)
    twc = np.broadcast_to(np.cos(tw)[:, None], (r * r, LANE))
    tws = np.broadcast_to(np.sin(tw)[:, None], (r * r, LANE))
    bf = lambda a: jnp.asarray(a, F32).astype(BF16)
    return bf(ma), bf(mbc), bf(mbs), jnp.asarray(twc, F32), jnp.asarray(tws, F32)


def _four_lat(f, wfold_bf, layer, consts):
    ma, mbc, mbs, twc, tws = consts
    r, s = FFT_R, FFT_S
    nb = r * s
    nj = r // s
    f4 = f.reshape(N_LAT_B, r, r, FOUR_W)
    a_step = lambda j: jnp.minimum(j, nj - 1)
    b_step = lambda j: jnp.maximum(j - nj, 0)
    const = lambda b, j: (0, 0)
    out = pl.pallas_call(
        _fft_body,
        grid=(N_LAT_B, 2 * nj),
        in_specs=[pl.BlockSpec((None, r, s, FOUR_W), lambda b, j: (b, 0, a_step(j), 0)),
                  pl.BlockSpec((2 * nb, nb), const),
                  pl.BlockSpec((nb, LANE), lambda b, j: (a_step(j), 0)),
                  pl.BlockSpec((nb, LANE), lambda b, j: (a_step(j), 0)),
                  pl.BlockSpec((nb, nb), const),
                  pl.BlockSpec((nb, nb), const),
                  pl.BlockSpec((None, F_GROUPS, 2 * F_CH, F_CH), lambda b, j: (layer, 0, 0, 0))],
        out_specs=pl.BlockSpec((None, r, s, FOUR_W), lambda b, j: (b, 0, b_step(j), 0)),
        out_shape=jax.ShapeDtypeStruct((N_LAT_B, r, r, FOUR_W), F32),
        scratch_shapes=[pltpu.VMEM((r, r, 2 * FOUR_W), F32)],
        compiler_params=_cparams(("arbitrary", "arbitrary")),
        name="fft_lat",
    )(f4, ma, twc, tws, mbc, mbs, wfold_bf)
    return out.reshape(N_LAT_B * T_LAT, FOUR_W)


def _outproj_body(a_ref, f_ref, x_ref, gt_ref, sh_ref, sc_ref, gpost_ref, gpre_ref, wo_ref, *rest,
                  moe, group0, tiles_per_group):
    if moe:
        wr_ref, x1_ref, hn_ref, gates_ref, sel_ref, len_ref = rest
    else:
        x1_ref, hn_ref = rest
    grp = _group_of(pl.program_id(0), group0, tiles_per_group)
    gt, sh, sc = gt_ref[pl.ds(grp, 1), :], sh_ref[pl.ds(grp, 1), :], sc_ref[pl.ds(grp, 1), :]
    for half in range(TM_OUT // TM):
        rows = slice(half * TM, (half + 1) * TM)
        mix_in = jnp.concatenate([a_ref[rows, :], f_ref[rows, :].astype(BF16)], axis=1)
        mixed = jnp.dot(mix_in, wo_ref[...], preferred_element_type=F32)
        x1 = x_ref[rows, :] + gt * _rms(mixed, gpost_ref[...])
        x1_ref[rows, :] = x1
        hn = _rms(x1, gpre_ref[...]) * (1.0 + sc) + sh
        hn_ref[rows, :] = hn.astype(hn_ref.dtype)
        if moe:
            gates, sel = _route_top2(hn, wr_ref[...])
            gates_ref[rows, :] = gates
            sel_ref[rows, :] = sel.astype(sel_ref.dtype)
            group_rows = jnp.ceil(jnp.sum(sel, axis=0, keepdims=True) * (1.0 / GROUP)) * GROUP
            len_ref[half] = jnp.broadcast_to(group_rows, (N_GROUPS_PAD, E_PAD))


def _route_top2(hn, wr):
    wr_hi = wr.astype(BF16)
    wr_lo = (wr - wr_hi.astype(F32)).astype(BF16)
    hn_hi = hn.astype(BF16)
    hn_lo = (hn - hn_hi.astype(F32)).astype(BF16)
    hh = jnp.dot(hn_hi, jnp.concatenate([wr_hi, wr_lo], axis=1), preferred_element_type=F32)
    logits = hh[:, :E_PAD] + hh[:, E_PAD:] + jnp.dot(hn_lo, wr_hi, preferred_element_type=F32)
    lane = lax.broadcasted_iota(jnp.int32, logits.shape, 1).astype(F32)
    neg = jnp.float32(-jnp.inf)
    logits = jnp.where(lane < float(N_EXPERTS), logits, neg)
    m1 = jnp.max(logits, axis=-1, keepdims=True)
    i1 = jnp.min(jnp.where(logits == m1, lane, float(E_PAD)), axis=-1, keepdims=True)
    rest_l = jnp.where(lane == i1, neg, logits)
    m2 = jnp.max(rest_l, axis=-1, keepdims=True)
    i2 = jnp.min(jnp.where(rest_l == m2, lane, float(E_PAD)), axis=-1, keepdims=True)
    e2 = jnp.exp(m2 - m1)
    w1 = 1.0 / (1.0 + e2)
    w2 = e2 / (1.0 + e2)
    gates = jnp.where(lane == i1, w1, 0.0) + jnp.where(lane == i2, w2, 0.0)
    sel = jnp.where((lane == i1) | (lane == i2), 1.0, 0.0)
    return gates, sel


def _outproj(attn_o, four_o, x, mod, layer, g_post, g_ffn_pre, w_out_bf, w_router_pad, group0, tiles_per_group):
    rows = x.shape[0]
    moe = w_router_pad is not None
    row = lambda i: (i, 0)
    const = lambda i: (0, 0)
    modspec = lambda k: pl.BlockSpec((None, N_GROUPS_PAD, D), lambda i: (layer, 0, k))
    blocks_per_group = None if tiles_per_group is None else tiles_per_group * TM // TM_OUT
    in_specs = [pl.BlockSpec((TM_OUT, ATTN_W), row), pl.BlockSpec((TM_OUT, FOUR_W), row),
                pl.BlockSpec((TM_OUT, D), row),
                modspec(2), modspec(3), modspec(4),
                pl.BlockSpec((1, D), const), pl.BlockSpec((1, D), const), pl.BlockSpec((D, D), const)]
    args = [attn_o, four_o, x, mod, mod, mod, g_post.reshape(1, D), g_ffn_pre.reshape(1, D), w_out_bf]
    out_specs = [pl.BlockSpec((TM_OUT, D), row), pl.BlockSpec((TM_OUT, D), row)]
    out_shape = [jax.ShapeDtypeStruct((rows, D), F32), jax.ShapeDtypeStruct((rows, D), BF16)]
    if moe:
        in_specs.append(pl.BlockSpec((D, E_PAD), const))
        args.append(w_router_pad)
        out_specs += [pl.BlockSpec((TM_OUT, E_PAD), row)] * 2
        out_specs.append(pl.BlockSpec((TM_OUT // TM, N_GROUPS_PAD, E_PAD), lambda i: (i, 0, 0)))
        out_shape += [jax.ShapeDtypeStruct((rows, E_PAD), F32), jax.ShapeDtypeStruct((rows, E_PAD), BF16),
                      jax.ShapeDtypeStruct((rows // TM, N_GROUPS_PAD, E_PAD), F32)]
    return pl.pallas_call(
        functools.partial(_outproj_body, moe=moe, group0=group0, tiles_per_group=blocks_per_group),
        grid=(rows // TM_OUT,),
        in_specs=in_specs, out_specs=out_specs, out_shape=out_shape,
        compiler_params=_cparams(("arbitrary",)),
        name="outproj_moe" if moe else "outproj",
    )(*args)


def _swiglu(hn, wg, wu, wd):
    g = jnp.dot(hn, wg, preferred_element_type=F32)
    u = jnp.dot(hn, wu, preferred_element_type=F32)
    return jnp.dot((_silu(g) * u).astype(BF16), wd, preferred_element_type=F32)


def _ffn_dense_body(hn_ref, x1_ref, gt_ref, gpost_ref, wg_ref, wu_ref, wd_ref, o_ref, *, group0, tiles_per_group):
    grp = _group_of(pl.program_id(0), group0, tiles_per_group)
    ff = _swiglu(hn_ref[...], wg_ref[...], wu_ref[...], wd_ref[...])
    o_ref[...] = x1_ref[...] + gt_ref[pl.ds(grp, 1), :] * _rms(ff, gpost_ref[...])


def _ffn_dense(hn, x1, mod, layer, g_post, wg, wu, wd, group0, tiles_per_group):
    rows = x1.shape[0]
    row = lambda i: (i, 0)
    const = lambda i: (0, 0)
    resident = lambda shape: pl.BlockSpec(shape, const, pipeline_mode=pl.Buffered(1))
    return pl.pallas_call(
        functools.partial(_ffn_dense_body, group0=group0, tiles_per_group=tiles_per_group),
        grid=(rows // TM,),
        in_specs=[pl.BlockSpec((TM, D), row), pl.BlockSpec((TM, D), row),
                  pl.BlockSpec((None, N_GROUPS_PAD, D), lambda i: (layer, 0, 5)),
                  pl.BlockSpec((1, D), const),
                  resident((D, D_FF)), resident((D, D_FF)), resident((D_FF, D))],
        out_specs=pl.BlockSpec((TM, D), row),
        out_shape=jax.ShapeDtypeStruct((rows, D), F32),
        compiler_params=_cparams(("arbitrary",)),
        name="ffn_dense",
    )(hn, x1, mod, g_post.reshape(1, D), wg, wu, wd)


def _experts_body(te_ref, tn_ref, tnext_ref, xs_ref, wg_hbm, wu_hbm, wd_hbm, ys_ref, w32, wg_bf, wu_bf, wd_bf, sem):
    i = pl.program_id(0)
    e = te_ref[i]
    fresh = (i == 0) | (e != te_ref[jnp.maximum(i - 1, 0)])

    def fetch(expert, act):
        for j, hbm in enumerate((wg_hbm, wu_hbm, wd_hbm)):
            act(pltpu.make_async_copy(hbm.at[expert], w32.at[j], sem))

    @pl.when(i == 0)
    def _():
        fetch(e, lambda c: c.start())

    @pl.when(fresh)
    def _():
        fetch(e, lambda c: c.wait())
        wg_bf[...] = w32[0].astype(BF16)
        wu_bf[...] = w32[1].astype(BF16)
        wd_bf[...] = w32[2].astype(BF16)
        nxt = tnext_ref[i]

        @pl.when(nxt != e)
        def _():
            fetch(nxt, lambda c: c.start())

    n_real = tn_ref[i]

    @pl.when(n_real > 0)
    def _():
        ys_ref[...] = _swiglu(xs_ref[...].astype(BF16), wg_bf[...], wu_bf[...], wd_bf[...])

    @pl.when(n_real == 0)
    def _():
        ys_ref[...] = jnp.zeros_like(ys_ref)


def _experts(xs, tile_e, tile_n, tile_next, eg, eu, ed):
    assert D == D_FF_E
    tile = lambda i, te, tn, tx: (i, 0)
    return pl.pallas_call(
        _experts_body,
        grid_spec=pltpu.PrefetchScalarGridSpec(
            num_scalar_prefetch=3,
            grid=(N_TILES,),
            in_specs=[pl.BlockSpec((TMOE, D), tile)] + [pl.BlockSpec(memory_space=pl.ANY)] * 3,
            out_specs=pl.BlockSpec((TMOE, D), tile),
            scratch_shapes=[pltpu.VMEM((3, D, D), F32),
                            pltpu.VMEM((D, D_FF_E), BF16), pltpu.VMEM((D, D_FF_E), BF16),
                            pltpu.VMEM((D_FF_E, D), BF16), pltpu.SemaphoreType.DMA]),
        out_shape=jax.ShapeDtypeStruct((NP, D), F32),
        compiler_params=_cparams(("arbitrary",)),
        name="moe_experts",
    )(tile_e, tile_n, tile_next, xs, eg, eu, ed)


def _route_body(*refs, firsts):
    ns = len(firsts)
    gates_refs, sel_refs, len_refs = refs[:ns], refs[ns:2 * ns], refs[2 * ns:3 * ns]
    ltri_ref, w_ref, rho_ref, tmeta_ref, meta_ref, off_ref, run_ref = refs[3 * ns:]
    t = pl.program_id(0)

    def this_stream(stream_refs):
        val = stream_refs[0][...]
        for k in range(1, ns):
            val = jnp.where(t >= firsts[k], stream_refs[k][...], val)
        return val

    sel = this_stream(sel_refs)
    group_rows = jnp.ceil(jnp.sum(sel.astype(F32), axis=0, keepdims=True) * (1.0 / GROUP)) * GROUP

    @pl.when(t == 0)
    def _():
        cnt = sum(jnp.sum(r[...].reshape(-1, E_PAD), axis=0, keepdims=True) for r in len_refs) * (1.0 / N_GROUPS_PAD)
        off_ref[...] = _segment_meta(cnt, meta_ref)
        run_ref[...] = jnp.zeros_like(run_ref)

    lane1 = lax.broadcasted_iota(jnp.int32, group_rows.shape, 1).astype(F32)
    buf0 = jnp.zeros_like(group_rows)
    running = jnp.zeros((1, 1), F32)
    for e in range(N_EXPERTS):
        pick = lane1 == float(e)
        buf0 = jnp.where(pick, running, buf0)
        running = running + jnp.sum(jnp.where(pick, group_rows, 0.0), axis=-1, keepdims=True)
    slot0 = off_ref[...] + run_ref[...]
    run_ref[...] += group_rows
    row8 = lax.broadcasted_iota(jnp.int32, tmeta_ref.shape, 0)
    tmeta_ref[...] = jnp.where(row8 == 0, slot0, jnp.where(row8 == 1, group_rows,
                               jnp.where(row8 == 2, buf0, 0.0)))

    earlier = jnp.dot(ltri_ref[...], sel, preferred_element_type=F32)
    rho = earlier + buf0
    chosen = sel > 0
    r_lo = jnp.min(jnp.where(chosen, rho, float(R_TILE)), axis=-1, keepdims=True)
    r_hi = jnp.max(jnp.where(chosen, rho, -1.0), axis=-1, keepdims=True)
    g = this_stream(gates_refs)
    w_lo = jnp.sum(jnp.where(chosen & (rho == r_lo), g, 0.0), axis=-1, keepdims=True)
    w_hi = jnp.sum(jnp.where(chosen & (rho == r_hi), g, 0.0), axis=-1, keepdims=True)
    lane = lax.broadcasted_iota(jnp.int32, rho.shape, 1)
    w_ref[...] = jnp.where(lane == 2, w_lo, jnp.where(lane == 3, w_hi,
                           jnp.where(lane == 4, r_lo, jnp.where(lane == 5, r_hi, 0.0))))
    cols = jnp.where(lane == 0, r_lo, jnp.where(lane == 1, r_hi, 0.0))
    rho_ref[...] = jnp.transpose(cols)[0:N_GROUPS_PAD, :]


def _segment_meta(cnt, meta_ref):
    padded = jnp.ceil(cnt * (1.0 / TMOE)) * TMOE
    lane = lax.broadcasted_iota(jnp.int32, cnt.shape, 1).astype(F32)
    pad_of = lambda e: jnp.sum(jnp.where(lane == float(e), padded, 0.0), axis=-1, keepdims=True)
    following = [None] * N_EXPERTS
    seen = jnp.full((1, 1), -1.0, F32)
    for e in reversed(range(N_EXPERTS)):
        following[e] = seen
        seen = jnp.where(pad_of(e) > 0.0, float(e), seen)
    start = lane * TMOE
    off = jnp.zeros_like(cnt)
    tile_e = jnp.zeros_like(cnt)
    tile_n = jnp.zeros_like(cnt)
    tile_next = jnp.zeros_like(cnt)
    running = jnp.zeros((1, 1), F32)
    last_e = jnp.zeros((1, 1), F32)
    for e in range(N_EXPERTS):
        pick = lane == float(e)
        cnt_e = jnp.sum(jnp.where(pick, cnt, 0.0), axis=-1, keepdims=True)
        pad_e = pad_of(e)
        off = jnp.where(pick, running, off)
        in_seg = (start >= running) & (start < running + pad_e)
        tile_e = jnp.where(in_seg, float(e), tile_e)
        tile_n = jnp.where(in_seg, jnp.clip(cnt_e - (start - running), 0.0, float(TMOE)), tile_n)
        tile_next = jnp.where(in_seg, jnp.where(following[e] >= 0.0, following[e], float(e)), tile_next)
        last_e = jnp.where(pad_e > 0.0, float(e), last_e)
        running = running + pad_e
    tile_e = jnp.where(start >= running, last_e, tile_e)
    tile_next = jnp.where(start >= running, last_e, tile_next)
    row = lax.broadcasted_iota(jnp.int32, meta_ref.shape, 0)
    last_tile = jnp.where(lane < float(N_EXPERTS), off + padded - TMOE, running)
    last_tile = jnp.where((lane < float(N_EXPERTS)) & (padded == 0.0), -1.0, last_tile)
    meta_ref[...] = jnp.where(row == 0, tile_e, jnp.where(row == 1, tile_n,
                              jnp.where(row == 2, last_tile, jnp.where(row == 3, tile_next, 0.0))))
    return off


def _route(gates, sels, lens):
    tiles = [g.shape[0] // TM for g in gates]
    firsts = [int(f) for f in np.cumsum([0] + tiles)]
    n, rows = firsts[-1], firsts[-1] * TM
    ltri = jnp.asarray(np.tril(np.ones((TM, TM)), -1), F32).astype(BF16)
    stream_spec = lambda lo, cnt: pl.BlockSpec((TM, E_PAD), lambda t: (jnp.clip(t - lo, 0, cnt - 1), 0))
    stream_specs = [stream_spec(lo, cnt) for lo, cnt in zip(firsts[:-1], tiles)]
    whole = [pl.BlockSpec(ln.shape, lambda t: (0, 0, 0)) for ln in lens]
    per_tile = lambda t: (t, 0, 0)
    return pl.pallas_call(
        functools.partial(_route_body, firsts=tuple(firsts[:-1])),
        grid=(n,),
        in_specs=stream_specs + stream_specs + whole + [pl.BlockSpec((TM, TM), lambda t: (0, 0))],
        out_specs=[pl.BlockSpec((TM, E_PAD), lambda t: (t, 0)),
                   pl.BlockSpec((None, N_GROUPS_PAD, TM), per_tile),
                   pl.BlockSpec((None, N_GROUPS_PAD, E_PAD), per_tile),
                   pl.BlockSpec((N_GROUPS_PAD, E_PAD), lambda t: (0, 0))],
        out_shape=[jax.ShapeDtypeStruct((rows, E_PAD), F32),
                   jax.ShapeDtypeStruct((n, N_GROUPS_PAD, TM), F32),
                   jax.ShapeDtypeStruct((n, N_GROUPS_PAD, E_PAD), F32),
                   jax.ShapeDtypeStruct((N_GROUPS_PAD, E_PAD), F32)],
        scratch_shapes=[pltpu.VMEM((1, E_PAD), F32)] * 2,
        compiler_params=_cparams(("arbitrary",)),
        name="moe_route",
    )(*gates, *sels, *lens, ltri)


def _group_copies(tmeta_ref, slot_window, buf_window, sem, to_slots, act):
    piece = 0
    for e in range(N_EXPERTS):
        slot0, n, buf0 = tmeta_ref[0, e], tmeta_ref[1, e], tmeta_ref[2, e]
        done = jnp.int32(0)
        size = TM
        while size >= GROUP:
            take = (n - done) >= size
            s_at = pl.ds(pl.multiple_of(slot0 + done, GROUP), size)
            b_at = pl.ds(pl.multiple_of(buf0 + done, GROUP), size)
            src, dst = (buf_window(b_at), slot_window(s_at)) if to_slots else (slot_window(s_at), buf_window(b_at))

            @pl.when(take)
            def _(src=src, dst=dst, piece=piece):
                act(pltpu.make_async_copy(src, dst, sem), piece)

            done = done + jnp.where(take, size, 0)
            size //= 2
            piece += 1


def _start_alternating(copy, piece):
    copy.start(priority=piece % 2)


def _wait(copy, piece):
    del piece
    copy.wait()


def _dispatch_body(tmeta_ref, prev_tmeta_ref, pad_ref, rho_ref, *rest, firsts):
    ns = len(firsts)
    hn_refs, (xs_ref, pbuf, zeros, sems) = rest[:ns], rest[ns:]
    i = pl.program_id(0)
    cur = lax.rem(i, 2)
    sem = sems.at[0]

    def zero_fills(act):
        fills = []
        for e in range(N_EXPERTS):
            s = pad_ref[0, e]
            fills.append((s >= 0, pl.multiple_of(jnp.maximum(s, 0), TMOE)))
        for j in range(N_TILES - TOP_K * N_TOKENS // TMOE):
            s = NP - (j + 1) * TMOE
            fills.append((s >= pad_ref[0, N_EXPERTS], s))
        for live, s in fills:
            @pl.when(live)
            def _(s=s):
                act(pltpu.make_async_copy(zeros, xs_ref.at[pl.ds(s, TMOE), :], sem))

    @pl.when(i == 0)
    def _():
        zeros[...] = jnp.zeros_like(zeros)
        zero_fills(lambda c: c.start())

    hn = hn_refs[0][...]
    for k in range(1, ns):
        hn = jnp.where(i >= firsts[k], hn_refs[k][...], hn)
    row = lax.broadcasted_iota(jnp.int32, (R_TILE, TM), 0).astype(F32)
    onehot = jnp.where((row == rho_ref[0:1, :]) | (row == rho_ref[1:2, :]), 1.0, 0.0).astype(BF16)
    pbuf[cur] = jnp.dot(onehot, hn, preferred_element_type=F32)

    @pl.when(i == 0)
    def _():
        zero_fills(lambda c: c.wait())

    def store(meta_ref, half, act):
        _group_copies(meta_ref, lambda at: xs_ref.at[at, :], lambda at: pbuf.at[half, at, :], sems.at[half],
                      True, act)

    store(tmeta_ref, cur, _start_alternating)

    @pl.when(i > 0)
    def _():
        store(prev_tmeta_ref, 1 - cur, _wait)

    @pl.when(i + 1 == pl.num_programs(0))
    def _():
        store(tmeta_ref, cur, _wait)


def _dispatch(hns, tmeta, rho, pad_starts):
    tiles = [h.shape[0] // TM for h in hns]
    firsts = [int(f) for f in np.cumsum([0] + tiles)]
    n = firsts[-1]
    stream_spec = lambda lo, cnt: pl.BlockSpec((TM, D), lambda i: (jnp.clip(i - lo, 0, cnt - 1), 0))
    return pl.pallas_call(
        functools.partial(_dispatch_body, firsts=tuple(firsts[:-1])),
        grid=(n,),
        in_specs=[pl.BlockSpec((None, N_GROUPS_PAD, E_PAD), lambda i: (i, 0, 0), memory_space=pltpu.SMEM),
                  pl.BlockSpec((None, N_GROUPS_PAD, E_PAD), lambda i: (jnp.maximum(i - 1, 0), 0, 0),
                               memory_space=pltpu.SMEM),
                  pl.BlockSpec(memory_space=pltpu.SMEM),
                  pl.BlockSpec((None, N_GROUPS_PAD, TM), lambda i: (i, 0, 0))]
                 + [stream_spec(lo, cnt) for lo, cnt in zip(firsts[:-1], tiles)],
        out_specs=pl.BlockSpec(memory_space=pl.ANY),
        out_shape=jax.ShapeDtypeStruct((NP, D), F32),
        scratch_shapes=[pltpu.VMEM((2, R_TILE, D), F32), pltpu.VMEM((TMOE, D), F32),
                        pltpu.SemaphoreType.DMA((2,))],
        compiler_params=_cparams(("arbitrary",)),
        name="moe_dispatch",
    )(tmeta, tmeta, pad_starts, rho, *hns)


def _combine_body(tmeta_ref, next_tmeta_ref, w_ref, x1_ref, gt_ref, gpost_ref, ys_ref, o_ref, ybuf, sems,
                   *, group0, tiles_per_group):
    i = pl.program_id(0)
    grp = _group_of(i, group0, tiles_per_group)
    cur = lax.rem(i, 2)

    def fetch(meta_ref, half, act):
        _group_copies(meta_ref, lambda at: ys_ref.at[at, :], lambda at: ybuf.at[half, at, :], sems.at[half],
                      False, act)

    @pl.when(i == 0)
    def _():
        ybuf[...] = jnp.zeros_like(ybuf)
        fetch(tmeta_ref, 0, _start_alternating)

    @pl.when(i + 1 < pl.num_programs(0))
    def _():
        fetch(next_tmeta_ref, 1 - cur, _start_alternating)

    fetch(tmeta_ref, cur, _wait)
    w = w_ref[...]
    lane = lax.broadcasted_iota(jnp.int32, w.shape, 1)
    col = lambda j: jnp.sum(jnp.where(lane == j, w, 0.0), axis=-1, keepdims=True)
    w_lo, w_hi, r_lo, r_hi = col(2), col(3), col(4), col(5)
    y = ybuf[cur].astype(BF16)
    pos = lax.broadcasted_iota(jnp.int32, (TM, R_TILE), 1).astype(F32)
    pick = lambda r: jnp.dot(jnp.where(pos == r, 1.0, 0.0).astype(BF16), y, preferred_element_type=F32)
    ff = w_lo * pick(r_lo) + w_hi * pick(r_hi)
    o_ref[...] = x1_ref[...] + gt_ref[pl.ds(grp, 1), :] * _rms(ff, gpost_ref[...])


def _combine(ys, tmeta, slot_w, tile0, x1, mod, layer, g_post, group0, tiles_per_group):
    rows = x1.shape[0]
    n = rows // TM
    row = lambda i: (i, 0)
    return pl.pallas_call(
        functools.partial(_combine_body, group0=group0, tiles_per_group=tiles_per_group),
        grid=(n,),
        in_specs=[pl.BlockSpec((None, N_GROUPS_PAD, E_PAD), lambda i: (tile0 + i, 0, 0), memory_space=pltpu.SMEM),
                  pl.BlockSpec((None, N_GROUPS_PAD, E_PAD), lambda i: (tile0 + jnp.minimum(i + 1, n - 1), 0, 0),
                               memory_space=pltpu.SMEM),
                  pl.BlockSpec((TM, E_PAD), lambda i: (tile0 + i, 0)), pl.BlockSpec((TM, D), row),
                  pl.BlockSpec((None, N_GROUPS_PAD, D), lambda i: (layer, 0, 5)),
                  pl.BlockSpec((1, D), lambda i: (0, 0)),
                  pl.BlockSpec(memory_space=pl.ANY)],
        out_specs=pl.BlockSpec((TM, D), row),
        out_shape=jax.ShapeDtypeStruct((rows, D), F32),
        scratch_shapes=[pltpu.VMEM((2, R_TILE, D), F32), pltpu.SemaphoreType.DMA((2,))],
        compiler_params=_cparams(("arbitrary",)),
        name="moe_combine",
    )(tmeta, tmeta, slot_w, x1, mod, g_post.reshape(1, D), ys)


def kernel(x_prompt, x_sample, cache_k, cache_v, c, c_ctx, g_attn_pre, g_attn_post, g_ffn_pre, g_ffn_post,
           w_ada, b_ada, w_in, lam_params, g_subln, w_fnet, w_out, w_gate, w_up, w_down, w_router,
           e_gate, e_up, e_down):
    assert x_prompt.shape == (N_CTX_B, T_CTX, D) and x_sample.shape == (N_LAT_B, T_LAT, D)
    assert cache_k.shape == (N_LAT_B, DEPTH, T_PAST, N_HEADS, 2, QK_DIM)

    cond = jnp.zeros((N_GROUPS_PAD, D), F32).at[0].set(c_ctx).at[1:1 + N_LAT_B].set(c)
    mod = _ada(cond, w_ada, b_ada)
    wfold_bf = _wfold(w_fnet).astype(BF16)
    rope_tabs = _rope_tables()
    fft_consts = _fft_consts()

    w_in_bf = w_in.astype(BF16)
    w_out_bf = w_out.astype(BF16)
    w_router_pad = jnp.zeros((DEPTH // 2, D, E_PAD), F32).at[:, :, :N_EXPERTS].set(w_router)
    ck_bf = cache_k.reshape(N_LAT_B, DEPTH, T_PAST, ATTN_W).astype(BF16)
    cv_bf = cache_v.reshape(N_LAT_B, DEPTH, T_PAST, ATTN_W).astype(BF16)

    lat_tiles = T_LAT // TM
    streams = [dict(x=x_prompt.reshape(N_CTX_B * T_CTX, D), group0=0, tiles=None, lat=False),
               dict(x=x_sample.reshape(N_LAT_B * T_LAT, D), group0=1, tiles=lat_tiles, lat=True)]
    new_k = new_v = None
    for l in range(DEPTH):
        lam_init = 0.8 - 0.6 * math.exp(-0.3 * l)
        i = l // 2
        routed = []
        for s in streams:
            g0, tiles = s["group0"], s["tiles"]
            if s["lat"]:
                qkv, f = _inproj(s["x"], mod, l, g_attn_pre[l], w_in_bf[l], rope_tabs, g0, tiles)
                attn_o = _attn_lat(qkv, ck_bf[:, l], cv_bf[:, l], lam_params, g_subln, l, lam_init)
                attn_o = attn_o.reshape(N_LAT_B * T_LAT, ATTN_W)
                four_o = _four_lat(f, wfold_bf, l, fft_consts)
            else:
                qkv, f, new_k, new_v = _inproj(s["x"], mod, l, g_attn_pre[l], w_in_bf[l], None, g0, tiles,
                                               cache=None if l == 0 else (new_k, new_v))
                attn_o = _attn_ctx(qkv, lam_params, g_subln, l, lam_init).reshape(N_CTX_B * T_CTX, ATTN_W)
                four_o = _four_ctx(f, wfold_bf, l).reshape(N_CTX_B * T_CTX, FOUR_W)
            if l % 2 == 0:
                x1, hn = _outproj(attn_o, four_o, s["x"], mod, l, g_attn_post[l], g_ffn_pre[l], w_out_bf[l],
                                  None, g0, tiles)
                s["x"] = _ffn_dense(hn, x1, mod, l, g_ffn_post[l], w_gate[i].astype(BF16), w_up[i].astype(BF16),
                                    w_down[i].astype(BF16), g0, tiles)
            else:
                routed.append(_outproj(attn_o, four_o, s["x"], mod, l, g_attn_post[l], g_ffn_pre[l],
                                       w_out_bf[l], w_router_pad[i], g0, tiles))
        if l % 2 == 1:
            slot_w, rho, tmeta, meta = _route(*[[r[k] for r in routed] for k in (2, 3, 4)])
            tmeta = tmeta.astype(jnp.int32)
            tile_e = meta[0, :N_TILES].astype(jnp.int32)
            tile_n = meta[1, :N_TILES].astype(jnp.int32)
            tile_next = meta[3, :N_TILES].astype(jnp.int32)
            pad_starts = meta[2:3, :N_EXPERTS + 1].astype(jnp.int32)
            bounds = np.cumsum([0] + [r[0].shape[0] for r in routed])
            xs = _dispatch([r[1] for r in routed], tmeta, rho, pad_starts)
            ys = _experts(xs, tile_e, tile_n, tile_next, e_gate[i], e_up[i], e_down[i])
            for s, r, lo in zip(streams, routed, bounds[:-1]):
                s["x"] = _combine(ys, tmeta, slot_w, int(lo) // TM, r[0], mod, l, g_ffn_post[l],
                                   s["group0"], s["tiles"])

    y_prompt = streams[0]["x"].reshape(N_CTX_B, T_CTX, D)
    y_sample = streams[1]["x"].reshape(N_LAT_B, T_LAT, D)
    new_cache_k = new_k.reshape(N_CTX_B, DEPTH, T_CTX, N_HEADS, 2, QK_DIM)
    new_cache_v = new_v.reshape(N_CTX_B, DEPTH, T_CTX, N_HEADS, HEAD_W)
    return (y_prompt, y_sample, new_cache_k, new_cache_v)
```
